```python
import jax, jax.numpy as jnp
from jax import lax
import numpy as np

D_MODEL = 2048
BATCH = 8
SEQ = 4096
DEPTH = 4

CTX_LEN = 256
GRID_W = 64
EPS = 1e-6
F_MIN = 1e-30
N_MOD = 6

HEAD_DIM = 128
ATTN_HEADS = 8
ATTN_KV_HEADS = 2
ATTN_GROUP = ATTN_HEADS // ATTN_KV_HEADS
ATTN_Q_BLOCK = 128
ROPE_THETA = 10000.0
ATTN_Q_W = ATTN_HEADS * HEAD_DIM
ATTN_KV_W = ATTN_KV_HEADS * HEAD_DIM

HG_HEADS = 4
HG_DK = 128
HG_DV = 128
HG_CHUNK = 16
HG_K_W = HG_HEADS * HG_DK
HG_V_W = HG_HEADS * HG_DV

SG_GROUPS = 4
SG_DIM = 128
SG_CHUNK = 128
SG_W = SG_GROUPS * SG_DIM

D_MIX = ATTN_Q_W + HG_V_W + SG_W
IN_SIZES = (ATTN_Q_W, ATTN_KV_W, ATTN_KV_W, HG_K_W, HG_K_W, HG_K_W, HG_V_W, HG_V_W, SG_W, SG_W)
IN_COLS = 5120

D_FF = 5632
CONV_W = 3

kernel_name = "hybrid_parallel_mixer_dit_block"


def rms_norm(x, g):
    xf = x.astype(jnp.float32)
    y = xf * lax.rsqrt(jnp.mean(xf * xf, axis=-1, keepdims=True) + EPS)
    return (y * g.astype(jnp.float32)).astype(x.dtype)


def axial_rope_tables(n_tokens):
    rows = n_tokens // GRID_W
    row = jnp.repeat(jnp.arange(rows, dtype=jnp.float32), GRID_W)
    col = jnp.tile(jnp.arange(GRID_W, dtype=jnp.float32), rows)
    n_freq = HEAD_DIM // 4
    inv = ROPE_THETA ** (-jnp.arange(n_freq, dtype=jnp.float32) / n_freq)
    ang = jnp.concatenate([row[:, None] * inv, col[:, None] * inv], axis=-1)
    return jnp.cos(ang), jnp.sin(ang)


def apply_rope(x, cos, sin):
    xf = x.astype(jnp.float32).reshape(*x.shape[:-1], HEAD_DIM // 2, 2)
    x1, x2 = xf[..., 0], xf[..., 1]
    cs, sn = cos[None, :, None, :], sin[None, :, None, :]
    out = jnp.stack([x1 * cs - x2 * sn, x1 * sn + x2 * cs], axis=-1)
    return out.reshape(x.shape).astype(x.dtype)


def gqa_softmax(q, k, v):
    s = jnp.einsum('bqkgd,bskd->bkgqs', q, k).astype(jnp.float32) * (HEAD_DIM ** -0.5)
    p = jax.nn.softmax(s, axis=-1).astype(v.dtype)
    return jnp.einsum('bkgqs,bskd->bqkgd', p, v)


def latent_attention(q, k_all, v_all):
    B, T, H, Dh = q.shape
    nb = T // ATTN_Q_BLOCK
    qb = q.reshape(B, nb, ATTN_Q_BLOCK, ATTN_KV_HEADS, ATTN_GROUP, Dh).transpose(1, 0, 2, 3, 4, 5)
    o = lax.map(lambda qblk: gqa_softmax(qblk, k_all, v_all), qb)
    return o.transpose(1, 0, 2, 3, 4, 5).reshape(B, T, H * Dh)


def context_attention(q, k, v):
    B, L, H, Dh = q.shape
    o = gqa_softmax(q.reshape(B, L, ATTN_KV_HEADS, ATTN_GROUP, Dh), k, v)
    return o.reshape(B, L, H * Dh)


def hgrn_lower_bounds(lb_param):
    p = jax.nn.softmax(lb_param.astype(jnp.float32), axis=1)
    return jnp.cumsum(p, axis=1) - p[:, :1]


def hgrn2_gates(f_raw, lb):
    z = f_raw.astype(jnp.float32)
    lb = lb.reshape(HG_HEADS, HG_DK)
    f = lb + (1.0 - lb) * jax.nn.sigmoid(z)
    log_f = jnp.log(jnp.maximum(f, F_MIN))
    k = (1.0 - lb) * jax.nn.sigmoid(-z)
    return log_f, k


def gla_chunkwise(q, k, v, log_f, s0):
    B, T, H, K = q.shape
    V = v.shape[-1]
    C = HG_CHUNK
    N = T // C
    q, k, log_f = [a.reshape(B, N, C, H, K) for a in (q, k, log_f)]
    v = v.reshape(B, N, C, H, V)
    b = jnp.cumsum(log_f, axis=2)
    tri = jnp.tril(jnp.ones((C, C), dtype=bool))[None, None, :, :, None, None]
    diff = b[:, :, :, None] - b[:, :, None, :]
    decay = jnp.where(tri, jnp.exp(jnp.where(tri, diff, 0.0)), 0.0)
    scores = jnp.einsum('bnthk,bnshk,bntshk->bnhts', q, k, decay)
    o_intra = jnp.einsum('bnhts,bnshv->bnthv', scores, v)
    b_last = b[:, :, -1]
    q_dec = q * jnp.exp(b)
    k_dec = k * jnp.exp(b_last[:, :, None] - b)

    def step(S, xs):
        qd, kd, vv, dl = xs
        o = jnp.einsum('bthk,bhkv->bthv', qd, S)
        S = S * dl[..., None] + jnp.einsum('bthk,bthv->bhkv', kd, vv)
        return S, o

    xs = tuple(jnp.moveaxis(a, 1, 0) for a in (q_dec, k_dec, v, jnp.exp(b_last)))
    s_fin, o_inter = lax.scan(step, s0, xs)
    o = o_intra + jnp.moveaxis(o_inter, 0, 1)
    return o.reshape(B, T, H, V), s_fin


def hgrn2_direction(q_l, i_l, f_l, q_c, i_c, f_c, lb):
    B = q_l.shape[0]
    s0 = jnp.zeros((B, HG_HEADS, HG_DK, HG_DV), jnp.float32)
    logf_c, k_c = hgrn2_gates(f_c, lb)
    o_c, s_ctx = gla_chunkwise(q_c, k_c, i_c, logf_c, s0)
    logf_l, k_l = hgrn2_gates(f_l, lb)
    o_l, _ = gla_chunkwise(q_l, k_l, i_l, logf_l, s_ctx)
    return o_l, o_c


def spatial_gating(u, v, norm_g, w_s, b_s):
    B, T, _ = u.shape
    N = T // SG_CHUNK
    u = jax.nn.gelu(u)
    v = rms_norm(jax.nn.gelu(v).reshape(B, T, SG_GROUPS, SG_DIM), norm_g.reshape(SG_GROUPS, SG_DIM))
    v = v.reshape(B, N, SG_CHUNK, SG_GROUPS, SG_DIM)
    mixed = jnp.einsum('gts,bnsgd->bntgd', w_s, v) + b_s.T[:, :, None]
    return u * mixed.reshape(B, T, SG_W)


def conv_ffn(h, w_up, conv_w, conv_b, w_down):
    T = h.shape[1]
    up = h @ w_up
    pad = jnp.pad(up, ((0, 0), (CONV_W // 2, CONV_W // 2), (0, 0)))
    y = conv_b + sum(pad[:, j:j + T] * conv_w[j] for j in range(CONV_W))
    gate, val = jnp.split(y, 2, axis=-1)
    return (jax.nn.silu(gate) * val) @ w_down


def token_mixers(h, hc, w_in, q_g, k_g, lb_f, lb_b, hg_g, sg_g, sg_w, sg_b, cos, sin, need_ctx):
    B, T, _ = h.shape
    L = hc.shape[1]
    splits = np.cumsum(IN_SIZES)[:-1].tolist()
    (aq, ak, av, hq, hff, hfb, hi, hgt, su, sv) = jnp.split(h @ w_in, splits, axis=-1)
    (aqc, akc, avc, hqc, hffc, hfbc, hic, hgtc, suc, svc) = jnp.split(hc @ w_in, splits, axis=-1)

    q = apply_rope(rms_norm(aq.reshape(B, T, ATTN_HEADS, HEAD_DIM), q_g), cos, sin)
    k = apply_rope(rms_norm(ak.reshape(B, T, ATTN_KV_HEADS, HEAD_DIM), k_g), cos, sin)
    v = av.reshape(B, T, ATTN_KV_HEADS, HEAD_DIM)
    kc = rms_norm(akc.reshape(B, L, ATTN_KV_HEADS, HEAD_DIM), k_g)
    vc = avc.reshape(B, L, ATTN_KV_HEADS, HEAD_DIM)
    attn = latent_attention(q, jnp.concatenate([k, kc], axis=1), jnp.concatenate([v, vc], axis=1))

    def heads(a, d):
        return a.astype(jnp.float32).reshape(a.shape[0], a.shape[1], -1, d)
    flip = lambda a: jnp.flip(a, axis=1)
    q_l, i_l = jax.nn.silu(heads(hq, HG_DK)), heads(hi, HG_DV)
    q_c, i_c = jax.nn.silu(heads(hqc, HG_DK)), heads(hic, HG_DV)
    ff_l, fb_l, ff_c, fb_c = heads(hff, HG_DK), heads(hfb, HG_DK), heads(hffc, HG_DK), heads(hfbc, HG_DK)
    o_f, oc_f = hgrn2_direction(q_l, i_l, ff_l, q_c, i_c, ff_c, lb_f)
    o_b, oc_b = hgrn2_direction(flip(q_l), flip(i_l), flip(fb_l), flip(q_c), flip(i_c), flip(fb_c), lb_b)
    hg_out = (rms_norm(o_f + flip(o_b), hg_g).reshape(B, T, HG_V_W)
              * jax.nn.silu(hgt.astype(jnp.float32))).astype(h.dtype)

    sg = spatial_gating(su, sv, sg_g, sg_w, sg_b)

    mix = jnp.concatenate([attn, hg_out, sg], axis=-1)
    if not need_ctx:
        return mix, None
    qc = rms_norm(aqc.reshape(B, L, ATTN_HEADS, HEAD_DIM), q_g)
    attn_c = context_attention(qc, kc, vc)
    hg_c = (rms_norm(oc_f + flip(oc_b), hg_g).reshape(B, L, HG_V_W)
            * jax.nn.silu(hgtc.astype(jnp.float32))).astype(hc.dtype)
    sg_c = spatial_gating(suc, svc, sg_g, sg_w, sg_b)
    mix_c = jnp.concatenate([attn_c, hg_c, sg_c], axis=-1)
    return mix, mix_c


def _fwd_setup_inputs(seed: int = 0) -> dict:
    key = jax.random.key(seed)
    ks = jax.random.split(key, 24)
    nrm = lambda k, shape, s: jax.random.normal(k, shape, jnp.float32) * s
    L = DEPTH
    return {
        "x": nrm(ks[0], (BATCH, SEQ, D_MODEL), 1.0),
        "c": nrm(ks[1], (BATCH, D_MODEL), 1.0),
        "ctx": nrm(ks[2], (BATCH, CTX_LEN, D_MODEL), 1.0),
        "c_ctx": nrm(ks[3], (D_MODEL,), 1.0),
        "w_ada": nrm(ks[4], (L, D_MODEL, N_MOD * D_MODEL), 0.5 * D_MODEL ** -0.5),
        "b_ada": nrm(ks[5], (L, N_MOD * D_MODEL), 0.02),
        "norm1_g": 1.0 + nrm(ks[6], (L, D_MODEL), 0.02),
        "w_in": nrm(ks[7], (L, D_MODEL, IN_COLS), D_MODEL ** -0.5),
        "q_norm_g": 1.0 + nrm(ks[8], (L, HEAD_DIM), 0.02),
        "k_norm_g": 1.0 + nrm(ks[9], (L, HEAD_DIM), 0.02),
        "hg_lower_bounds": nrm(ks[10], (2, L, HG_K_W), 0.1),
        "hg_norm_g": 1.0 + nrm(ks[11], (L, HG_DV), 0.02),
        "sg_norm_g": 1.0 + nrm(ks[12], (L, SG_W), 0.02),
        "sg_w": nrm(ks[13], (L, SG_GROUPS, SG_CHUNK, SG_CHUNK), SG_CHUNK ** -0.5),
        "sg_b": 1.0 + nrm(ks[14], (L, SG_GROUPS, SG_CHUNK), 0.02),
        "w_out": nrm(ks[15], (L, D_MIX, D_MODEL), D_MIX ** -0.5),
        "norm2_g": 1.0 + nrm(ks[16], (L, D_MODEL), 0.02),
        "w_up": nrm(ks[17], (L, D_MODEL, 2 * D_FF), D_MODEL ** -0.5),
        "conv_w": nrm(ks[18], (L, CONV_W, 2 * D_FF), CONV_W ** -0.5),
        "conv_b": nrm(ks[19], (L, 2 * D_FF), 0.02),
        "w_down": nrm(ks[20], (L, D_FF, D_MODEL), D_FF ** -0.5),
        "final_norm_g": 1.0 + nrm(ks[21], (D_MODEL,), 0.02),
    }


def _fwd_reference(x, c, ctx, c_ctx, w_ada, b_ada, norm1_g, w_in, q_norm_g, k_norm_g,
              hg_lower_bounds, hg_norm_g, sg_norm_g, sg_w, sg_b, w_out,
              norm2_g, w_up, conv_w, conv_b, w_down, final_norm_g):
    B, T, _ = x.shape
    cos, sin = axial_rope_tables(T)
    lbs = hgrn_lower_bounds(hg_lower_bounds)
    silu_c = jax.nn.silu(c)
    silu_cc = jax.nn.silu(c_ctx)
    cx = ctx
    for l in range(DEPTH):
        need_ctx = l < DEPTH - 1
        mod = (silu_c @ w_ada[l] + b_ada[l]).reshape(B, N_MOD, 1, D_MODEL)
        mod_c = (silu_cc @ w_ada[l] + b_ada[l]).reshape(N_MOD, D_MODEL)

        h = rms_norm(x, norm1_g[l]) * (1.0 + mod[:, 1]) + mod[:, 0]
        hc = rms_norm(cx, norm1_g[l]) * (1.0 + mod_c[1]) + mod_c[0]
        mix, mix_c = token_mixers(h, hc, w_in[l], q_norm_g[l], k_norm_g[l], lbs[0, l], lbs[1, l],
                                  hg_norm_g[l], sg_norm_g[l], sg_w[l], sg_b[l], cos, sin, need_ctx)
        x = x + mod[:, 2] * (mix @ w_out[l])

        h2 = rms_norm(x, norm2_g[l]) * (1.0 + mod[:, 4]) + mod[:, 3]
        x = x + mod[:, 5] * conv_ffn(h2, w_up[l], conv_w[l], conv_b[l], w_down[l])

        if need_ctx:
            cx = cx + mod_c[2] * (mix_c @ w_out[l])
            hc2 = rms_norm(cx, norm2_g[l]) * (1.0 + mod_c[4]) + mod_c[3]
            cx = cx + mod_c[5] * conv_ffn(hc2, w_up[l], conv_w[l], conv_b[l], w_down[l])
    return rms_norm(x, final_norm_g)


import jax as _jax
import jax.numpy as _jnp

TWIN_FORMAT = 'train_step'
FWD_PARAMS = ['x', 'c', 'ctx', 'c_ctx', 'w_ada', 'b_ada', 'norm1_g', 'w_in', 'q_norm_g', 'k_norm_g', 'hg_lower_bounds', 'hg_norm_g', 'sg_norm_g', 'sg_w', 'sg_b', 'w_out', 'norm2_g', 'w_up', 'conv_w', 'conv_b', 'w_down', 'final_norm_g']
TWIN_WEIGHTS = ['c_ctx', 'w_ada', 'b_ada', 'norm1_g', 'w_in', 'q_norm_g', 'k_norm_g', 'hg_lower_bounds', 'hg_norm_g', 'sg_norm_g', 'sg_w', 'sg_b', 'w_out', 'norm2_g', 'w_up', 'conv_w', 'conv_b', 'w_down', 'final_norm_g']
TWIN_DIFF_INPUT = 'x'
TWIN_INPUTS = ['x', 'c', 'ctx', 'c_ctx', 'w_ada', 'b_ada', 'norm1_g', 'w_in', 'q_norm_g', 'k_norm_g', 'hg_lower_bounds', 'hg_norm_g', 'sg_norm_g', 'sg_w', 'sg_b', 'w_out', 'norm2_g', 'w_up', 'conv_w', 'conv_b', 'w_down', 'final_norm_g', 'loss_target', 'm_c_ctx', 'm_w_ada', 'm_b_ada', 'm_norm1_g', 'm_w_in', 'm_q_norm_g', 'm_k_norm_g', 'm_hg_lower_bounds', 'm_hg_norm_g', 'm_sg_norm_g', 'm_sg_w', 'm_sg_b', 'm_w_out', 'm_norm2_g', 'm_w_up', 'm_conv_w', 'm_conv_b', 'm_w_down', 'm_final_norm_g', 'v_c_ctx', 'v_w_ada', 'v_b_ada', 'v_norm1_g', 'v_w_in', 'v_q_norm_g', 'v_k_norm_g', 'v_hg_lower_bounds', 'v_hg_norm_g', 'v_sg_norm_g', 'v_sg_w', 'v_sg_b', 'v_w_out', 'v_norm2_g', 'v_w_up', 'v_conv_w', 'v_conv_b', 'v_w_down', 'v_final_norm_g']
TWIN_OUTPUTS = ['loss', 'grad_x', 'grad_c_ctx', 'grad_w_ada', 'grad_b_ada', 'grad_norm1_g', 'grad_w_in', 'grad_q_norm_g', 'grad_k_norm_g', 'grad_hg_lower_bounds', 'grad_hg_norm_g', 'grad_sg_norm_g', 'grad_sg_w', 'grad_sg_b', 'grad_w_out', 'grad_norm2_g', 'grad_w_up', 'grad_conv_w', 'grad_conv_b', 'grad_w_down', 'grad_final_norm_g', 'delta_c_ctx', 'delta_w_ada', 'delta_b_ada', 'delta_norm1_g', 'delta_w_in', 'delta_q_norm_g', 'delta_k_norm_g', 'delta_hg_lower_bounds', 'delta_hg_norm_g', 'delta_sg_norm_g', 'delta_sg_w', 'delta_sg_b', 'delta_w_out', 'delta_norm2_g', 'delta_w_up', 'delta_conv_w', 'delta_conv_b', 'delta_w_down', 'delta_final_norm_g', 'new_m_c_ctx', 'new_m_w_ada', 'new_m_b_ada', 'new_m_norm1_g', 'new_m_w_in', 'new_m_q_norm_g', 'new_m_k_norm_g', 'new_m_hg_lower_bounds', 'new_m_hg_norm_g', 'new_m_sg_norm_g', 'new_m_sg_w', 'new_m_sg_b', 'new_m_w_out', 'new_m_norm2_g', 'new_m_w_up', 'new_m_conv_w', 'new_m_conv_b', 'new_m_w_down', 'new_m_final_norm_g', 'new_v_c_ctx', 'new_v_w_ada', 'new_v_b_ada', 'new_v_norm1_g', 'new_v_w_in', 'new_v_q_norm_g', 'new_v_k_norm_g', 'new_v_hg_lower_bounds', 'new_v_hg_norm_g', 'new_v_sg_norm_g', 'new_v_sg_w', 'new_v_sg_b', 'new_v_w_out', 'new_v_norm2_g', 'new_v_w_up', 'new_v_conv_w', 'new_v_conv_b', 'new_v_w_down', 'new_v_final_norm_g']
TWIN_LEAF_KINDS = {'loss': 'loss', 'grad_x': 'grad_x', 'grad_c_ctx': 'grad_w', 'grad_w_ada': 'grad_w', 'grad_b_ada': 'grad_w', 'grad_norm1_g': 'grad_w', 'grad_w_in': 'grad_w', 'grad_q_norm_g': 'grad_w', 'grad_k_norm_g': 'grad_w', 'grad_hg_lower_bounds': 'grad_w', 'grad_hg_norm_g': 'grad_w', 'grad_sg_norm_g': 'grad_w', 'grad_sg_w': 'grad_w', 'grad_sg_b': 'grad_w', 'grad_w_out': 'grad_w', 'grad_norm2_g': 'grad_w', 'grad_w_up': 'grad_w', 'grad_conv_w': 'grad_w', 'grad_conv_b': 'grad_w', 'grad_w_down': 'grad_w', 'grad_final_norm_g': 'grad_w', 'delta_c_ctx': 'delta_w', 'delta_w_ada': 'delta_w', 'delta_b_ada': 'delta_w', 'delta_norm1_g': 'delta_w', 'delta_w_in': 'delta_w', 'delta_q_norm_g': 'delta_w', 'delta_k_norm_g': 'delta_w', 'delta_hg_lower_bounds': 'delta_w', 'delta_hg_norm_g': 'delta_w', 'delta_sg_norm_g': 'delta_w', 'delta_sg_w': 'delta_w', 'delta_sg_b': 'delta_w', 'delta_w_out': 'delta_w', 'delta_norm2_g': 'delta_w', 'delta_w_up': 'delta_w', 'delta_conv_w': 'delta_w', 'delta_conv_b': 'delta_w', 'delta_w_down': 'delta_w', 'delta_final_norm_g': 'delta_w', 'new_m_c_ctx': 'new_m', 'new_m_w_ada': 'new_m', 'new_m_b_ada': 'new_m', 'new_m_norm1_g': 'new_m', 'new_m_w_in': 'new_m', 'new_m_q_norm_g': 'new_m', 'new_m_k_norm_g': 'new_m', 'new_m_hg_lower_bounds': 'new_m', 'new_m_hg_norm_g': 'new_m', 'new_m_sg_norm_g': 'new_m', 'new_m_sg_w': 'new_m', 'new_m_sg_b': 'new_m', 'new_m_w_out': 'new_m', 'new_m_norm2_g': 'new_m', 'new_m_w_up': 'new_m', 'new_m_conv_w': 'new_m', 'new_m_conv_b': 'new_m', 'new_m_w_down': 'new_m', 'new_m_final_norm_g': 'new_m', 'new_v_c_ctx': 'new_v', 'new_v_w_ada': 'new_v', 'new_v_b_ada': 'new_v', 'new_v_norm1_g': 'new_v', 'new_v_w_in': 'new_v', 'new_v_q_norm_g': 'new_v', 'new_v_k_norm_g': 'new_v', 'new_v_hg_lower_bounds': 'new_v', 'new_v_hg_norm_g': 'new_v', 'new_v_sg_norm_g': 'new_v', 'new_v_sg_w': 'new_v', 'new_v_sg_b': 'new_v', 'new_v_w_out': 'new_v', 'new_v_norm2_g': 'new_v', 'new_v_w_up': 'new_v', 'new_v_conv_w': 'new_v', 'new_v_conv_b': 'new_v', 'new_v_w_down': 'new_v', 'new_v_final_norm_g': 'new_v'}


def _forward(args):
    return _fwd_reference(*[args[k] for k in FWD_PARAMS])


def _output_shape():
    def fwd():
        inp = _fwd_setup_inputs(0)
        return _fwd_reference(*[inp[k] for k in FWD_PARAMS])
    out = _jax.eval_shape(fwd)
    return out.shape, out.dtype

N_MICROBATCH = 1
ADAM_LR = 0.001
ADAM_B1 = 0.9
ADAM_B2 = 0.999
ADAM_EPS = 1e-08
ADAM_WD = 0.01
ADAM_STEP = 10
PER_EXAMPLE_BATCH_AXIS = {'x': 0, 'c': 0, 'ctx': 0, 'loss_target': 0}
SHARED_INPUTS = []
_WEIGHT_DTYPES = {'c_ctx': _jnp.float32, 'w_ada': _jnp.float32, 'b_ada': _jnp.float32, 'norm1_g': _jnp.float32, 'w_in': _jnp.float32, 'q_norm_g': _jnp.float32, 'k_norm_g': _jnp.float32, 'hg_lower_bounds': _jnp.float32, 'hg_norm_g': _jnp.float32, 'sg_norm_g': _jnp.float32, 'sg_w': _jnp.float32, 'sg_b': _jnp.float32, 'w_out': _jnp.float32, 'norm2_g': _jnp.float32, 'w_up': _jnp.float32, 'conv_w': _jnp.float32, 'conv_b': _jnp.float32, 'w_down': _jnp.float32, 'final_norm_g': _jnp.float32}
MOMENT_SCALE = {'c_ctx': 6.793433e-03, 'w_ada': 2.438623e-02, 'b_ada': 4.153337e-02, 'norm1_g': 1.979228e-02, 'w_in': 1.372525e-02, 'q_norm_g': 5.933014e-03, 'k_norm_g': 6.010354e-03, 'hg_lower_bounds': 6.837689e-04, 'hg_norm_g': 3.402202e-02, 'sg_norm_g': 1.931204e-02, 'sg_w': 1.883430e-02, 'sg_b': 1.871441e-02, 'w_out': 1.685447e-02, 'norm2_g': 2.506786e-02, 'w_up': 1.113929e-02, 'conv_w': 1.122029e-02, 'conv_b': 1.027878e-02, 'w_down': 1.820380e-02, 'final_norm_g': 1.601768e+01}


def _to_microbatches(a, axis):
    t = _jnp.moveaxis(a, axis, 0)
    t = t.reshape((N_MICROBATCH, t.shape[0] // N_MICROBATCH) + t.shape[1:])
    return _jnp.moveaxis(t, 1, axis + 1)


def setup_inputs(seed: int = 0) -> dict:
    inp = _fwd_setup_inputs(seed)
    key = _jax.random.fold_in(_jax.random.key(seed), 7919)
    shape, _ = _output_shape()
    out = dict(inp)
    out["loss_target"] = _jax.random.normal(_jax.random.fold_in(key, 0), shape, _jnp.float32)
    for i, name in enumerate(TWIN_WEIGHTS):
        w = inp[name].astype(_jnp.float32)
        if MOMENT_SCALE is None:
            s = _jnp.sqrt(_jnp.mean(_jnp.square(w)) + 1e-30)
        else:
            s = MOMENT_SCALE[name]
        km, kv = _jax.random.split(_jax.random.fold_in(key, i + 1))
        out[name] = w
        out["m_" + name] = s * _jax.random.normal(km, w.shape, _jnp.float32)
        out["v_" + name] = (s * s) * _jax.random.uniform(kv, w.shape, _jnp.float32, 0.5, 1.5)
    if N_MICROBATCH > 1:
        for name, axis in PER_EXAMPLE_BATCH_AXIS.items():
            out[name] = _to_microbatches(out[name], axis)
    return {'x': out['x'], 'c': out['c'], 'ctx': out['ctx'], 'c_ctx': out['c_ctx'], 'w_ada': out['w_ada'], 'b_ada': out['b_ada'], 'norm1_g': out['norm1_g'], 'w_in': out['w_in'], 'q_norm_g': out['q_norm_g'], 'k_norm_g': out['k_norm_g'], 'hg_lower_bounds': out['hg_lower_bounds'], 'hg_norm_g': out['hg_norm_g'], 'sg_norm_g': out['sg_norm_g'], 'sg_w': out['sg_w'], 'sg_b': out['sg_b'], 'w_out': out['w_out'], 'norm2_g': out['norm2_g'], 'w_up': out['w_up'], 'conv_w': out['conv_w'], 'conv_b': out['conv_b'], 'w_down': out['w_down'], 'final_norm_g': out['final_norm_g'], 'loss_target': out['loss_target'], 'm_c_ctx': out['m_c_ctx'], 'm_w_ada': out['m_w_ada'], 'm_b_ada': out['m_b_ada'], 'm_norm1_g': out['m_norm1_g'], 'm_w_in': out['m_w_in'], 'm_q_norm_g': out['m_q_norm_g'], 'm_k_norm_g': out['m_k_norm_g'], 'm_hg_lower_bounds': out['m_hg_lower_bounds'], 'm_hg_norm_g': out['m_hg_norm_g'], 'm_sg_norm_g': out['m_sg_norm_g'], 'm_sg_w': out['m_sg_w'], 'm_sg_b': out['m_sg_b'], 'm_w_out': out['m_w_out'], 'm_norm2_g': out['m_norm2_g'], 'm_w_up': out['m_w_up'], 'm_conv_w': out['m_conv_w'], 'm_conv_b': out['m_conv_b'], 'm_w_down': out['m_w_down'], 'm_final_norm_g': out['m_final_norm_g'], 'v_c_ctx': out['v_c_ctx'], 'v_w_ada': out['v_w_ada'], 'v_b_ada': out['v_b_ada'], 'v_norm1_g': out['v_norm1_g'], 'v_w_in': out['v_w_in'], 'v_q_norm_g': out['v_q_norm_g'], 'v_k_norm_g': out['v_k_norm_g'], 'v_hg_lower_bounds': out['v_hg_lower_bounds'], 'v_hg_norm_g': out['v_hg_norm_g'], 'v_sg_norm_g': out['v_sg_norm_g'], 'v_sg_w': out['v_sg_w'], 'v_sg_b': out['v_sg_b'], 'v_w_out': out['v_w_out'], 'v_norm2_g': out['v_norm2_g'], 'v_w_up': out['v_w_up'], 'v_conv_w': out['v_conv_w'], 'v_conv_b': out['v_conv_b'], 'v_w_down': out['v_w_down'], 'v_final_norm_g': out['v_final_norm_g']}


def _loss(weights, diff, rest, loss_target):
    with _jax.named_scope("forward"):
        args = {**rest, TWIN_DIFF_INPUT: diff, **{k: w.astype(_WEIGHT_DTYPES[k]) for k, w in weights.items()}}
        y = _forward(args)
    with _jax.named_scope("loss_head"):
        err = _jnp.square(y.astype(_jnp.float32) - loss_target)
        return 0.5 * _jnp.sum(_jnp.mean(err, axis=-1)) if err.ndim else 0.5 * err


def _adamw(w, g, m, v):
    m = ADAM_B1 * m + (1.0 - ADAM_B1) * g
    v = ADAM_B2 * v + (1.0 - ADAM_B2) * _jnp.square(g)
    m_hat = m / (1.0 - ADAM_B1 ** ADAM_STEP)
    v_hat = v / (1.0 - ADAM_B2 ** ADAM_STEP)
    delta = -ADAM_LR * (m_hat / (_jnp.sqrt(v_hat) + ADAM_EPS) + ADAM_WD * w)
    return delta, m, v


def reference(x, c, ctx, c_ctx, w_ada, b_ada, norm1_g, w_in, q_norm_g, k_norm_g, hg_lower_bounds, hg_norm_g, sg_norm_g, sg_w, sg_b, w_out, norm2_g, w_up, conv_w, conv_b, w_down, final_norm_g, loss_target, m_c_ctx, m_w_ada, m_b_ada, m_norm1_g, m_w_in, m_q_norm_g, m_k_norm_g, m_hg_lower_bounds, m_hg_norm_g, m_sg_norm_g, m_sg_w, m_sg_b, m_w_out, m_norm2_g, m_w_up, m_conv_w, m_conv_b, m_w_down, m_final_norm_g, v_c_ctx, v_w_ada, v_b_ada, v_norm1_g, v_w_in, v_q_norm_g, v_k_norm_g, v_hg_lower_bounds, v_hg_norm_g, v_sg_norm_g, v_sg_w, v_sg_b, v_w_out, v_norm2_g, v_w_up, v_conv_w, v_conv_b, v_w_down, v_final_norm_g):
    given = dict(x=x, c=c, ctx=ctx, c_ctx=c_ctx, w_ada=w_ada, b_ada=b_ada, norm1_g=norm1_g, w_in=w_in, q_norm_g=q_norm_g, k_norm_g=k_norm_g, hg_lower_bounds=hg_lower_bounds, hg_norm_g=hg_norm_g, sg_norm_g=sg_norm_g, sg_w=sg_w, sg_b=sg_b, w_out=w_out, norm2_g=norm2_g, w_up=w_up, conv_w=conv_w, conv_b=conv_b, w_down=w_down, final_norm_g=final_norm_g, loss_target=loss_target, m_c_ctx=m_c_ctx, m_w_ada=m_w_ada, m_b_ada=m_b_ada, m_norm1_g=m_norm1_g, m_w_in=m_w_in, m_q_norm_g=m_q_norm_g, m_k_norm_g=m_k_norm_g, m_hg_lower_bounds=m_hg_lower_bounds, m_hg_norm_g=m_hg_norm_g, m_sg_norm_g=m_sg_norm_g, m_sg_w=m_sg_w, m_sg_b=m_sg_b, m_w_out=m_w_out, m_norm2_g=m_norm2_g, m_w_up=m_w_up, m_conv_w=m_conv_w, m_conv_b=m_conv_b, m_w_down=m_w_down, m_final_norm_g=m_final_norm_g, v_c_ctx=v_c_ctx, v_w_ada=v_w_ada, v_b_ada=v_b_ada, v_norm1_g=v_norm1_g, v_w_in=v_w_in, v_q_norm_g=v_q_norm_g, v_k_norm_g=v_k_norm_g, v_hg_lower_bounds=v_hg_lower_bounds, v_hg_norm_g=v_hg_norm_g, v_sg_norm_g=v_sg_norm_g, v_sg_w=v_sg_w, v_sg_b=v_sg_b, v_w_out=v_w_out, v_norm2_g=v_norm2_g, v_w_up=v_w_up, v_conv_w=v_conv_w, v_conv_b=v_conv_b, v_w_down=v_w_down, v_final_norm_g=v_final_norm_g)
    weights = {n: given[n] for n in TWIN_WEIGHTS}
    shared = {n: given[n] for n in SHARED_INPUTS}
    per_example = {n: given[n] for n in ['x', 'c', 'ctx']}
    grad_fn = _jax.value_and_grad(_loss, argnums=(0, 1))

    def one_microbatch(ex, loss_target):
        ex = dict(ex)
        diff = ex.pop(TWIN_DIFF_INPUT)
        return grad_fn(weights, diff, {**shared, **ex}, loss_target)

    if N_MICROBATCH == 1:
        loss, (grad_w, grad_x) = one_microbatch(per_example, given["loss_target"])
    else:
        def body(carry, xs):
            loss_sum, grad_sum = carry
            l_k, (gw_k, gx_k) = one_microbatch(xs[0], xs[1])
            with _jax.named_scope("update"):
                return (loss_sum + l_k, _jax.tree.map(_jnp.add, grad_sum, gw_k)), gx_k

        init = (_jnp.zeros((), _jnp.float32), _jax.tree.map(_jnp.zeros_like, weights))
        (loss, grad_w), grad_x = _jax.lax.scan(body, init, (per_example, given["loss_target"]))
    with _jax.named_scope("update"):
        delta_w, new_m, new_v = {}, {}, {}
        for n in TWIN_WEIGHTS:
            delta_w[n], new_m[n], new_v[n] = _adamw(weights[n], grad_w[n], given["m_" + n], given["v_" + n])
    return (loss, grad_x, *[grad_w[n] for n in TWIN_WEIGHTS], *[delta_w[n] for n in TWIN_WEIGHTS],
            *[new_m[n] for n in TWIN_WEIGHTS], *[new_v[n] for n in TWIN_WEIGHTS])
```

```python
import functools
import math
from typing import NamedTuple

import numpy as np
import jax
import jax.numpy as jnp
from jax import lax
from jax.experimental import pallas as pl
from jax.experimental.pallas import tpu as pltpu

F32, BF16 = jnp.float32, jnp.bfloat16
S = jax.ShapeDtypeStruct
MESH = pl.DeviceIdType.MESH

LANE = 128
EPS = 1e-6
F_MIN = 1e-30
ROPE_THETA = 10000.0
N_MOD = 6
ADAM_LR, ADAM_B1, ADAM_B2, ADAM_EPS, ADAM_WD, ADAM_STEP = 0.001, 0.9, 0.999, 1e-08, 0.01, 10
VMEM_LIMIT = 56 * 1024 * 1024


class Dims(NamedTuple):
    d_model: int = 2048
    seq: int = 4096
    ctx_len: int = 256
    grid_w: int = 64
    depth: int = 4
    attn_heads: int = 8
    kv_heads: int = 2
    hg_heads: int = 4
    hg_chunk: int = 16
    sg_groups: int = 4
    d_ff: int = 5632

    @property
    def n_tok(self):
        return self.seq + self.ctx_len

    @property
    def q_w(self):
        return self.attn_heads * LANE

    @property
    def kv_w(self):
        return self.kv_heads * LANE

    @property
    def hg_w(self):
        return self.hg_heads * LANE

    @property
    def sg_w(self):
        return self.sg_groups * LANE

    @property
    def d_mix(self):
        return self.q_w + self.hg_w + self.sg_w

    @property
    def in_sizes(self):
        return (self.q_w, self.kv_w, self.kv_w) + (self.hg_w,) * 5 + (self.sg_w,) * 2

    @property
    def in_cols(self):
        return sum(self.in_sizes)

    @property
    def in_offs(self):
        return tuple(int(v) for v in np.cumsum((0,) + self.in_sizes)[:-1])

    @property
    def row_tile(self):
        return min(256, self.ctx_len)


def _cparams(sem, vmem=VMEM_LIMIT):
    return pltpu.CompilerParams(dimension_semantics=sem, vmem_limit_bytes=vmem)


_NN, _NT, _TN = ((1,), (0,)), ((1,), (1,)), ((0,), (0,))


def _dg(a, b, dims):
    return lax.dot_general(a.astype(BF16), b.astype(BF16), (dims, ((), ())), preferred_element_type=F32)


@jax.custom_vjp
def _bdot_nn(a, b):
    return _dg(a, b, _NN)


@jax.custom_vjp
def _bdot_nt(a, b):
    return _dg(a, b, _NT)


@jax.custom_vjp
def _bdot_tn(a, b):
    return _dg(a, b, _TN)


_bdot_nn.defvjp(lambda a, b: (_dg(a, b, _NN), (a, b)),
                lambda r, g: (_bdot_nt(g, r[1]).astype(r[0].dtype), _bdot_tn(r[0], g).astype(r[1].dtype)))
_bdot_nt.defvjp(lambda a, b: (_dg(a, b, _NT), (a, b)),
                lambda r, g: (_bdot_nn(g, r[1]).astype(r[0].dtype), _bdot_tn(g, r[0]).astype(r[1].dtype)))
_bdot_tn.defvjp(lambda a, b: (_dg(a, b, _TN), (a, b)),
                lambda r, g: (_bdot_nt(r[1], g).astype(r[0].dtype), _bdot_nn(r[0], g).astype(r[1].dtype)))


def _f32dot(a, b):
    return lax.dot_general(a, b, (_NN, ((), ())), precision=lax.Precision.HIGHEST, preferred_element_type=F32)


@jax.custom_vjp
def _tri_dot(tri, tri_t, x):
    return _f32dot(tri, x)


_tri_dot.defvjp(lambda tri, tri_t, x: (_f32dot(tri, x), (tri, tri_t)),
                lambda r, g: (jnp.zeros_like(r[0]), jnp.zeros_like(r[1]), _f32dot(r[1], g)))


@jax.custom_vjp
def _pair_swap(x):
    lane = lax.broadcasted_iota(jnp.int32, x.shape, x.ndim - 1)
    return jnp.where(lane % 2 == 0, pltpu.roll(x, LANE - 1, x.ndim - 1), pltpu.roll(x, 1, x.ndim - 1))


_pair_swap.defvjp(lambda x: (_pair_swap(x), None), lambda _, g: (_pair_swap(g),))


def R(a, w=None, cb=0, rmap=None):
    return ("r", a, a.shape[1] if w is None else w, cb, rmap)


def C(a):
    return ("c", a)


def _rows(name, fn, n_rows, tm, ins, outs, accs=()):
    n_in, n_out = len(ins), len(outs)
    in_specs, args = [], []
    for e in ins:
        if e[0] == "r":
            _, a, w, cb, rmap = e
            assert w % LANE == 0 or w == a.shape[1]
            if rmap is None:
                in_specs.append(pl.BlockSpec((tm, w), functools.partial(lambda i, cb: (i, cb), cb=cb)))
            else:
                in_specs.append(pl.BlockSpec((tm, w), functools.partial(lambda i, cb, rm: (rm(i), cb), cb=cb, rm=rmap)))
        else:
            a = e[1]
            in_specs.append(pl.BlockSpec(a.shape, functools.partial(lambda i, nd: (0,) * nd, nd=a.ndim)))
        args.append(a)
    out_shape = [S((n_rows, w), dt) for w, dt in outs] + [S(tuple(sh), F32) for sh in accs]
    out_specs = [pl.BlockSpec((tm, w), lambda i: (i, 0)) for w, _ in outs]
    out_specs += [pl.BlockSpec(tuple(sh), functools.partial(lambda i, nd: (0,) * nd, nd=len(sh))) for sh in accs]

    def body(*refs):
        i = pl.program_id(0)
        vals = fn(i, *[r[...] for r in refs[:n_in]])
        assert len(vals) == n_out + len(accs), (name, len(vals))
        for r, v in zip(refs[n_in:n_in + n_out], vals[:n_out]):
            r[...] = v.astype(r.dtype)
        for r, v in zip(refs[n_in + n_out:], vals[n_out:]):
            def init(r=r, v=v):
                r[...] = v.astype(F32)

            def add(r=r, v=v):
                r[...] += v.astype(F32)

            pl.when(i == 0)(init)
            pl.when(i != 0)(add)

    res = pl.pallas_call(body, name=name, grid=(n_rows // tm,), in_specs=in_specs, out_specs=out_specs,
                         out_shape=out_shape, compiler_params=_cparams(("arbitrary",)))(*args)
    return res


def _div_tile(n, cap, mult):
    if n <= cap:
        return n
    best = None
    for t in range(mult, cap + 1, mult):
        if n % t == 0:
            best = t
    assert best is not None, (n, cap, mult)
    return best


def _matmul(name, a, b, form, out_dtype, b_shards=1, out_shards=1, caps=(1088, 1408, 1408)):
    if form == "tn":
        K, M = a.shape
    else:
        M, K = a.shape
    if form == "nn":
        N = b.shape[-1] * b_shards
    elif form == "nt":
        N = b.shape[-2]
    else:
        N = b.shape[1]
    n_per = N // (b_shards if form == "nn" else out_shards)
    k_per = K // (b_shards if form == "nt" else 1)
    tm = _div_tile(M, caps[0], 16 if form != "tn" else LANE)
    tn = _div_tile(n_per, caps[1], LANE)
    tk = _div_tile(k_per, caps[2], LANE if form != "tn" else 16)
    nk = K // tk
    grid = (M // tm, N // tn, nk)
    nps, kps = n_per // tn, k_per // tk

    if form == "tn":
        a_spec = pl.BlockSpec((tk, tm), lambda i, j, k: (k, i))
    else:
        a_spec = pl.BlockSpec((tm, tk), lambda i, j, k: (i, k))
    if form == "nn":
        if b_shards > 1:
            b_spec = pl.BlockSpec((None, tk, tn), lambda i, j, k: (j // nps, k, j % nps))
        else:
            b_spec = pl.BlockSpec((tk, tn), lambda i, j, k: (k, j))
    elif form == "nt":
        if b_shards > 1:
            b_spec = pl.BlockSpec((None, tn, tk), lambda i, j, k: (k // kps, j, k % kps))
        else:
            b_spec = pl.BlockSpec((tn, tk), lambda i, j, k: (j, k))
    else:
        b_spec = pl.BlockSpec((tk, tn), lambda i, j, k: (k, j))
    if out_shards > 1:
        o_spec = pl.BlockSpec((None, tm, tn), lambda i, j, k: (j // nps, i, j % nps))
        o_shape = S((out_shards, M, n_per), out_dtype)
    else:
        o_spec = pl.BlockSpec((tm, tn), lambda i, j, k: (i, j))
        o_shape = S((M, N), out_dtype)
    dims = {"nn": _NN, "nt": _NT, "tn": _TN}[form]

    def body(a_ref, b_ref, o_ref, acc_ref):
        k = pl.program_id(2)
        p = _dg(a_ref[...], b_ref[...], dims)
        if nk == 1:
            o_ref[...] = p.astype(o_ref.dtype)
        else:
            @pl.when(k == 0)
            def _():
                acc_ref[...] = p

            @pl.when(jnp.logical_and(k > 0, k < nk - 1))
            def _():
                acc_ref[...] += p

            @pl.when(k == nk - 1)
            def _():
                o_ref[...] = (acc_ref[...] + p).astype(o_ref.dtype)

    return pl.pallas_call(body, name=name, grid=grid, in_specs=[a_spec, b_spec], out_specs=o_spec, out_shape=o_shape,
                          scratch_shapes=[pltpu.VMEM((tm, tn), F32)],
                          compiler_params=_cparams(("parallel", "parallel", "arbitrary")))(a, b)


def _rms(x, g):
    return x * lax.rsqrt(jnp.mean(x * x, axis=-1, keepdims=True) + EPS) * g


def _sel2(mm, is_ctx):
    return jnp.where(is_ctx, mm[0:1], mm[1:2])


def _put2(v, is_ctx):
    row = lax.broadcasted_iota(jnp.int32, (2, v.shape[-1]), 0)
    return jnp.where(row == jnp.where(is_ctx, 0, 1), v, 0.0)


def _normmod(x, g, sh, sc):
    return _rms(x, g) * (1.0 + sc) + sh


def _colsum(v):
    return jnp.sum(v, axis=0, keepdims=True)


def _normmod_fwd(dm, name, x, g, sh2, sc2):
    nc = dm.ctx_len // dm.row_tile

    def fn(i, x, g, sh2, sc2):
        is_ctx = i < nc
        return (_normmod(x, g, _sel2(sh2, is_ctx), _sel2(sc2, is_ctx)),)

    return _rows(name, fn, dm.n_tok, dm.row_tile, [R(x), C(g), C(sh2), C(sc2)], [(dm.d_model, BF16)])[0]


def _resnorm_fwd(dm, name, x, y, gate2, g, sh2, sc2):
    nc = dm.ctx_len // dm.row_tile

    def fn(i, x, y, gate2, g, sh2, sc2):
        is_ctx = i < nc
        x1 = x + _sel2(gate2, is_ctx) * y
        return x1, _normmod(x1, g, _sel2(sh2, is_ctx), _sel2(sc2, is_ctx))

    return _rows(name, fn, dm.n_tok, dm.row_tile, [R(x), R(y), C(gate2), C(g), C(sh2), C(sc2)],
                 [(dm.d_model, F32), (dm.d_model, BF16)])


def _res_fwd(dm, name, x, y, gate2):
    nc = dm.ctx_len // dm.row_tile

    def fn(i, x, y, gate2):
        return (x + _sel2(gate2, i < nc) * y,)

    return _rows(name, fn, dm.n_tok, dm.row_tile, [R(x), R(y), C(gate2)], [(dm.d_model, F32)])[0]


def _gate_bwd(dm, name, dx, y, gate2):
    nc = dm.ctx_len // dm.row_tile

    def fn(i, dx, y, gate2):
        is_ctx = i < nc
        return dx * _sel2(gate2, is_ctx), _put2(_colsum(dx * y), is_ctx)

    return _rows(name, fn, dm.n_tok, dm.row_tile, [R(dx), R(y), C(gate2)], [(dm.d_model, BF16)], [(2, dm.d_model)])


def _normmod_bwd(dm, name, x, dh, dres, g, sh2, sc2):
    nc = dm.ctx_len // dm.row_tile

    def fn(i, x, dh, dres, g, sh2, sc2):
        is_ctx = i < nc
        sh, sc = _sel2(sh2, is_ctx), _sel2(sc2, is_ctx)
        _, vjp = jax.vjp(_normmod, x, g, sh, sc)
        dx, dg, dsh, dsc = vjp(dh)
        return dres + dx, dg, _put2(dsh, is_ctx), _put2(dsc, is_ctx)

    return _rows(name, fn, dm.n_tok, dm.row_tile, [R(x), R(dh), R(dres), C(g), C(sh2), C(sc2)],
                 [(dm.d_model, F32)], [(1, dm.d_model), (2, dm.d_model), (2, dm.d_model)])


def _rope_tables(dm):
    t = jnp.arange(dm.seq)
    row = (t // dm.grid_w).astype(F32)
    col = (t % dm.grid_w).astype(F32)
    n_freq = LANE // 4
    inv = ROPE_THETA ** (-jnp.arange(n_freq, dtype=F32) / n_freq)
    ang = jnp.concatenate([row[:, None] * inv, col[:, None] * inv], axis=-1)
    cos, sin = jnp.cos(ang), jnp.sin(ang)
    ct = jnp.repeat(cos, 2, axis=-1)
    st = jnp.stack([-sin, sin], axis=-1).reshape(dm.seq, LANE)
    ct = jnp.concatenate([jnp.ones((dm.ctx_len, LANE), F32), ct], axis=0)
    st = jnp.concatenate([jnp.zeros((dm.ctx_len, LANE), F32), st], axis=0)
    return ct, st


def _qk_fn(n_q, n_k):
    def fn(aq, ak, ct, st, qg, kg):
        def head(x, g):
            y = _rms(x, g)
            return y * ct + _pair_swap(y) * st

        q = jnp.concatenate([head(aq[:, h * LANE:(h + 1) * LANE], qg) for h in range(n_q)], axis=1)
        k = jnp.concatenate([head(ak[:, h * LANE:(h + 1) * LANE], kg) for h in range(n_k)], axis=1)
        return q, k

    return fn


def _qk_fwd(dm, name, proj, ct, st, qg, kg):
    f = _qk_fn(dm.attn_heads, dm.kv_heads)
    o = dm.in_offs
    return _rows(name, lambda i, *a: f(*a), dm.n_tok, dm.row_tile,
                 [R(proj, dm.q_w, o[0] // dm.q_w), R(proj, dm.kv_w, o[1] // dm.kv_w), R(ct), R(st), C(qg), C(kg)],
                 [(dm.q_w, BF16), (dm.kv_w, BF16)])


def _qk_bwd(dm, name, proj, ct, st, qg, kg, dq, dk):
    f = _qk_fn(dm.attn_heads, dm.kv_heads)
    o = dm.in_offs

    def fn(i, aq, ak, ct, st, qg, kg, dq, dk):
        _, vjp = jax.vjp(lambda aq, ak, qg, kg: f(aq, ak, ct, st, qg, kg), aq, ak, qg, kg)
        return vjp((dq, dk))

    return _rows(name, fn, dm.n_tok, dm.row_tile,
                 [R(proj, dm.q_w, o[0] // dm.q_w), R(proj, dm.kv_w, o[1] // dm.kv_w), R(ct), R(st), C(qg), C(kg),
                  R(dq), R(dk)],
                 [(dm.q_w, BF16), (dm.kv_w, BF16)], [(1, LANE), (1, LANE)])


def _attn_probs(q, k, i, nc, ctx_len, n_tok):
    s = _dg(q, k, _NT) * (LANE ** -0.5)
    col = lax.broadcasted_iota(jnp.int32, s.shape, 1)
    s = jnp.where(col < jnp.where(i < nc, ctx_len, n_tok), s, -1e30)
    p = jnp.exp(s - jnp.max(s, axis=-1, keepdims=True))
    return p / jnp.sum(p, axis=-1, keepdims=True)


def _attn_fwd(dm, name, qh, kh, proj):
    tq, n = dm.row_tile, dm.n_tok
    nc, grp = dm.ctx_len // tq, dm.attn_heads // dm.kv_heads
    v_cb = dm.in_offs[2] // LANE

    def body(q_ref, k_ref, v_ref, o_ref):
        p = _attn_probs(q_ref[...], k_ref[...], pl.program_id(1), nc, dm.ctx_len, n)
        o_ref[...] = _dg(p, v_ref[...], _NN).astype(o_ref.dtype)

    return pl.pallas_call(
        body, name=name, grid=(dm.attn_heads, n // tq),
        in_specs=[pl.BlockSpec((tq, LANE), lambda h, i: (i, h)),
                  pl.BlockSpec((n, LANE), lambda h, i: (0, h // grp)),
                  pl.BlockSpec((n, LANE), lambda h, i: (0, v_cb + h // grp))],
        out_specs=pl.BlockSpec((tq, LANE), lambda h, i: (i, h)),
        out_shape=S((n, dm.q_w), BF16), compiler_params=_cparams(("parallel", "parallel")))(qh, kh, proj)


def _attn_bwd(dm, name, qh, kh, proj, attn, dmix):
    tq, n = dm.row_tile, dm.n_tok
    nc, grp = dm.ctx_len // tq, dm.attn_heads // dm.kv_heads
    v_cb = dm.in_offs[2] // LANE

    def body(q_ref, k_ref, v_ref, o_ref, do_ref, dq_ref, dk_ref, dv_ref):
        first = jnp.logical_and(pl.program_id(1) == 0, pl.program_id(2) == 0)
        q, k, v, do = q_ref[...], k_ref[...], v_ref[...], do_ref[...]
        p = _attn_probs(q, k, pl.program_id(2), nc, dm.ctx_len, n)
        delta = jnp.sum(do * o_ref[...].astype(F32), axis=-1, keepdims=True)
        ds = p * (_dg(do, v, _NT) - delta) * (LANE ** -0.5)
        dq_ref[...] = _dg(ds, k, _NN)
        dk = _dg(ds, q, _TN)
        dv = _dg(p, do, _TN)

        @pl.when(first)
        def _():
            dk_ref[...] = dk
            dv_ref[...] = dv

        @pl.when(jnp.logical_not(first))
        def _():
            dk_ref[...] += dk
            dv_ref[...] += dv

    return pl.pallas_call(
        body, name=name, grid=(dm.kv_heads, grp, n // tq),
        in_specs=[pl.BlockSpec((tq, LANE), lambda g, hh, i: (i, g * grp + hh)),
                  pl.BlockSpec((n, LANE), lambda g, hh, i: (0, g)),
                  pl.BlockSpec((n, LANE), lambda g, hh, i: (0, v_cb + g)),
                  pl.BlockSpec((tq, LANE), lambda g, hh, i: (i, g * grp + hh)),
                  pl.BlockSpec((tq, LANE), lambda g, hh, i: (i, g * grp + hh))],
        out_specs=[pl.BlockSpec((tq, LANE), lambda g, hh, i: (i, g * grp + hh)),
                   pl.BlockSpec((n, LANE), lambda g, hh, i: (0, g)),
                   pl.BlockSpec((n, LANE), lambda g, hh, i: (0, g))],
        out_shape=[S((n, dm.q_w), F32), S((n, dm.kv_w), F32), S((n, dm.kv_w), F32)],
        compiler_params=_cparams(("parallel", "arbitrary", "arbitrary")))(qh, kh, proj, attn, dmix)


def _hg_chunk(d, st, qraw, fraw, v, lb):
    c = qraw.shape[0]
    sig = jax.nn.sigmoid(fraw)
    f = lb + (1.0 - lb) * sig
    logf = jnp.log(jnp.maximum(f, F_MIN))
    k = (1.0 - lb) * jax.nn.sigmoid(-fraw)
    q = qraw * jax.nn.sigmoid(qraw)
    r_i = lax.broadcasted_iota(jnp.int32, (c, c), 0)
    c_i = lax.broadcasted_iota(jnp.int32, (c, c), 1)
    sgn = 1 - 2 * d
    tri = ((r_i - c_i) * sgn >= 0).astype(F32)
    tri_t = ((c_i - r_i) * sgn >= 0).astype(F32)
    b = _tri_dot(tri, tri_t, logf)
    b_last = jnp.sum(logf, axis=0, keepdims=True)
    trow = lax.broadcasted_iota(jnp.int32, (c, 1), 0)
    o = _bdot_nt(q * jnp.exp(b), st)
    for s in range(c):
        m = (trow - s) * sgn >= 0
        e = jnp.exp(jnp.where(m, b - b[s:s + 1], 0.0))
        w = jnp.where(m, q * k[s:s + 1] * e, 0.0)
        o = o + jnp.sum(w, axis=-1, keepdims=True) * v[s:s + 1]
    st_new = st * jnp.exp(b_last) + _bdot_tn(v, k * jnp.exp(b_last - b))
    return o, st_new


def _hg_blk(dm, tb):
    nbc, nbl = dm.ctx_len // tb, dm.seq // tb

    def blk(d, j):
        rev = jnp.where(j < nbc, nbc - 1 - j, 2 * nbc + nbl - 1 - j)
        return jnp.where(d == 0, j, rev)

    return blk, nbc + nbl


def _hgrn_fwd(dm, name, proj, lb):
    tb, n, hh, ck = dm.row_tile, dm.n_tok, dm.hg_heads, dm.hg_chunk
    blk, nblk = _hg_blk(dm, tb)
    ncb = tb // ck
    o = dm.in_offs
    q_cb, f_cb, v_cb = o[3] // LANE, o[4] // LANE, o[6] // LANE

    def body(q_ref, f_ref, v_ref, lb_ref, o_ref, hist_ref, st_ref):
        d = pl.program_id(0)

        @pl.when(pl.program_id(2) == 0)
        def _():
            st_ref[...] = jnp.zeros_like(st_ref)

        lbv = lb_ref[...]

        def chunk(ci, carry):
            c = jnp.where(d == 0, ci, ncb - 1 - ci)
            rows = pl.ds(pl.multiple_of(c * ck, ck), ck)
            st = st_ref[...]
            hist_ref[c] = st
            oc, stn = _hg_chunk(d, st, q_ref[rows, :], f_ref[rows, :], v_ref[rows, :], lbv)
            o_ref[rows, :] = oc
            st_ref[...] = stn
            return carry

        lax.fori_loop(0, ncb, chunk, 0)

    return pl.pallas_call(
        body, name=name, grid=(2, hh, nblk),
        in_specs=[pl.BlockSpec((tb, LANE), lambda d, h, j: (blk(d, j), q_cb + h)),
                  pl.BlockSpec((tb, LANE), lambda d, h, j: (blk(d, j), f_cb + d * hh + h)),
                  pl.BlockSpec((tb, LANE), lambda d, h, j: (blk(d, j), v_cb + h)),
                  pl.BlockSpec((None, 1, LANE), lambda d, h, j: (d * hh + h, 0, 0))],
        out_specs=[pl.BlockSpec((None, tb, LANE), lambda d, h, j: (d, blk(d, j), h)),
                   pl.BlockSpec((None, None, ncb, LANE, LANE), lambda d, h, j: (d, h, blk(d, j), 0, 0))],
        out_shape=[S((2, n, dm.hg_w), F32), S((2, hh, n // ck, LANE, LANE), F32)],
        scratch_shapes=[pltpu.VMEM((LANE, LANE), F32)],
        compiler_params=_cparams(("parallel", "parallel", "arbitrary")))(proj, proj, proj, lb)


def _hgrn_bwd(dm, name, proj, lb, hist, do):
    tb, n, hh, ck = dm.row_tile, dm.n_tok, dm.hg_heads, dm.hg_chunk
    blk, nblk = _hg_blk(dm, tb)
    ncb = tb // ck
    o = dm.in_offs
    q_cb, f_cb, v_cb = o[3] // LANE, o[4] // LANE, o[6] // LANE

    def rblk(d, j):
        return blk(d, nblk - 1 - j)

    def body(q_ref, f_ref, v_ref, lb_ref, hist_ref, do_ref, dq_ref, df_ref, dv_ref, dlb_ref, dst_ref):
        d = pl.program_id(0)

        @pl.when(pl.program_id(2) == 0)
        def _():
            dst_ref[...] = jnp.zeros_like(dst_ref)
            dlb_ref[...] = jnp.zeros_like(dlb_ref)

        lbv = lb_ref[...]

        def chunk(ci, carry):
            cp = ncb - 1 - ci
            c = jnp.where(d == 0, cp, ncb - 1 - cp)
            rows = pl.ds(pl.multiple_of(c * ck, ck), ck)
            _, vjp = jax.vjp(functools.partial(_hg_chunk, d), hist_ref[c], q_ref[rows, :], f_ref[rows, :],
                             v_ref[rows, :], lbv)
            dst, dq, df, dv, dlb = vjp((do_ref[rows, :], dst_ref[...]))
            dq_ref[rows, :] = dq
            df_ref[rows, :] = df
            dv_ref[rows, :] = dv
            dlb_ref[...] += dlb
            dst_ref[...] = dst
            return carry

        lax.fori_loop(0, ncb, chunk, 0)

    row3 = pl.BlockSpec((None, tb, LANE), lambda d, h, j: (d, rblk(d, j), h))
    return pl.pallas_call(
        body, name=name, grid=(2, hh, nblk),
        in_specs=[pl.BlockSpec((tb, LANE), lambda d, h, j: (rblk(d, j), q_cb + h)),
                  pl.BlockSpec((tb, LANE), lambda d, h, j: (rblk(d, j), f_cb + d * hh + h)),
                  pl.BlockSpec((tb, LANE), lambda d, h, j: (rblk(d, j), v_cb + h)),
                  pl.BlockSpec((None, 1, LANE), lambda d, h, j: (d * hh + h, 0, 0)),
                  pl.BlockSpec((None, None, ncb, LANE, LANE), lambda d, h, j: (d, h, rblk(d, j), 0, 0)),
                  pl.BlockSpec((tb, LANE), lambda d, h, j: (rblk(d, j), h))],
        out_specs=[row3, row3, row3, pl.BlockSpec((None, 1, LANE), lambda d, h, j: (d * hh + h, 0, 0))],
        out_shape=[S((2, n, dm.hg_w), F32)] * 3 + [S((2 * hh, 1, LANE), F32)],
        scratch_shapes=[pltpu.VMEM((LANE, LANE), F32)],
        compiler_params=_cparams(("parallel", "parallel", "arbitrary")))(proj, proj, proj, lb, hist, do)


def _hgc_fn(n_h):
    def fn(o0, o1, gt, g):
        osum = o0 + o1
        y = jnp.concatenate([_rms(osum[:, h * LANE:(h + 1) * LANE], g) for h in range(n_h)], axis=1)
        return y * (gt * jax.nn.sigmoid(gt))

    return fn


def _hgc_fwd(dm, name, o_dir, proj, g):
    f = _hgc_fn(dm.hg_heads)
    cb = dm.in_offs[7] // dm.hg_w
    return _rows(name, lambda i, *a: (f(*a),), dm.n_tok, dm.row_tile,
                 [R(o_dir[0]), R(o_dir[1]), R(proj, dm.hg_w, cb), C(g)], [(dm.hg_w, BF16)])[0]


def _hgc_bwd(dm, name, o_dir, proj, g, dmix):
    f = _hgc_fn(dm.hg_heads)
    cb = dm.in_offs[7] // dm.hg_w

    def fn(i, o0, o1, gt, g, dy):
        _, vjp = jax.vjp(f, o0, o1, gt, g)
        do, _, dgt, dg = vjp(dy)
        return do, dgt, dg

    return _rows(name, fn, dm.n_tok, dm.row_tile,
                 [R(o_dir[0]), R(o_dir[1]), R(proj, dm.hg_w, cb), C(g), R(dmix, dm.hg_w, dm.q_w // dm.hg_w)],
                 [(dm.hg_w, F32), (dm.hg_w, BF16)], [(1, LANE)])


def _sg_fn(n_g):
    def fn(su, sv, ng, w, bcol):
        u = jax.nn.gelu(su)
        gv = jax.nn.gelu(sv)
        outs = []
        for g in range(n_g):
            sl = slice(g * LANE, (g + 1) * LANE)
            vn = _rms(gv[:, sl], ng[:, sl])
            outs.append(_bdot_nn(w[g], vn) + bcol[g])
        return u * jnp.concatenate(outs, axis=1)

    return fn


def _sg_fwd(dm, name, proj, ng, w, bcol):
    f = _sg_fn(dm.sg_groups)
    o = dm.in_offs
    return _rows(name, lambda i, *a: (f(*a),), dm.n_tok, LANE,
                 [R(proj, dm.sg_w, o[8] // dm.sg_w), R(proj, dm.sg_w, o[9] // dm.sg_w), C(ng), C(w), C(bcol)],
                 [(dm.sg_w, BF16)])[0]


def _sg_bwd(dm, name, proj, ng, w, bcol, dmix):
    f = _sg_fn(dm.sg_groups)
    o = dm.in_offs

    def fn(i, su, sv, ng, w, bcol, dy):
        _, vjp = jax.vjp(f, su, sv, ng, w, bcol)
        return vjp(dy)

    return _rows(name, fn, dm.n_tok, LANE,
                 [R(proj, dm.sg_w, o[8] // dm.sg_w), R(proj, dm.sg_w, o[9] // dm.sg_w), C(ng), C(w), C(bcol),
                  R(dmix, dm.sg_w, (dm.q_w + dm.hg_w) // dm.sg_w)],
                 [(dm.sg_w, BF16), (dm.sg_w, BF16)],
                 [(1, dm.sg_w), (dm.sg_groups, LANE, LANE), (dm.sg_groups, LANE, 1)])


def _conv3(x, prev8, next8, w0, w1, w2, zero_prev, zero_next):
    tm = x.shape[0]
    row = lax.broadcasted_iota(jnp.int32, (tm, 1), 0)
    up = jnp.where(zero_prev, 0.0, prev8[7:8])
    dn = jnp.where(zero_next, 0.0, next8[0:1])
    x_m1 = jnp.where(row == 0, up, pltpu.roll(x, 1, 0))
    x_p1 = jnp.where(row == tm - 1, dn, pltpu.roll(x, tm - 1, 0))
    return w0 * x_m1 + w1 * x + w2 * x_p1, x_m1, x_p1


def _conv_edges(dm, tm):
    nbc, nb = dm.ctx_len // tm, dm.n_tok // tm

    def edges(i):
        return (jnp.logical_or(i == 0, i == nbc), jnp.logical_or(i == nbc - 1, i == nb - 1))

    return edges


def _halo_specs(tm, tn, n_rows, cb_off, i_axis, j_axis):
    last8 = n_rows // 8 - 1
    r8 = tm // 8

    def pick(ids):
        return ids[i_axis], ids[j_axis] + cb_off

    return [pl.BlockSpec((tm, tn), lambda *ids: pick(ids)),
            pl.BlockSpec((8, tn), lambda *ids: (jnp.maximum(pick(ids)[0] * r8 - 1, 0), pick(ids)[1])),
            pl.BlockSpec((8, tn), lambda *ids: (jnp.minimum((pick(ids)[0] + 1) * r8, last8), pick(ids)[1]))]


def _conv_fwd(dm, name, up, cw, cb):
    n, ff, tm = dm.n_tok, dm.d_ff, dm.row_tile
    tn = _div_tile(ff, 512, LANE)
    nj = ff // tn
    edges = _conv_edges(dm, tm)

    def body(g_ref, gp_ref, gn_ref, v_ref, vp_ref, vn_ref, wg_ref, wv_ref, bg_ref, bv_ref, a_ref):
        zp, zn = edges(pl.program_id(0))
        wg, wv = wg_ref[...], wv_ref[...]
        yg = _conv3(g_ref[...], gp_ref[...], gn_ref[...], wg[0:1], wg[1:2], wg[2:3], zp, zn)[0] + bg_ref[...]
        yv = _conv3(v_ref[...], vp_ref[...], vn_ref[...], wv[0:1], wv[1:2], wv[2:3], zp, zn)[0] + bv_ref[...]
        a_ref[...] = (yg * jax.nn.sigmoid(yg) * yv).astype(a_ref.dtype)

    small = lambda off: pl.BlockSpec((3, tn), lambda i, j: (0, j + off))
    bias = lambda off: pl.BlockSpec((1, tn), lambda i, j: (0, j + off))
    return pl.pallas_call(
        body, name=name, grid=(n // tm, nj),
        in_specs=_halo_specs(tm, tn, n, 0, 0, 1) + _halo_specs(tm, tn, n, nj, 0, 1)
        + [small(0), small(nj), bias(0), bias(nj)],
        out_specs=pl.BlockSpec((tm, tn), lambda i, j: (i, j)), out_shape=S((n, ff), BF16),
        compiler_params=_cparams(("parallel", "parallel")))(up, up, up, up, up, up, cw, cw, cb, cb)


def _conv_bwd_dy(dm, name, up, cw, cb, da):
    n, ff, tm = dm.n_tok, dm.d_ff, dm.row_tile
    tn = _div_tile(ff, 512, LANE)
    nj = ff // tn
    edges = _conv_edges(dm, tm)

    def body(g_ref, gp_ref, gn_ref, v_ref, vp_ref, vn_ref, wg_ref, wv_ref, bg_ref, bv_ref, da_ref,
             dyg_ref, dyv_ref, dwg_ref, dwv_ref, dbg_ref, dbv_ref):
        i = pl.program_id(1)
        zp, zn = edges(i)
        wg, wv = wg_ref[...], wv_ref[...]
        g, v = g_ref[...], v_ref[...]
        cg, g_m1, g_p1 = _conv3(g, gp_ref[...], gn_ref[...], wg[0:1], wg[1:2], wg[2:3], zp, zn)
        cv, v_m1, v_p1 = _conv3(v, vp_ref[...], vn_ref[...], wv[0:1], wv[1:2], wv[2:3], zp, zn)
        yg, yv = cg + bg_ref[...], cv + bv_ref[...]
        sg = jax.nn.sigmoid(yg)
        da = da_ref[...]
        dyg = da * yv * (sg * (1.0 + yg * (1.0 - sg)))
        dyv = da * (yg * sg)
        dyg_ref[...] = dyg
        dyv_ref[...] = dyv
        row = lax.broadcasted_iota(jnp.int32, (3, tn), 0)

        def stack3(dy, a, b, c):
            return jnp.where(row == 0, _colsum(dy * a), jnp.where(row == 1, _colsum(dy * b), _colsum(dy * c)))

        upd = [(dwg_ref, stack3(dyg, g_m1, g, g_p1)), (dwv_ref, stack3(dyv, v_m1, v, v_p1)),
               (dbg_ref, _colsum(dyg)), (dbv_ref, _colsum(dyv))]

        @pl.when(i == 0)
        def _():
            for r, val in upd:
                r[...] = val

        @pl.when(i != 0)
        def _():
            for r, val in upd:
                r[...] += val

    hs = lambda off: _halo_specs(tm, tn, n, off, 1, 0)
    small = lambda off: pl.BlockSpec((3, tn), lambda j, i: (0, j + off))
    bias = lambda off: pl.BlockSpec((1, tn), lambda j, i: (0, j + off))
    blk = pl.BlockSpec((tm, tn), lambda j, i: (i, j))
    return pl.pallas_call(
        body, name=name, grid=(nj, n // tm),
        in_specs=hs(0) + hs(nj) + [small(0), small(nj), bias(0), bias(nj), blk],
        out_specs=[blk, blk, small(0), small(0), bias(0), bias(0)],
        out_shape=[S((n, ff), F32), S((n, ff), F32), S((3, ff), F32), S((3, ff), F32), S((1, ff), F32),
                   S((1, ff), F32)],
        compiler_params=_cparams(("parallel", "arbitrary")))(up, up, up, up, up, up, cw, cw, cb, cb, da)


def _conv_bwd_dx(dm, name, dyg, dyv, cw):
    n, ff, tm = dm.n_tok, dm.d_ff, dm.row_tile
    tn = _div_tile(ff, 512, LANE)
    nj = ff // tn
    edges = _conv_edges(dm, tm)

    def body(g_ref, gp_ref, gn_ref, v_ref, vp_ref, vn_ref, wg_ref, wv_ref, o_ref):
        zp, zn = edges(pl.program_id(0))
        half = pl.program_id(1) // nj
        x = jnp.where(half == 0, g_ref[...], v_ref[...])
        xp = jnp.where(half == 0, gp_ref[...], vp_ref[...])
        xn = jnp.where(half == 0, gn_ref[...], vn_ref[...])
        w = jnp.where(half == 0, wg_ref[...], wv_ref[...])
        o_ref[...] = _conv3(x, xp, xn, w[2:3], w[1:2], w[0:1], zp, zn)[0].astype(o_ref.dtype)

    def hs():
        last8, r8 = n // 8 - 1, tm // 8
        return [pl.BlockSpec((tm, tn), lambda i, j: (i, j % nj)),
                pl.BlockSpec((8, tn), lambda i, j: (jnp.maximum(i * r8 - 1, 0), j % nj)),
                pl.BlockSpec((8, tn), lambda i, j: (jnp.minimum((i + 1) * r8, last8), j % nj))]

    return pl.pallas_call(
        body, name=name, grid=(n // tm, 2 * nj),
        in_specs=hs() + hs() + [pl.BlockSpec((3, tn), lambda i, j: (0, j % nj)),
                                pl.BlockSpec((3, tn), lambda i, j: (0, nj + j % nj))],
        out_specs=pl.BlockSpec((tm, tn), lambda i, j: (i, j)), out_shape=S((n, 2 * ff), BF16),
        compiler_params=_cparams(("parallel", "parallel")))(dyg, dyg, dyg, dyv, dyv, dyv, cw, cw)


def _loss_head(dm, name, x, tgt, g):
    tm = dm.row_tile
    nc = dm.ctx_len // tm

    def fn(i, x, t, g):
        def f(x, g):
            err = _rms(x, g) - t
            return 0.5 * jnp.sum(jnp.mean(err * err, axis=-1, keepdims=True), axis=0, keepdims=True)

        loss, vjp = jax.vjp(f, x, g)
        dx, dg = vjp(jnp.ones((1, 1), F32))
        live = i >= nc
        return (jnp.where(live, dx, 0.0), jnp.where(live, jnp.broadcast_to(loss, (1, LANE)), 0.0),
                jnp.where(live, dg, 0.0))

    return _rows(name, fn, dm.n_tok, tm, [R(x), R(tgt, rmap=lambda i: jnp.maximum(i - nc, 0)), C(g)],
                 [(dm.d_model, F32)], [(1, LANE), (1, dm.d_model)])


def _dproj_assemble(dm, name, d_aq, d_ak, dv, dq_dir, df_dir, dv_dir, d_hgt, d_su, d_sv):
    def fn(i, d_aq, d_ak, dv, q0, q1, f0, f1, v0, v1, d_hgt, d_su, d_sv):
        parts = [d_aq, d_ak, dv, q0 + q1, f0, f1, v0 + v1, d_hgt, d_su, d_sv]
        return (jnp.concatenate([p.astype(F32) for p in parts], axis=1),)

    ins = [R(d_aq), R(d_ak), R(dv), R(dq_dir[0]), R(dq_dir[1]), R(df_dir[0]), R(df_dir[1]), R(dv_dir[0]),
           R(dv_dir[1]), R(d_hgt), R(d_su), R(d_sv)]
    return _rows(name, fn, dm.n_tok, dm.row_tile, ins, [(dm.in_cols, BF16)])[0]


def _layer_fwd(dm, l, x, h, mods, wl, tabs):
    ct, st = tabs
    proj = _matmul(f"l{l}_proj", h, wl["w_in"], "nn", F32, b_shards=4)
    qh, kh = _qk_fwd(dm, f"l{l}_qk", proj, ct, st, wl["q_g"], wl["k_g"])
    attn = _attn_fwd(dm, f"l{l}_attn", qh, kh, proj)
    o_dir, hist = _hgrn_fwd(dm, f"l{l}_hgrn", proj, wl["lb"])
    hg = _hgc_fwd(dm, f"l{l}_hgc", o_dir, proj, wl["hg_g"])
    sg = _sg_fwd(dm, f"l{l}_sg", proj, wl["sg_g"], wl["sg_w"], wl["sg_bcol"])
    mix = jnp.concatenate([attn, hg, sg], axis=1)
    m = _matmul(f"l{l}_out", mix, wl["w_out"], "nn", F32)
    x1, h2 = _resnorm_fwd(dm, f"l{l}_resnorm2", x, m, mods[2], wl["norm2_g"], mods[3], mods[4])
    up = _matmul(f"l{l}_up", h2, wl["w_up"], "nn", F32, b_shards=4)
    a = _conv_fwd(dm, f"l{l}_conv", up, wl["conv_w"], wl["conv_b"])
    f = _matmul(f"l{l}_down", a, wl["w_down"], "nn", F32)
    saved = dict(x=x, h=h, proj=proj, qh=qh, kh=kh, attn=attn, o_dir=o_dir, hist=hist, mix=mix, m=m, x1=x1, h2=h2,
                 up=up, a=a, f=f)
    return x1, f, saved


def _layer_bwd(dm, l, dx2, sv, mods, wl, tabs):
    ct, st = tabs
    g = {}
    df, g["mod5"] = _gate_bwd(dm, f"l{l}_b_gate5", dx2, sv["f"], mods[5])
    da = _matmul(f"l{l}_b_da", df, wl["w_down"], "nt", F32)
    g["w_down"] = _matmul(f"l{l}_b_wdown", sv["a"], df, "tn", BF16)
    dyg, dyv, dwg, dwv, dbg, dbv = _conv_bwd_dy(dm, f"l{l}_b_convdy", sv["up"], wl["conv_w"], wl["conv_b"], da)
    g["conv_w"] = jnp.concatenate([dwg, dwv], axis=1)
    g["conv_b"] = jnp.concatenate([dbg, dbv], axis=1)
    d_up = _conv_bwd_dx(dm, f"l{l}_b_convdx", dyg, dyv, wl["conv_w"])
    dh2 = _matmul(f"l{l}_b_dh2", d_up, wl["w_up"], "nt", F32, b_shards=4)
    g["w_up"] = _matmul(f"l{l}_b_wup", sv["h2"], d_up, "tn", BF16, out_shards=4)
    dx1, g["norm2_g"], g["mod3"], g["mod4"] = _normmod_bwd(dm, f"l{l}_b_norm2", sv["x1"], dh2, dx2, wl["norm2_g"],
                                                             mods[3], mods[4])
    dmv, g["mod2"] = _gate_bwd(dm, f"l{l}_b_gate2", dx1, sv["m"], mods[2])
    dmix = _matmul(f"l{l}_b_dmix", dmv, wl["w_out"], "nt", F32)
    g["w_out"] = _matmul(f"l{l}_b_wout", sv["mix"], dmv, "tn", BF16)
    proj = sv["proj"]
    dqh, dkh, dv = _attn_bwd(dm, f"l{l}_b_attn", sv["qh"], sv["kh"], proj, sv["attn"], dmix)
    d_aq, d_ak, g["q_g"], g["k_g"] = _qk_bwd(dm, f"l{l}_b_qk", proj, ct, st, wl["q_g"], wl["k_g"], dqh, dkh)
    do, d_hgt, g["hg_g"] = _hgc_bwd(dm, f"l{l}_b_hgc", sv["o_dir"], proj, wl["hg_g"], dmix)
    dq_dir, df_dir, dv_dir, g["lb"] = _hgrn_bwd(dm, f"l{l}_b_hgrn", proj, wl["lb"], sv["hist"], do)
    d_su, d_sv, g["sg_g"], g["sg_w"], g["sg_bcol"] = _sg_bwd(dm, f"l{l}_b_sg", proj, wl["sg_g"], wl["sg_w"],
                                                            wl["sg_bcol"], dmix)
    dproj = _dproj_assemble(dm, f"l{l}_b_dproj", d_aq, d_ak, dv, dq_dir, df_dir, dv_dir, d_hgt, d_su, d_sv)
    dh = _matmul(f"l{l}_b_dh", dproj, wl["w_in"], "nt", F32, b_shards=4)
    g["w_in"] = _matmul(f"l{l}_b_win", sv["h"], dproj, "tn", BF16, out_shards=4)
    dx, g["norm1_g"], g["mod0"], g["mod1"] = _normmod_bwd(dm, f"l{l}_b_norm1", sv["x"], dh, dx1, wl["norm1_g"],
                                                           mods[0], mods[1])
    return dx, g


def _sample_step(dm, x_all, tgt, mods, wls, final_g, tabs):
    saved = []
    x = x_all
    h = _normmod_fwd(dm, "l0_norm1", x, wls[0]["norm1_g"], mods[0][0], mods[0][1])
    for l in range(dm.depth):
        x1, f, sv = _layer_fwd(dm, l, x, h, mods[l], wls[l], tabs)
        saved.append(sv)
        if l + 1 < dm.depth:
            x, h = _resnorm_fwd(dm, f"l{l}_resnorm1", x1, f, mods[l][5], wls[l + 1]["norm1_g"], mods[l + 1][0],
                                mods[l + 1][1])
        else:
            x = _res_fwd(dm, f"l{l}_res", x1, f, mods[l][5])
    dx, loss, dfg = _loss_head(dm, "loss_head", x, tgt, final_g)
    grads = [None] * dm.depth
    for l in reversed(range(dm.depth)):
        dx, grads[l] = _layer_bwd(dm, l, dx, saved[l], mods[l], wls[l], tabs)
    return loss, dx, grads, dfg


_ANY = pl.BlockSpec(memory_space=pl.ANY)


def _place():
    x, y, c = lax.axis_index("x"), lax.axis_index("y"), lax.axis_index("c")
    return x, y, c


def _all_gather8(name, blk):
    r, cdim = blk.shape

    def body(x_ref, out_ref, send_sems, recv_sems, local_sem):
        x, y, c = _place()
        me, sibling = (x, y, c), (x, y, 1 - c)
        chips = [(1 - x, y), (x, 1 - y), (1 - x, 1 - y)]

        def slot(px, py, pc):
            return out_ref.at[4 * px + 2 * py + pc]

        def copy(k, block, to, src=None):
            return pltpu.make_async_remote_copy(
                src_ref=slot(*block) if src is None else src, dst_ref=slot(*block),
                send_sem=send_sems.at[k], recv_sem=recv_sems.at[k], device_id=to, device_id_type=MESH)

        mine = pltpu.make_async_copy(x_ref, slot(*me), local_sem)
        mine.start()
        first = [copy(0, me, sibling, src=x_ref)]
        first += [copy(1 + j, me, (*chip, c), src=x_ref) for j, chip in enumerate(chips)]
        for cp in first:
            cp.start()
        passed = [copy(4 + j, (*chip, c), sibling) for j, chip in enumerate(chips)]
        for j, chip in enumerate(chips):
            copy(1 + j, (*chip, c), me).wait_recv()
            passed[j].start()
        copy(0, sibling, me).wait_recv()
        for j, chip in enumerate(chips):
            copy(4 + j, (*chip, 1 - c), me).wait_recv()
        for cp in first + passed:
            cp.wait_send()
        mine.wait()

    return pl.pallas_call(
        body, name=name, out_shape=S((8, r, cdim), blk.dtype), in_specs=[_ANY], out_specs=_ANY,
        scratch_shapes=[pltpu.SemaphoreType.DMA((7,)), pltpu.SemaphoreType.DMA((7,)), pltpu.SemaphoreType.DMA])(blk)


def _pair_swap_halves(name, g):
    n_s, _, r, cdim = g.shape

    def body(g_ref, own_ref, recv_ref, send_sems, recv_sems, local_sems):
        x, y, c = _place()
        local = [pltpu.make_async_copy(g_ref.at[s, c], own_ref.at[s], local_sems.at[s]) for s in range(n_s)]
        remote = [pltpu.make_async_remote_copy(src_ref=g_ref.at[s, 1 - c], dst_ref=recv_ref.at[s],
                                               send_sem=send_sems.at[s], recv_sem=recv_sems.at[s],
                                               device_id=(x, y, 1 - c), device_id_type=MESH) for s in range(n_s)]
        for cp in local + remote:
            cp.start()
        for cp in remote:
            cp.wait()
        for cp in local:
            cp.wait()

    return pl.pallas_call(
        body, name=name, out_shape=[S((n_s, r, cdim), g.dtype)] * 2, in_specs=[_ANY], out_specs=[_ANY, _ANY],
        scratch_shapes=[pltpu.SemaphoreType.DMA((n_s,))] * 3)(g)


def _chip_exchange(name, p):
    _, r, cdim = p.shape

    def body(p_ref, own_ref, recv_ref, send_sems, recv_sems, local_sem):
        x, y, c = _place()
        peers = [(1 - x, y), (x, 1 - y), (1 - x, 1 - y)]
        local = pltpu.make_async_copy(p_ref.at[2 * x + y], own_ref, local_sem)
        local.start()
        remote = [pltpu.make_async_remote_copy(src_ref=p_ref.at[2 * px + py], dst_ref=recv_ref.at[k],
                                               send_sem=send_sems.at[k], recv_sem=recv_sems.at[k],
                                               device_id=(px, py, c), device_id_type=MESH)
                  for k, (px, py) in enumerate(peers)]
        for cp in remote:
            cp.start()
        for cp in remote:
            cp.wait()
        local.wait()

    return pl.pallas_call(
        body, name=name, out_shape=[S((r, cdim), p.dtype), S((3, r, cdim), p.dtype)], in_specs=[_ANY],
        out_specs=[_ANY, _ANY],
        scratch_shapes=[pltpu.SemaphoreType.DMA((3,)), pltpu.SemaphoreType.DMA((3,)), pltpu.SemaphoreType.DMA])(p)


def _pair_share(name, t):
    r, cdim = t.shape

    def body(t_ref, out_ref, send_sem, recv_sem, local_sem):
        x, y, c = _place()
        local = pltpu.make_async_copy(t_ref, out_ref.at[c], local_sem)
        remote = pltpu.make_async_remote_copy(src_ref=t_ref, dst_ref=out_ref.at[c], send_sem=send_sem,
                                              recv_sem=recv_sem, device_id=(x, y, 1 - c), device_id_type=MESH)
        local.start()
        remote.start()
        remote.wait()
        local.wait()

    return pl.pallas_call(
        body, name=name, out_shape=S((2, r, cdim), t.dtype), in_specs=[_ANY], out_specs=_ANY,
        scratch_shapes=[pltpu.SemaphoreType.DMA, pltpu.SemaphoreType.DMA, pltpu.SemaphoreType.DMA])(t)


def _ew_tile(rows, cols, n_arrays):
    cap = min(1024, max(16, (24 * 1024 * 1024) // (n_arrays * 2 * cols * 4)))
    if rows <= 16:
        return rows
    mult = 16 if any(rows % t == 0 for t in range(16, cap + 1, 16)) else 8
    return _div_tile(rows, cap, mult)


def _reduce_scatter_grad(name, gb):
    _, _, r, cdim = gb.shape
    own, recv = _pair_swap_halves(name + "_swap", gb)
    tm = _ew_tile(4 * r, cdim, 3)
    p = _rows(name + "_sum2", lambda i, a, b: (a.astype(F32) + b.astype(F32),), 4 * r, tm,
              [R(own.reshape(4 * r, cdim)), R(recv.reshape(4 * r, cdim))], [(cdim, BF16)])[0]
    mine, others = _chip_exchange(name + "_xchg", p.reshape(4, r, cdim))
    tm = _ew_tile(r, cdim, 5)
    tot = _rows(name + "_sum4",
                lambda i, a, b, c, d: (((a.astype(F32) + b.astype(F32)) + c.astype(F32)) + d.astype(F32),), r, tm,
                [R(mine), R(others[0]), R(others[1]), R(others[2])], [(cdim, F32)])[0]
    return _pair_share(name + "_share", tot).reshape(2 * r, cdim)


def _ada_fwd(name, a16, w_ada, b_cols):
    depth, d, cols = w_ada.shape
    tn = _div_tile(cols, 1536, LANE)

    def body(a_ref, w_ref, b_ref, o_ref):
        a = a_ref[...]
        o_ref[...] = _dg(a * jax.nn.sigmoid(a), w_ref[...], _NN) + b_ref[...]

    return pl.pallas_call(
        body, name=name, grid=(depth, cols // tn),
        in_specs=[pl.BlockSpec((16, d), lambda l, j: (0, 0)), pl.BlockSpec((None, d, tn), lambda l, j: (l, 0, j)),
                  pl.BlockSpec((None, 1, tn), lambda l, j: (l, 0, j))],
        out_specs=pl.BlockSpec((None, 16, tn), lambda l, j: (l, 0, j)), out_shape=S((depth, 16, cols), F32),
        compiler_params=_cparams(("parallel", "parallel")))(a16, w_ada, b_cols)


def _ada_bwd_w(name, a_t, dmod):
    depth, _, cols = dmod.shape
    d = a_t.shape[0]
    tm, tn = _div_tile(d, 512, 8), _div_tile(cols, 1536, LANE)

    def body(a_ref, g_ref, o_ref):
        a = a_ref[...]
        o_ref[...] = _dg(a * jax.nn.sigmoid(a), g_ref[...], _NN)

    return pl.pallas_call(
        body, name=name, grid=(depth, d // tm, cols // tn),
        in_specs=[pl.BlockSpec((tm, 16), lambda l, i, j: (i, 0)), pl.BlockSpec((None, 16, tn), lambda l, i, j: (l, 0, j))],
        out_specs=pl.BlockSpec((None, tm, tn), lambda l, i, j: (l, i, j)), out_shape=S((depth, d, cols), F32),
        compiler_params=_cparams(("parallel", "parallel", "parallel")))(a_t, dmod)


def _ada_bwd_a(name, dmod, w_ada):
    depth, d, cols = w_ada.shape
    tn = _div_tile(cols, 1536, LANE)
    nj = cols // tn

    def body(g_ref, w_ref, o_ref):
        first = jnp.logical_and(pl.program_id(0) == 0, pl.program_id(1) == 0)
        p = _dg(g_ref[...], w_ref[...], _NT)

        @pl.when(first)
        def _():
            o_ref[...] = p

        @pl.when(jnp.logical_not(first))
        def _():
            o_ref[...] += p

    return pl.pallas_call(
        body, name=name, grid=(depth, nj),
        in_specs=[pl.BlockSpec((None, 16, tn), lambda l, j: (l, 0, j)), pl.BlockSpec((None, d, tn), lambda l, j: (l, 0, j))],
        out_specs=pl.BlockSpec((16, d), lambda l, j: (0, 0)), out_shape=S((16, d), F32),
        compiler_params=_cparams(("arbitrary", "arbitrary")))(dmod, w_ada)


def _lbs_fn(p):
    depth = p.shape[0]
    rows = [p[l] for l in range(depth)]
    mx = functools.reduce(jnp.maximum, rows)
    ex = [jnp.exp(r - mx) for r in rows]
    den = functools.reduce(lambda a, b: a + b, ex)
    sm = [e / den for e in ex]
    out, run = [], None
    for l in range(depth):
        run = sm[l] if run is None else run + sm[l]
        out.append(run - sm[0])
    return jnp.stack(out, axis=0)


def _lbs_fwd(name, p):
    def body(p_ref, o_ref):
        o_ref[...] = _lbs_fn(p_ref[...])

    return pl.pallas_call(body, name=name, out_shape=S(p.shape, F32))(p)


def _lbs_bwd(name, p, d_out):
    def body(p_ref, g_ref, o_ref):
        _, vjp = jax.vjp(_lbs_fn, p_ref[...])
        o_ref[...] = vjp(g_ref[...])[0]

    return pl.pallas_call(body, name=name, out_shape=S(p.shape, F32))(p, d_out)


def _sum8(name, g):
    _, r, cdim = g.shape
    tm = _ew_tile(r, cdim, 9)

    def body(g_ref, o_ref):
        acc = g_ref[0]
        for k in range(1, 8):
            acc = acc + g_ref[k]
        o_ref[...] = acc

    return pl.pallas_call(body, name=name, grid=(r // tm,),
                          in_specs=[pl.BlockSpec((8, tm, cdim), lambda i: (0, i, 0))],
                          out_specs=pl.BlockSpec((tm, cdim), lambda i: (i, 0)), out_shape=S((r, cdim), F32),
                          compiler_params=_cparams(("parallel",)))(g)


def _adamw(name, w, m, v, g):
    rows, cols = w.shape
    tm = _ew_tile(rows, cols, 7)

    def fn(i, w, m, v, g):
        m = ADAM_B1 * m + (1.0 - ADAM_B1) * g
        v = ADAM_B2 * v + (1.0 - ADAM_B2) * jnp.square(g)
        m_hat = m / (1.0 - ADAM_B1 ** ADAM_STEP)
        v_hat = v / (1.0 - ADAM_B2 ** ADAM_STEP)
        return -ADAM_LR * (m_hat / (jnp.sqrt(v_hat) + ADAM_EPS) + ADAM_WD * w), m, v

    return _rows(name, fn, rows, tm, [R(w), R(m), R(v), R(g)], [(cols, F32)] * 3)


def _silu_grad_mul(name, g, z):
    def body(g_ref, z_ref, o_ref):
        zz = z_ref[...]
        sg = jax.nn.sigmoid(zz)
        o_ref[...] = g_ref[...] * (sg * (1.0 + zz * (1.0 - sg)))

    return pl.pallas_call(body, name=name, out_shape=S(g.shape, F32))(g, z)


def _pack(arrs):
    parts, meta, off = [], [], 0
    for a in arrs:
        n = int(np.prod(a.shape))
        rows = -(-n // (8 * LANE)) * 8
        flat = a.reshape(-1).astype(F32)
        parts.append(jnp.pad(flat, (0, rows * LANE - n)).reshape(rows, LANE))
        meta.append((off, rows, n, a.shape))
        off += rows
    return jnp.concatenate(parts, axis=0), meta


def _unpack(buf, meta, lead=()):
    out = []
    for off, rows, n, shape in meta:
        seg = buf[..., off:off + rows, :].reshape(*lead, rows * LANE)[..., :n]
        out.append(seg.reshape(*lead, *shape))
    return out


_SMALL = ("c_ctx", "b_ada", "norm1_g", "q_norm_g", "k_norm_g", "hg_lower_bounds", "hg_norm_g", "sg_norm_g", "sg_w",
          "sg_b", "norm2_g", "conv_w", "conv_b", "final_norm_g")
_BIG = ("w_ada", "w_in", "w_out", "w_up", "w_down")
_WEIGHTS = ("c_ctx", "w_ada", "b_ada", "norm1_g", "w_in", "q_norm_g", "k_norm_g", "hg_lower_bounds", "hg_norm_g",
            "sg_norm_g", "sg_w", "sg_b", "w_out", "norm2_g", "w_up", "conv_w", "conv_b", "w_down", "final_norm_g")


def _dims_of(x, ctx, w_in, w_down):
    return Dims(d_model=x.shape[-1], seq=x.shape[1], ctx_len=ctx.shape[1], depth=w_in.shape[0],
                d_ff=w_down.shape[1] * 4)


def _step(dm, x, c, ctx, tgt, w, m, v):
    d, depth = dm.d_model, dm.depth
    xi, yi, ci = _place()
    chip = 2 * xi + yi
    me = 4 * xi + 2 * yi + ci
    n_chips = 4
    take_chips = lambda g8: g8[0::2]

    small_in, meta_in = _pack([c, w["conv_w"], w["hg_lower_bounds"]])
    gath = _all_gather8("ag_small_in", small_in)
    c_all, conv_sh, lb_sh = _unpack(gath, meta_in, lead=(8,))
    c_all = c_all.reshape(8, d)
    conv_w = take_chips(conv_sh).transpose(1, 2, 0, 3).reshape(depth, 3, 2 * dm.d_ff)
    lb_logits = take_chips(lb_sh).transpose(1, 2, 0, 3).reshape(2, depth, dm.hg_w)
    lb_p = lb_logits.transpose(1, 0, 2)
    lbs = _lbs_fwd("lbs_fwd", lb_p)

    a16 = jnp.concatenate([c_all, w["c_ctx"][None], jnp.zeros((7, d), F32)], axis=0)
    cols = w["w_ada"].shape[-1]
    b_cols = lax.dynamic_slice_in_dim(w["b_ada"], chip * cols, cols, axis=1)[:, None, :]
    mod_sh = _ada_fwd("ada_fwd", a16, w["w_ada"], b_cols)
    mod_g = take_chips(_all_gather8("ag_mod", mod_sh.reshape(depth * 16, cols)))
    mod_all = mod_g.reshape(n_chips, depth, 16, cols).transpose(1, 2, 0, 3).reshape(depth, 16, n_chips * cols)
    mod_lat = lax.dynamic_index_in_dim(mod_all, me, axis=1, keepdims=False)
    mod_ctx = mod_all[:, 8]
    mods = [[jnp.stack([mod_ctx[l, k * d:(k + 1) * d], mod_lat[l, k * d:(k + 1) * d]]) for k in range(N_MOD)]
            for l in range(depth)]

    def gather_rows(name, shard):
        half = shard.shape[0] // 2
        mine = lax.dynamic_slice_in_dim(shard.astype(BF16), ci * half, half, axis=0)
        return _all_gather8(name, mine)

    wls = []
    for l in range(depth):
        w_in = gather_rows(f"ag_w_in{l}", w["w_in"][l])
        w_up = gather_rows(f"ag_w_up{l}", w["w_up"][l])
        w_out = gather_rows(f"ag_w_out{l}", w["w_out"][l])
        w_down = gather_rows(f"ag_w_down{l}", w["w_down"][l])
        wls.append(dict(
            w_in=w_in.reshape(n_chips, d, -1), w_up=w_up.reshape(n_chips, d, -1),
            w_out=w_out.reshape(dm.d_mix, d), w_down=w_down.reshape(dm.d_ff, d),
            conv_w=conv_w[l], conv_b=w["conv_b"][l][None], norm1_g=w["norm1_g"][l][None],
            norm2_g=w["norm2_g"][l][None], q_g=w["q_norm_g"][l][None], k_g=w["k_norm_g"][l][None],
            hg_g=w["hg_norm_g"][l][None], sg_g=w["sg_norm_g"][l][None], sg_w=w["sg_w"][l],
            sg_bcol=w["sg_b"][l][:, :, None], lb=lbs[l].reshape(2 * dm.hg_heads, 1, LANE)))

    x_all = jnp.concatenate([ctx[0], x[0]], axis=0)
    loss_row, dx_all, grads, dfg = _sample_step(dm, x_all, tgt[0], mods, wls, w["final_norm_g"][None],
                                                _rope_tables(dm))
    loss = lax.psum(loss_row[0, 0], ("x", "y", "c"))
    grad_x = dx_all[dm.ctx_len:][None]

    g_big = {}
    for name in ("w_in", "w_up", "w_out", "w_down"):
        per_layer = []
        for l in range(depth):
            gb = grads[l][name]
            rows = w[name].shape[1] // 2
            per_layer.append(_reduce_scatter_grad(f"rs_{name}{l}", gb.reshape(n_chips, 2, rows, gb.shape[-1])))
        g_big[name] = jnp.stack(per_layer)

    dmod_lat = jnp.stack([jnp.concatenate([grads[l][f"mod{k}"][1] for k in range(N_MOD)]) for l in range(depth)])
    dmod_ctx = jnp.stack([jnp.concatenate([grads[l][f"mod{k}"][0] for k in range(N_MOD)]) for l in range(depth)])
    d_lbs = jnp.stack([grads[l]["lb"].reshape(2, dm.hg_w) for l in range(depth)])
    d_lb_p = _lbs_bwd("lbs_bwd", lb_p, d_lbs).transpose(1, 0, 2)
    stk = lambda key: jnp.stack([grads[l][key] for l in range(depth)])
    part = {
        "b_ada": dmod_lat + dmod_ctx, "norm1_g": stk("norm1_g")[:, 0], "q_norm_g": stk("q_g")[:, 0],
        "k_norm_g": stk("k_g")[:, 0], "hg_lower_bounds": d_lb_p, "hg_norm_g": stk("hg_g")[:, 0],
        "sg_norm_g": stk("sg_g")[:, 0], "sg_w": stk("sg_w"), "sg_b": stk("sg_bcol")[..., 0],
        "norm2_g": stk("norm2_g")[:, 0], "conv_w": stk("conv_w"), "conv_b": stk("conv_b")[:, 0],
        "final_norm_g": dfg[0]}
    names = [n for n in _SMALL if n != "c_ctx"]
    packed, meta = _pack([part[n] for n in names] + [dmod_ctx, dmod_lat])
    gath = _all_gather8("ag_small_grads", packed)
    summed = _unpack(_sum8("sum_small_grads", gath), meta)
    g_small = dict(zip(names, summed[:len(names)]))
    dmod_ctx_tot = summed[len(names)]
    dmod_lat_all = _unpack(gath, meta[-1:], lead=(8,))[0]

    dmod16 = jnp.concatenate([dmod_lat_all.transpose(1, 0, 2), dmod_ctx_tot[:, None], jnp.zeros((depth, 7, 6 * d), F32)],
                             axis=1)
    dmod16 = lax.dynamic_slice_in_dim(dmod16, chip * cols, cols, axis=2)
    g_big["w_ada"] = _ada_bwd_w("ada_bwd_w", a16.T, dmod16)
    da16 = _ada_bwd_a("ada_bwd_a", dmod16, w["w_ada"])
    da_g = take_chips(_all_gather8("ag_dctx", da16))
    da_sum = _rows("sum_dctx", lambda i, a, b, c2, d2: (((a + b) + c2) + d2,), 16, 16,
                   [R(da_g[k]) for k in range(n_chips)], [(d, F32)])[0]
    g_small["c_ctx"] = _silu_grad_mul("dctx_silu", da_sum[8:9], w["c_ctx"][None])[0]

    g_small["conv_w"] = lax.dynamic_slice_in_dim(g_small["conv_w"], chip * w["conv_w"].shape[-1], w["conv_w"].shape[-1], axis=2)
    g_small["hg_lower_bounds"] = lax.dynamic_slice_in_dim(g_small["hg_lower_bounds"], chip * w["hg_lower_bounds"].shape[-1],
                                                          w["hg_lower_bounds"].shape[-1], axis=2)

    grads_out, deltas, new_m, new_v = {}, {}, {}, {}
    for name in _BIG:
        shp = w[name].shape
        flat = lambda a: a.reshape(-1, shp[-1])
        dl, nm, nv = _adamw(f"adamw_{name}", flat(w[name]), flat(m[name]), flat(v[name]), flat(g_big[name]))
        grads_out[name] = g_big[name].reshape(shp)
        deltas[name], new_m[name], new_v[name] = dl.reshape(shp), nm.reshape(shp), nv.reshape(shp)
    pw, meta_s = _pack([w[n] for n in _SMALL])
    pm, _ = _pack([m[n] for n in _SMALL])
    pv, _ = _pack([v[n] for n in _SMALL])
    pg, _ = _pack([g_small[n].reshape(w[n].shape) for n in _SMALL])
    dl, nm, nv = _adamw("adamw_small", pw, pm, pv, pg)
    for name, a, b, c2 in zip(_SMALL, _unpack(dl, meta_s), _unpack(nm, meta_s), _unpack(nv, meta_s)):
        grads_out[name] = g_small[name].reshape(w[name].shape)
        deltas[name], new_m[name], new_v[name] = a, b, c2
    return loss, grad_x, grads_out, deltas, new_m, new_v


def kernel(x, c, ctx, c_ctx, w_ada, b_ada, norm1_g, w_in, q_norm_g, k_norm_g, hg_lower_bounds, hg_norm_g, sg_norm_g, sg_w, sg_b, w_out, norm2_g, w_up, conv_w, conv_b, w_down, final_norm_g, loss_target, m_c_ctx, m_w_ada, m_b_ada, m_norm1_g, m_w_in, m_q_norm_g, m_k_norm_g, m_hg_lower_bounds, m_hg_norm_g, m_sg_norm_g, m_sg_w, m_sg_b, m_w_out, m_norm2_g, m_w_up, m_conv_w, m_conv_b, m_w_down, m_final_norm_g, v_c_ctx, v_w_ada, v_b_ada, v_norm1_g, v_w_in, v_q_norm_g, v_k_norm_g, v_hg_lower_bounds, v_hg_norm_g, v_sg_norm_g, v_sg_w, v_sg_b, v_w_out, v_norm2_g, v_w_up, v_conv_w, v_conv_b, v_w_down, v_final_norm_g):
    w = dict(c_ctx=c_ctx, w_ada=w_ada, b_ada=b_ada, norm1_g=norm1_g, w_in=w_in, q_norm_g=q_norm_g, k_norm_g=k_norm_g, hg_lower_bounds=hg_lower_bounds, hg_norm_g=hg_norm_g, sg_norm_g=sg_norm_g, sg_w=sg_w, sg_b=sg_b, w_out=w_out, norm2_g=norm2_g, w_up=w_up, conv_w=conv_w, conv_b=conv_b, w_down=w_down, final_norm_g=final_norm_g)
    m = dict(c_ctx=m_c_ctx, w_ada=m_w_ada, b_ada=m_b_ada, norm1_g=m_norm1_g, w_in=m_w_in, q_norm_g=m_q_norm_g, k_norm_g=m_k_norm_g, hg_lower_bounds=m_hg_lower_bounds, hg_norm_g=m_hg_norm_g, sg_norm_g=m_sg_norm_g, sg_w=m_sg_w, sg_b=m_sg_b, w_out=m_w_out, norm2_g=m_norm2_g, w_up=m_w_up, conv_w=m_conv_w, conv_b=m_conv_b, w_down=m_w_down, final_norm_g=m_final_norm_g)
    v = dict(c_ctx=v_c_ctx, w_ada=v_w_ada, b_ada=v_b_ada, norm1_g=v_norm1_g, w_in=v_w_in, q_norm_g=v_q_norm_g, k_norm_g=v_k_norm_g, hg_lower_bounds=v_hg_lower_bounds, hg_norm_g=v_hg_norm_g, sg_norm_g=v_sg_norm_g, sg_w=v_sg_w, sg_b=v_sg_b, w_out=v_w_out, norm2_g=v_norm2_g, w_up=v_w_up, conv_w=v_conv_w, conv_b=v_conv_b, w_down=v_w_down, final_norm_g=v_final_norm_g)
    dm = _dims_of(x, ctx, w_in, w_down)
    loss, grad_x, g, dl, nm, nv = _step(dm, x, c, ctx, loss_target, w, m, v)
    return (loss, grad_x, *[g[n] for n in _WEIGHTS], *[dl[n] for n in _WEIGHTS], *[nm[n] for n in _WEIGHTS],
            *[nv[n] for n in _WEIGHTS])
```

```python
import functools
import math
from typing import NamedTuple

import numpy as np
import jax
import jax.numpy as jnp
from jax import lax
from jax.experimental import pallas as pl
from jax.experimental.pallas import tpu as pltpu

F32, BF16 = jnp.float32, jnp.bfloat16
S = jax.ShapeDtypeStruct
MESH = pl.DeviceIdType.MESH

LANE = 128
EPS = 1e-6
F_MIN = 1e-30
ROPE_THETA = 10000.0
N_MOD = 6
ADAM_LR, ADAM_B1, ADAM_B2, ADAM_EPS, ADAM_WD, ADAM_STEP = 0.001, 0.9, 0.999, 1e-08, 0.01, 10
VMEM_LIMIT = 56 * 1024 * 1024


class Dims(NamedTuple):
    d_model: int = 2048
    seq: int = 4096
    ctx_len: int = 256
    grid_w: int = 64
    depth: int = 4
    attn_heads: int = 8
    kv_heads: int = 2
    hg_heads: int = 4
    hg_chunk: int = 16
    sg_groups: int = 4
    d_ff: int = 5632

    @property
    def n_tok(self):
        return self.seq + self.ctx_len

    @property
    def q_w(self):
        return self.attn_heads * LANE

    @property
    def kv_w(self):
        return self.kv_heads * LANE

    @property
    def hg_w(self):
        return self.hg_heads * LANE

    @property
    def sg_w(self):
        return self.sg_groups * LANE

    @property
    def d_mix(self):
        return self.q_w + self.hg_w + self.sg_w

    @property
    def in_sizes(self):
        return (self.q_w, self.kv_w, self.kv_w) + (self.hg_w,) * 5 + (self.sg_w,) * 2

    @property
    def in_cols(self):
        return sum(self.in_sizes)

    @property
    def in_offs(self):
        return tuple(int(v) for v in np.cumsum((0,) + self.in_sizes)[:-1])

    @property
    def row_tile(self):
        return min(256, self.ctx_len)


def _cparams(sem, vmem=VMEM_LIMIT):
    return pltpu.CompilerParams(dimension_semantics=sem, vmem_limit_bytes=vmem)


_NN, _NT, _TN = ((1,), (0,)), ((1,), (1,)), ((0,), (0,))


def _dg(a, b, dims):
    return lax.dot_general(a.astype(BF16), b.astype(BF16), (dims, ((), ())), preferred_element_type=F32)


@jax.custom_vjp
def _bdot_nn(a, b):
    return _dg(a, b, _NN)


@jax.custom_vjp
def _bdot_nt(a, b):
    return _dg(a, b, _NT)


@jax.custom_vjp
def _bdot_tn(a, b):
    return _dg(a, b, _TN)


_bdot_nn.defvjp(lambda a, b: (_dg(a, b, _NN), (a, b)),
                lambda r, g: (_bdot_nt(g, r[1]).astype(r[0].dtype), _bdot_tn(r[0], g).astype(r[1].dtype)))
_bdot_nt.defvjp(lambda a, b: (_dg(a, b, _NT), (a, b)),
                lambda r, g: (_bdot_nn(g, r[1]).astype(r[0].dtype), _bdot_tn(g, r[0]).astype(r[1].dtype)))
_bdot_tn.defvjp(lambda a, b: (_dg(a, b, _TN), (a, b)),
                lambda r, g: (_bdot_nt(r[1], g).astype(r[0].dtype), _bdot_nn(r[0], g).astype(r[1].dtype)))


def _f32dot(a, b):
    return lax.dot_general(a, b, (_NN, ((), ())), precision=lax.Precision.HIGHEST, preferred_element_type=F32)


@jax.custom_vjp
def _tri_dot(tri, tri_t, x):
    return _f32dot(tri, x)


_tri_dot.defvjp(lambda tri, tri_t, x: (_f32dot(tri, x), (tri, tri_t)),
                lambda r, g: (jnp.zeros_like(r[0]), jnp.zeros_like(r[1]), _f32dot(r[1], g)))


@jax.custom_vjp
def _pair_swap(x):
    lane = lax.broadcasted_iota(jnp.int32, x.shape, x.ndim - 1)
    return jnp.where(lane % 2 == 0, pltpu.roll(x, LANE - 1, x.ndim - 1), pltpu.roll(x, 1, x.ndim - 1))


_pair_swap.defvjp(lambda x: (_pair_swap(x), None), lambda _, g: (_pair_swap(g),))


def R(a, w=None, cb=0, rmap=None):
    return ("r", a, a.shape[1] if w is None else w, cb, rmap)


def C(a):
    return ("c", a)


def _rows(name, fn, n_rows, tm, ins, outs, accs=()):
    n_in, n_out = len(ins), len(outs)
    in_specs, args = [], []
    for e in ins:
        if e[0] == "r":
            _, a, w, cb, rmap = e
            assert w % LANE == 0 or w == a.shape[1]
            if rmap is None:
                in_specs.append(pl.BlockSpec((tm, w), functools.partial(lambda i, cb: (i, cb), cb=cb)))
            else:
                in_specs.append(pl.BlockSpec((tm, w), functools.partial(lambda i, cb, rm: (rm(i), cb), cb=cb, rm=rmap)))
        else:
            a = e[1]
            in_specs.append(pl.BlockSpec(a.shape, functools.partial(lambda i, nd: (0,) * nd, nd=a.ndim)))
        args.append(a)
    out_shape = [S((n_rows, w), dt) for w, dt in outs] + [S(tuple(sh), F32) for sh in accs]
    out_specs = [pl.BlockSpec((tm, w), lambda i: (i, 0)) for w, _ in outs]
    out_specs += [pl.BlockSpec(tuple(sh), functools.partial(lambda i, nd: (0,) * nd, nd=len(sh))) for sh in accs]

    def body(*refs):
        i = pl.program_id(0)
        vals = fn(i, *[r[...] for r in refs[:n_in]])
        assert len(vals) == n_out + len(accs), (name, len(vals))
        for r, v in zip(refs[n_in:n_in + n_out], vals[:n_out]):
            r[...] = v.astype(r.dtype)
        for r, v in zip(refs[n_in + n_out:], vals[n_out:]):
            def init(r=r, v=v):
                r[...] = v.astype(F32)

            def add(r=r, v=v):
                r[...] += v.astype(F32)

            pl.when(i == 0)(init)
            pl.when(i != 0)(add)

    res = pl.pallas_call(body, name=name, grid=(n_rows // tm,), in_specs=in_specs, out_specs=out_specs,
                         out_shape=out_shape, compiler_params=_cparams(("arbitrary",)))(*args)
    return res


def _div_tile(n, cap, mult):
    if n <= cap:
        return n
    best = None
    for t in range(mult, cap + 1, mult):
        if n % t == 0:
            best = t
    assert best is not None, (n, cap, mult)
    return best


def _matmul(name, a, b, form, out_dtype, b_shards=1, out_shards=1, caps=(1088, 1408, 1408)):
    if form == "tn":
        K, M = a.shape
    else:
        M, K = a.shape
    if form == "nn":
        N = b.shape[-1] * b_shards
    elif form == "nt":
        N = b.shape[-2]
    else:
        N = b.shape[1]
    n_per = N // (b_shards if form == "nn" else out_shards)
    k_per = K // (b_shards if form == "nt" else 1)
    tm = _div_tile(M, caps[0], 16 if form != "tn" else LANE)
    tn = _div_tile(n_per, caps[1], LANE)
    tk = _div_tile(k_per, caps[2], LANE if form != "tn" else 16)
    nk = K // tk
    grid = (M // tm, N // tn, nk)
    nps, kps = n_per // tn, k_per // tk

    if form == "tn":
        a_spec = pl.BlockSpec((tk, tm), lambda i, j, k: (k, i))
    else:
        a_spec = pl.BlockSpec((tm, tk), lambda i, j, k: (i, k))
    if form == "nn":
        if b_shards > 1:
            b_spec = pl.BlockSpec((None, tk, tn), lambda i, j, k: (j // nps, k, j % nps))
        else:
            b_spec = pl.BlockSpec((tk, tn), lambda i, j, k: (k, j))
    elif form == "nt":
        if b_shards > 1:
            b_spec = pl.BlockSpec((None, tn, tk), lambda i, j, k: (k // kps, j, k % kps))
        else:
            b_spec = pl.BlockSpec((tn, tk), lambda i, j, k: (j, k))
    else:
        b_spec = pl.BlockSpec((tk, tn), lambda i, j, k: (k, j))
    if out_shards > 1:
        o_spec = pl.BlockSpec((None, tm, tn), lambda i, j, k: (j // nps, i, j % nps))
        o_shape = S((out_shards, M, n_per), out_dtype)
    else:
        o_spec = pl.BlockSpec((tm, tn), lambda i, j, k: (i, j))
        o_shape = S((M, N), out_dtype)
    dims = {"nn": _NN, "nt": _NT, "tn": _TN}[form]

    def body(a_ref, b_ref, o_ref, acc_ref):
        k = pl.program_id(2)
        p = _dg(a_ref[...], b_ref[...], dims)
        if nk == 1:
            o_ref[...] = p.astype(o_ref.dtype)
        else:
            @pl.when(k == 0)
            def _():
                acc_ref[...] = p

            @pl.when(jnp.logical_and(k > 0, k < nk - 1))
            def _():
                acc_ref[...] += p

            @pl.when(k == nk - 1)
            def _():
                o_ref[...] = (acc_ref[...] + p).astype(o_ref.dtype)

    return pl.pallas_call(body, name=name, grid=grid, in_specs=[a_spec, b_spec], out_specs=o_spec, out_shape=o_shape,
                          scratch_shapes=[pltpu.VMEM((tm, tn), F32)],
                          compiler_params=_cparams(("parallel", "parallel", "arbitrary")))(a, b)


def _rms(x, g):
    return x * lax.rsqrt(jnp.mean(x * x, axis=-1, keepdims=True) + EPS) * g


def _sel2(mm, is_ctx):
    return jnp.where(is_ctx, mm[0:1], mm[1:2])


def _put2(v, is_ctx):
    row = lax.broadcasted_iota(jnp.int32, (2, v.shape[-1]), 0)
    return jnp.where(row == jnp.where(is_ctx, 0, 1), v, 0.0)


def _normmod(x, g, sh, sc):
    return _rms(x, g) * (1.0 + sc) + sh


def _colsum(v):
    return jnp.sum(v, axis=0, keepdims=True)


def _normmod_fwd(dm, name, x, g, sh2, sc2):
    nc = dm.ctx_len // dm.row_tile

    def fn(i, x, g, sh2, sc2):
        is_ctx = i < nc
        return (_normmod(x, g, _sel2(sh2, is_ctx), _sel2(sc2, is_ctx)),)

    return _rows(name, fn, dm.n_tok, dm.row_tile, [R(x), C(g), C(sh2), C(sc2)], [(dm.d_model, BF16)])[0]


def _resnorm_fwd(dm, name, x, y, gate2, g, sh2, sc2):
    nc = dm.ctx_len // dm.row_tile

    def fn(i, x, y, gate2, g, sh2, sc2):
        is_ctx = i < nc
        x1 = x + _sel2(gate2, is_ctx) * y
        return x1, _normmod(x1, g, _sel2(sh2, is_ctx), _sel2(sc2, is_ctx))

    return _rows(name, fn, dm.n_tok, dm.row_tile, [R(x), R(y), C(gate2), C(g), C(sh2), C(sc2)],
                 [(dm.d_model, F32), (dm.d_model, BF16)])


def _res_fwd(dm, name, x, y, gate2):
    nc = dm.ctx_len // dm.row_tile

    def fn(i, x, y, gate2):
        return (x + _sel2(gate2, i < nc) * y,)

    return _rows(name, fn, dm.n_tok, dm.row_tile, [R(x), R(y), C(gate2)], [(dm.d_model, F32)])[0]


def _gate_bwd(dm, name, dx, y, gate2):
    nc = dm.ctx_len // dm.row_tile

    def fn(i, dx, y, gate2):
        is_ctx = i < nc
        return dx * _sel2(gate2, is_ctx), _put2(_colsum(dx * y), is_ctx)

    return _rows(name, fn, dm.n_tok, dm.row_tile, [R(dx), R(y), C(gate2)], [(dm.d_model, BF16)], [(2, dm.d_model)])


def _normmod_bwd(dm, name, x, dh, dres, g, sh2, sc2):
    nc = dm.ctx_len // dm.row_tile

    def fn(i, x, dh, dres, g, sh2, sc2):
        is_ctx = i < nc
        sh, sc = _sel2(sh2, is_ctx), _sel2(sc2, is_ctx)
        _, vjp = jax.vjp(_normmod, x, g, sh, sc)
        dx, dg, dsh, dsc = vjp(dh)
        return dres + dx, dg, _put2(dsh, is_ctx), _put2(dsc, is_ctx)

    return _rows(name, fn, dm.n_tok, dm.row_tile, [R(x), R(dh), R(dres), C(g), C(sh2), C(sc2)],
                 [(dm.d_model, F32)], [(1, dm.d_model), (2, dm.d_model), (2, dm.d_model)])


def _rope_tables(dm):
    t = jnp.arange(dm.seq)
    row = (t // dm.grid_w).astype(F32)
    col = (t % dm.grid_w).astype(F32)
    n_freq = LANE // 4
    inv = ROPE_THETA ** (-jnp.arange(n_freq, dtype=F32) / n_freq)
    ang = jnp.concatenate([row[:, None] * inv, col[:, None] * inv], axis=-1)
    cos, sin = jnp.cos(ang), jnp.sin(ang)
    ct = jnp.repeat(cos, 2, axis=-1)
    st = jnp.stack([-sin, sin], axis=-1).reshape(dm.seq, LANE)
    ct = jnp.concatenate([jnp.ones((dm.ctx_len, LANE), F32), ct], axis=0)
    st = jnp.concatenate([jnp.zeros((dm.ctx_len, LANE), F32), st], axis=0)
    return ct, st


def _qk_fn(n_q, n_k):
    def fn(aq, ak, ct, st, qg, kg):
        def head(x, g):
            y = _rms(x, g)
            return y * ct + _pair_swap(y) * st

        q = jnp.concatenate([head(aq[:, h * LANE:(h + 1) * LANE], qg) for h in range(n_q)], axis=1)
        k = jnp.concatenate([head(ak[:, h * LANE:(h + 1) * LANE], kg) for h in range(n_k)], axis=1)
        return q, k

    return fn


def _qk_fwd(dm, name, proj, ct, st, qg, kg):
    f = _qk_fn(dm.attn_heads, dm.kv_heads)
    o = dm.in_offs
    return _rows(name, lambda i, *a: f(*a), dm.n_tok, dm.row_tile,
                 [R(proj, dm.q_w, o[0] // dm.q_w), R(proj, dm.kv_w, o[1] // dm.kv_w), R(ct), R(st), C(qg), C(kg)],
                 [(dm.q_w, BF16), (dm.kv_w, BF16)])


def _qk_bwd(dm, name, proj, ct, st, qg, kg, dq, dk):
    f = _qk_fn(dm.attn_heads, dm.kv_heads)
    o = dm.in_offs

    def fn(i, aq, ak, ct, st, qg, kg, dq, dk):
        _, vjp = jax.vjp(lambda aq, ak, qg, kg: f(aq, ak, ct, st, qg, kg), aq, ak, qg, kg)
        return vjp((dq, dk))

    return _rows(name, fn, dm.n_tok, dm.row_tile,
                 [R(proj, dm.q_w, o[0] // dm.q_w), R(proj, dm.kv_w, o[1] // dm.kv_w), R(ct), R(st), C(qg), C(kg),
                  R(dq), R(dk)],
                 [(dm.q_w, BF16), (dm.kv_w, BF16)], [(1, LANE), (1, LANE)])


def _attn_probs(q, k, i, nc, ctx_len, n_tok):
    s = _dg(q, k, _NT) * (LANE ** -0.5)
    col = lax.broadcasted_iota(jnp.int32, (1, n_tok), 1)
    s = s + jnp.where(col < jnp.where(i < nc, ctx_len, n_tok), 0.0, -1e30)
    e = jnp.exp(s - jnp.max(s, axis=-1, keepdims=True))
    return e, 1.0 / jnp.sum(e, axis=-1, keepdims=True)


def _attn_fwd(dm, name, qh, kh, proj):
    tq, n = dm.row_tile, dm.n_tok
    nc, grp = dm.ctx_len // tq, dm.attn_heads // dm.kv_heads
    v_cb = dm.in_offs[2] // LANE

    def body(q_ref, k_ref, v_ref, o_ref):
        e, inv = _attn_probs(q_ref[...], k_ref[...], pl.program_id(1), nc, dm.ctx_len, n)
        o_ref[...] = (_dg(e, v_ref[...], _NN) * inv).astype(o_ref.dtype)

    return pl.pallas_call(
        body, name=name, grid=(dm.attn_heads, n // tq),
        in_specs=[pl.BlockSpec((tq, LANE), lambda h, i: (i, h)),
                  pl.BlockSpec((n, LANE), lambda h, i: (0, h // grp)),
                  pl.BlockSpec((n, LANE), lambda h, i: (0, v_cb + h // grp))],
        out_specs=pl.BlockSpec((tq, LANE), lambda h, i: (i, h)),
        out_shape=S((n, dm.q_w), BF16), compiler_params=_cparams(("parallel", "parallel")))(qh, kh, proj)


def _attn_bwd(dm, name, qh, kh, proj, attn, dmix):
    tq, n = dm.row_tile, dm.n_tok
    nc, grp = dm.ctx_len // tq, dm.attn_heads // dm.kv_heads
    v_cb = dm.in_offs[2] // LANE

    def body(q_ref, k_ref, v_ref, o_ref, do_ref, dq_ref, dk_ref, dv_ref):
        first = jnp.logical_and(pl.program_id(1) == 0, pl.program_id(2) == 0)
        q, k, v, do = q_ref[...], k_ref[...], v_ref[...], do_ref[...]
        e, inv = _attn_probs(q, k, pl.program_id(2), nc, dm.ctx_len, n)
        delta = jnp.sum(do * o_ref[...].astype(F32), axis=-1, keepdims=True)
        ds = e * ((_dg(do, v, _NT) - delta) * (inv * (LANE ** -0.5)))
        dq_ref[...] = _dg(ds, k, _NN)
        dk = _dg(ds, q, _TN)
        dv = _dg(e, do * inv, _TN)

        @pl.when(first)
        def _():
            dk_ref[...] = dk
            dv_ref[...] = dv

        @pl.when(jnp.logical_not(first))
        def _():
            dk_ref[...] += dk
            dv_ref[...] += dv

    return pl.pallas_call(
        body, name=name, grid=(dm.kv_heads, grp, n // tq),
        in_specs=[pl.BlockSpec((tq, LANE), lambda g, hh, i: (i, g * grp + hh)),
                  pl.BlockSpec((n, LANE), lambda g, hh, i: (0, g)),
                  pl.BlockSpec((n, LANE), lambda g, hh, i: (0, v_cb + g)),
                  pl.BlockSpec((tq, LANE), lambda g, hh, i: (i, g * grp + hh)),
                  pl.BlockSpec((tq, LANE), lambda g, hh, i: (i, g * grp + hh))],
        out_specs=[pl.BlockSpec((tq, LANE), lambda g, hh, i: (i, g * grp + hh)),
                   pl.BlockSpec((n, LANE), lambda g, hh, i: (0, g)),
                   pl.BlockSpec((n, LANE), lambda g, hh, i: (0, g))],
        out_shape=[S((n, dm.q_w), F32), S((n, dm.kv_w), F32), S((n, dm.kv_w), F32)],
        compiler_params=_cparams(("parallel", "arbitrary", "arbitrary")))(qh, kh, proj, attn, dmix)


def _hg_chunk(d, st, qraw, fraw, v, lb):
    c = qraw.shape[0]
    sig = jax.nn.sigmoid(fraw)
    f = lb + (1.0 - lb) * sig
    logf = jnp.log(jnp.maximum(f, F_MIN))
    k = (1.0 - lb) * jax.nn.sigmoid(-fraw)
    q = qraw * jax.nn.sigmoid(qraw)
    r_i = lax.broadcasted_iota(jnp.int32, (c, c), 0)
    c_i = lax.broadcasted_iota(jnp.int32, (c, c), 1)
    sgn = 1 - 2 * d
    tri = ((r_i - c_i) * sgn >= 0).astype(F32)
    tri_t = ((c_i - r_i) * sgn >= 0).astype(F32)
    b = _tri_dot(tri, tri_t, logf)
    b_last = jnp.sum(logf, axis=0, keepdims=True)
    trow = lax.broadcasted_iota(jnp.int32, (c, 1), 0)
    o = _bdot_nt(q * jnp.exp(b), st)
    for s in range(c):
        m = (trow - s) * sgn >= 0
        e = jnp.exp(jnp.where(m, b - b[s:s + 1], 0.0))
        w = jnp.where(m, q * k[s:s + 1] * e, 0.0)
        o = o + jnp.sum(w, axis=-1, keepdims=True) * v[s:s + 1]
    st_new = st * jnp.exp(b_last) + _bdot_tn(v, k * jnp.exp(b_last - b))
    return o, st_new


def _hg_blk(dm, tb):
    nbc, nbl = dm.ctx_len // tb, dm.seq // tb

    def blk(d, j):
        rev = jnp.where(j < nbc, nbc - 1 - j, 2 * nbc + nbl - 1 - j)
        return jnp.where(d == 0, j, rev)

    return blk, nbc + nbl


def _hgrn_fwd(dm, name, proj, lb):
    tb, n, hh, ck, hw = dm.row_tile, dm.n_tok, dm.hg_heads, dm.hg_chunk, dm.hg_w
    blk, nblk = _hg_blk(dm, tb)
    ncb = tb // ck
    o = dm.in_offs
    q_cb, f_cb, v_cb = o[3] // hw, o[4] // hw, o[6] // hw

    def body(q_ref, f_ref, v_ref, lb_ref, o_ref, hist_ref, st_ref):
        d = pl.program_id(0)

        @pl.when(pl.program_id(1) == 0)
        def _():
            st_ref[...] = jnp.zeros_like(st_ref)

        lbv = lb_ref[...]

        def chunk(ci, carry):
            c = jnp.where(d == 0, ci, ncb - 1 - ci)
            rows = pl.ds(pl.multiple_of(c * ck, ck), ck)
            for h in range(hh):
                cols = slice(h * LANE, (h + 1) * LANE)
                st = st_ref[h]
                hist_ref[h, c] = st
                oc, stn = _hg_chunk(d, st, q_ref[rows, cols], f_ref[rows, cols], v_ref[rows, cols], lbv[:, cols])
                o_ref[rows, cols] = oc
                st_ref[h] = stn
            return carry

        lax.fori_loop(0, ncb, chunk, 0)

    return pl.pallas_call(
        body, name=name, grid=(2, nblk),
        in_specs=[pl.BlockSpec((tb, hw), lambda d, j: (blk(d, j), q_cb)),
                  pl.BlockSpec((tb, hw), lambda d, j: (blk(d, j), f_cb + d)),
                  pl.BlockSpec((tb, hw), lambda d, j: (blk(d, j), v_cb)),
                  pl.BlockSpec((None, 1, hw), lambda d, j: (d, 0, 0))],
        out_specs=[pl.BlockSpec((None, tb, hw), lambda d, j: (d, blk(d, j), 0)),
                   pl.BlockSpec((None, hh, ncb, LANE, LANE), lambda d, j: (d, 0, blk(d, j), 0, 0))],
        out_shape=[S((2, n, hw), F32), S((2, hh, n // ck, LANE, LANE), F32)],
        scratch_shapes=[pltpu.VMEM((hh, LANE, LANE), F32)],
        compiler_params=_cparams(("parallel", "arbitrary")))(proj, proj, proj, lb)


def _hgrn_bwd(dm, name, proj, lb, hist, do):
    tb, n, hh, ck, hw = dm.row_tile, dm.n_tok, dm.hg_heads, dm.hg_chunk, dm.hg_w
    blk, nblk = _hg_blk(dm, tb)
    ncb = tb // ck
    o = dm.in_offs
    q_cb, f_cb, v_cb = o[3] // hw, o[4] // hw, o[6] // hw

    def rblk(d, j):
        return blk(d, nblk - 1 - j)

    def body(q_ref, f_ref, v_ref, lb_ref, hist_ref, do_ref, dq_ref, df_ref, dv_ref, dlb_ref, dst_ref):
        d = pl.program_id(0)

        @pl.when(pl.program_id(1) == 0)
        def _():
            dst_ref[...] = jnp.zeros_like(dst_ref)
            dlb_ref[...] = jnp.zeros_like(dlb_ref)

        lbv = lb_ref[...]

        def chunk(ci, carry):
            cp = ncb - 1 - ci
            c = jnp.where(d == 0, cp, ncb - 1 - cp)
            rows = pl.ds(pl.multiple_of(c * ck, ck), ck)
            for h in range(hh):
                cols = slice(h * LANE, (h + 1) * LANE)
                _, vjp = jax.vjp(functools.partial(_hg_chunk, d), hist_ref[h, c], q_ref[rows, cols],
                                 f_ref[rows, cols], v_ref[rows, cols], lbv[:, cols])
                dst, dq, df, dv, dlb = vjp((do_ref[rows, cols], dst_ref[h]))
                dq_ref[rows, cols] = dq
                df_ref[rows, cols] = df
                dv_ref[rows, cols] = dv
                dlb_ref[:, cols] += dlb
                dst_ref[h] = dst
            return carry

        lax.fori_loop(0, ncb, chunk, 0)

    row3 = pl.BlockSpec((None, tb, hw), lambda d, j: (d, rblk(d, j), 0))
    return pl.pallas_call(
        body, name=name, grid=(2, nblk),
        in_specs=[pl.BlockSpec((tb, hw), lambda d, j: (rblk(d, j), q_cb)),
                  pl.BlockSpec((tb, hw), lambda d, j: (rblk(d, j), f_cb + d)),
                  pl.BlockSpec((tb, hw), lambda d, j: (rblk(d, j), v_cb)),
                  pl.BlockSpec((None, 1, hw), lambda d, j: (d, 0, 0)),
                  pl.BlockSpec((None, hh, ncb, LANE, LANE), lambda d, j: (d, 0, rblk(d, j), 0, 0)),
                  pl.BlockSpec((tb, hw), lambda d, j: (rblk(d, j), 0))],
        out_specs=[row3, row3, row3, pl.BlockSpec((None, 1, hw), lambda d, j: (d, 0, 0))],
        out_shape=[S((2, n, hw), F32)] * 3 + [S((2, 1, hw), F32)],
        scratch_shapes=[pltpu.VMEM((hh, LANE, LANE), F32)],
        compiler_params=_cparams(("parallel", "arbitrary")))(proj, proj, proj, lb, hist, do)


def _hgc_fn(n_h):
    def fn(o0, o1, gt, g):
        osum = o0 + o1
        y = jnp.concatenate([_rms(osum[:, h * LANE:(h + 1) * LANE], g) for h in range(n_h)], axis=1)
        return y * (gt * jax.nn.sigmoid(gt))

    return fn


def _hgc_fwd(dm, name, o_dir, proj, g):
    f = _hgc_fn(dm.hg_heads)
    cb = dm.in_offs[7] // dm.hg_w
    return _rows(name, lambda i, *a: (f(*a),), dm.n_tok, dm.row_tile,
                 [R(o_dir[0]), R(o_dir[1]), R(proj, dm.hg_w, cb), C(g)], [(dm.hg_w, BF16)])[0]


def _hgc_bwd(dm, name, o_dir, proj, g, dmix):
    f = _hgc_fn(dm.hg_heads)
    cb = dm.in_offs[7] // dm.hg_w

    def fn(i, o0, o1, gt, g, dy):
        _, vjp = jax.vjp(f, o0, o1, gt, g)
        do, _, dgt, dg = vjp(dy)
        return do, dgt, dg

    return _rows(name, fn, dm.n_tok, dm.row_tile,
                 [R(o_dir[0]), R(o_dir[1]), R(proj, dm.hg_w, cb), C(g), R(dmix, dm.hg_w, dm.q_w // dm.hg_w)],
                 [(dm.hg_w, F32), (dm.hg_w, BF16)], [(1, LANE)])


def _sg_fn(n_g):
    def fn(su, sv, ng, w, bcol):
        u = jax.nn.gelu(su)
        gv = jax.nn.gelu(sv)
        outs = []
        for g in range(n_g):
            sl = slice(g * LANE, (g + 1) * LANE)
            vn = _rms(gv[:, sl], ng[:, sl])
            outs.append(_bdot_nn(w[g], vn) + bcol[g])
        return u * jnp.concatenate(outs, axis=1)

    return fn


def _sg_fwd(dm, name, proj, ng, w, bcol):
    f = _sg_fn(dm.sg_groups)
    o = dm.in_offs
    return _rows(name, lambda i, *a: (f(*a),), dm.n_tok, LANE,
                 [R(proj, dm.sg_w, o[8] // dm.sg_w), R(proj, dm.sg_w, o[9] // dm.sg_w), C(ng), C(w), C(bcol)],
                 [(dm.sg_w, BF16)])[0]


def _sg_bwd(dm, name, proj, ng, w, bcol, dmix):
    f = _sg_fn(dm.sg_groups)
    o = dm.in_offs

    def fn(i, su, sv, ng, w, bcol, dy):
        _, vjp = jax.vjp(f, su, sv, ng, w, bcol)
        return vjp(dy)

    return _rows(name, fn, dm.n_tok, LANE,
                 [R(proj, dm.sg_w, o[8] // dm.sg_w), R(proj, dm.sg_w, o[9] // dm.sg_w), C(ng), C(w), C(bcol),
                  R(dmix, dm.sg_w, (dm.q_w + dm.hg_w) // dm.sg_w)],
                 [(dm.sg_w, BF16), (dm.sg_w, BF16)],
                 [(1, dm.sg_w), (dm.sg_groups, LANE, LANE), (dm.sg_groups, LANE, 1)])


def _conv3(x, prev, nxt, w0, w1, w2, zero_prev, zero_next):
    tm = x.shape[0]
    x = x.astype(F32)
    row = lax.broadcasted_iota(jnp.int32, (tm, 1), 0)
    up = jnp.where(zero_prev, 0.0, prev[prev.shape[0] - 1:].astype(F32))
    dn = jnp.where(zero_next, 0.0, nxt[0:1].astype(F32))
    x_m1 = jnp.where(row == 0, up, pltpu.roll(x, 1, 0))
    x_p1 = jnp.where(row == tm - 1, dn, pltpu.roll(x, tm - 1, 0))
    return w0 * x_m1 + w1 * x + w2 * x_p1, x_m1, x_p1


def _conv_edges(dm, tm):
    nbc, nb = dm.ctx_len // tm, dm.n_tok // tm

    def edges(i):
        return (jnp.logical_or(i == 0, i == nbc), jnp.logical_or(i == nbc - 1, i == nb - 1))

    return edges


def _halo_rows(dtype):
    return 16 if dtype == BF16 else 8


def _halo_specs(tm, tn, n_rows, hr, pick):
    last, per = n_rows // hr - 1, tm // hr
    return [pl.BlockSpec((tm, tn), lambda *ids: pick(*ids)),
            pl.BlockSpec((hr, tn), lambda *ids: (jnp.maximum(pick(*ids)[0] * per - 1, 0), pick(*ids)[1])),
            pl.BlockSpec((hr, tn), lambda *ids: (jnp.minimum((pick(*ids)[0] + 1) * per, last), pick(*ids)[1]))]


def _conv_fwd(dm, name, up, cw, cb):
    n, ff, tm = dm.n_tok, dm.d_ff, dm.row_tile
    tn = _div_tile(ff, 512, LANE)
    nj = ff // tn
    edges = _conv_edges(dm, tm)

    def body(g_ref, gp_ref, gn_ref, v_ref, vp_ref, vn_ref, wg_ref, wv_ref, bg_ref, bv_ref, a_ref):
        zp, zn = edges(pl.program_id(0))
        wg, wv = wg_ref[...], wv_ref[...]
        yg = _conv3(g_ref[...], gp_ref[...], gn_ref[...], wg[0:1], wg[1:2], wg[2:3], zp, zn)[0] + bg_ref[...]
        yv = _conv3(v_ref[...], vp_ref[...], vn_ref[...], wv[0:1], wv[1:2], wv[2:3], zp, zn)[0] + bv_ref[...]
        a_ref[...] = (yg * jax.nn.sigmoid(yg) * yv).astype(a_ref.dtype)

    small = lambda off: pl.BlockSpec((3, tn), lambda i, j: (0, j + off))
    bias = lambda off: pl.BlockSpec((1, tn), lambda i, j: (0, j + off))
    return pl.pallas_call(
        body, name=name, grid=(n // tm, nj),
        in_specs=_halo_specs(tm, tn, n, _halo_rows(up.dtype), lambda i, j: (i, j))
        + _halo_specs(tm, tn, n, _halo_rows(up.dtype), lambda i, j: (i, j + nj))
        + [small(0), small(nj), bias(0), bias(nj)],
        out_specs=pl.BlockSpec((tm, tn), lambda i, j: (i, j)), out_shape=S((n, ff), BF16),
        compiler_params=_cparams(("parallel", "parallel")))(up, up, up, up, up, up, cw, cw, cb, cb)


def _conv_bwd_dy(dm, name, up, cw, cb, da):
    n, ff, tm = dm.n_tok, dm.d_ff, dm.row_tile
    tn = _div_tile(ff, 512, LANE)
    nj = ff // tn
    edges = _conv_edges(dm, tm)

    def body(g_ref, gp_ref, gn_ref, v_ref, vp_ref, vn_ref, wg_ref, wv_ref, bg_ref, bv_ref, da_ref,
             dyg_ref, dyv_ref, dwg_ref, dwv_ref, dbg_ref, dbv_ref):
        i = pl.program_id(1)
        zp, zn = edges(i)
        wg, wv = wg_ref[...], wv_ref[...]
        g, v = g_ref[...].astype(F32), v_ref[...].astype(F32)
        cg, g_m1, g_p1 = _conv3(g, gp_ref[...], gn_ref[...], wg[0:1], wg[1:2], wg[2:3], zp, zn)
        cv, v_m1, v_p1 = _conv3(v, vp_ref[...], vn_ref[...], wv[0:1], wv[1:2], wv[2:3], zp, zn)
        yg, yv = cg + bg_ref[...], cv + bv_ref[...]
        sg = jax.nn.sigmoid(yg)
        da = da_ref[...]
        dyg = da * yv * (sg * (1.0 + yg * (1.0 - sg)))
        dyv = da * (yg * sg)
        dyg_ref[...] = dyg.astype(dyg_ref.dtype)
        dyv_ref[...] = dyv.astype(dyv_ref.dtype)
        row = lax.broadcasted_iota(jnp.int32, (3, tn), 0)

        def stack3(dy, a, b, c):
            return jnp.where(row == 0, _colsum(dy * a), jnp.where(row == 1, _colsum(dy * b), _colsum(dy * c)))

        upd = [(dwg_ref, stack3(dyg, g_m1, g, g_p1)), (dwv_ref, stack3(dyv, v_m1, v, v_p1)),
               (dbg_ref, _colsum(dyg)), (dbv_ref, _colsum(dyv))]

        @pl.when(i == 0)
        def _():
            for r, val in upd:
                r[...] = val

        @pl.when(i != 0)
        def _():
            for r, val in upd:
                r[...] += val

    hs = lambda off: _halo_specs(tm, tn, n, _halo_rows(up.dtype), lambda j, i: (i, j + off))
    small = lambda off: pl.BlockSpec((3, tn), lambda j, i: (0, j + off))
    bias = lambda off: pl.BlockSpec((1, tn), lambda j, i: (0, j + off))
    blk = pl.BlockSpec((tm, tn), lambda j, i: (i, j))
    return pl.pallas_call(
        body, name=name, grid=(nj, n // tm),
        in_specs=hs(0) + hs(nj) + [small(0), small(nj), bias(0), bias(nj), blk],
        out_specs=[blk, blk, small(0), small(0), bias(0), bias(0)],
        out_shape=[S((n, ff), BF16), S((n, ff), BF16), S((3, ff), F32), S((3, ff), F32), S((1, ff), F32),
                   S((1, ff), F32)],
        compiler_params=_cparams(("parallel", "arbitrary")))(up, up, up, up, up, up, cw, cw, cb, cb, da)


def _conv_bwd_dx(dm, name, dyg, dyv, cw):
    n, ff, tm = dm.n_tok, dm.d_ff, dm.row_tile
    tn = _div_tile(ff, 512, LANE)
    nj = ff // tn
    edges = _conv_edges(dm, tm)

    def body(g_ref, gp_ref, gn_ref, v_ref, vp_ref, vn_ref, wg_ref, wv_ref, o_ref):
        zp, zn = edges(pl.program_id(0))
        half = pl.program_id(1) // nj
        x = jnp.where(half == 0, g_ref[...], v_ref[...])
        xp = jnp.where(half == 0, gp_ref[...], vp_ref[...])
        xn = jnp.where(half == 0, gn_ref[...], vn_ref[...])
        w = jnp.where(half == 0, wg_ref[...], wv_ref[...])
        o_ref[...] = _conv3(x, xp, xn, w[2:3], w[1:2], w[0:1], zp, zn)[0].astype(o_ref.dtype)

    hr = _halo_rows(dyg.dtype)
    g_specs = _halo_specs(tm, tn, n, hr, lambda i, j: (i, jnp.minimum(j, nj - 1)))
    v_specs = _halo_specs(tm, tn, n, hr, lambda i, j: (i, jnp.maximum(j - nj, 0)))
    return pl.pallas_call(
        body, name=name, grid=(n // tm, 2 * nj),
        in_specs=g_specs + v_specs + [pl.BlockSpec((3, tn), lambda i, j: (0, jnp.minimum(j, nj - 1))),
                                      pl.BlockSpec((3, tn), lambda i, j: (0, nj + jnp.maximum(j - nj, 0)))],
        out_specs=pl.BlockSpec((tm, tn), lambda i, j: (i, j)), out_shape=S((n, 2 * ff), BF16),
        compiler_params=_cparams(("parallel", "parallel")))(dyg, dyg, dyg, dyv, dyv, dyv, cw, cw)


def _loss_head(dm, name, x, tgt, g):
    tm = dm.row_tile
    nc = dm.ctx_len // tm

    def fn(i, x, t, g):
        def f(x, g):
            err = _rms(x, g) - t
            return 0.5 * jnp.sum(jnp.mean(err * err, axis=-1, keepdims=True), axis=0, keepdims=True)

        loss, vjp = jax.vjp(f, x, g)
        dx, dg = vjp(jnp.ones((1, 1), F32))
        live = i >= nc
        return (jnp.where(live, dx, 0.0), jnp.where(live, jnp.broadcast_to(loss, (1, LANE)), 0.0),
                jnp.where(live, dg, 0.0))

    return _rows(name, fn, dm.n_tok, tm, [R(x), R(tgt, rmap=lambda i: jnp.maximum(i - nc, 0)), C(g)],
                 [(dm.d_model, F32)], [(1, LANE), (1, dm.d_model)])


def _dproj_assemble(dm, name, d_aq, d_ak, dv, dq_dir, df_dir, dv_dir, d_hgt, d_su, d_sv):
    def fn(i, d_aq, d_ak, dv, q0, q1, f0, f1, v0, v1, d_hgt, d_su, d_sv):
        parts = [d_aq, d_ak, dv, q0 + q1, f0, f1, v0 + v1, d_hgt, d_su, d_sv]
        return (jnp.concatenate([p.astype(F32) for p in parts], axis=1),)

    ins = [R(d_aq), R(d_ak), R(dv), R(dq_dir[0]), R(dq_dir[1]), R(df_dir[0]), R(df_dir[1]), R(dv_dir[0]),
           R(dv_dir[1]), R(d_hgt), R(d_su), R(d_sv)]
    return _rows(name, fn, dm.n_tok, dm.row_tile, ins, [(dm.in_cols, BF16)])[0]


def _layer_fwd(dm, l, x, h, mods, wl, tabs):
    ct, st = tabs
    proj = _matmul("proj", h, wl["w_in"], "nn", F32, b_shards=4)
    qh, kh = _qk_fwd(dm, "qk", proj, ct, st, wl["q_g"], wl["k_g"])
    attn = _attn_fwd(dm, "attn", qh, kh, proj)
    o_dir, hist = _hgrn_fwd(dm, "hgrn", proj, wl["lb"])
    hg = _hgc_fwd(dm, "hgc", o_dir, proj, wl["hg_g"])
    sg = _sg_fwd(dm, "sg", proj, wl["sg_g"], wl["sg_w"], wl["sg_bcol"])
    mix = jnp.concatenate([attn, hg, sg], axis=1)
    m = _matmul("out", mix, wl["w_out"], "nn", F32)
    x1, h2 = _resnorm_fwd(dm, "resnorm2", x, m, mods[2], wl["norm2_g"], mods[3], mods[4])
    up = _matmul("up", h2, wl["w_up"], "nn", BF16, b_shards=4)
    a = _conv_fwd(dm, "conv", up, wl["conv_w"], wl["conv_b"])
    f = _matmul("down", a, wl["w_down"], "nn", F32)
    saved = dict(x=x, h=h, proj=proj, qh=qh, kh=kh, attn=attn, o_dir=o_dir, hist=hist, mix=mix, m=m, x1=x1, h2=h2,
                 up=up, a=a, f=f)
    return x1, f, saved


def _layer_bwd(dm, l, dx2, sv, mods, wl, tabs):
    ct, st = tabs
    g = {}
    df, g["mod5"] = _gate_bwd(dm, "b_gate5", dx2, sv["f"], mods[5])
    da = _matmul("b_da", df, wl["w_down"], "nt", F32)
    g["w_down"] = _matmul("b_wdown", sv["a"], df, "tn", BF16)
    dyg, dyv, dwg, dwv, dbg, dbv = _conv_bwd_dy(dm, "b_convdy", sv["up"], wl["conv_w"], wl["conv_b"], da)
    g["conv_w"] = jnp.concatenate([dwg, dwv], axis=1)
    g["conv_b"] = jnp.concatenate([dbg, dbv], axis=1)
    d_up = _conv_bwd_dx(dm, "b_convdx", dyg, dyv, wl["conv_w"])
    dh2 = _matmul("b_dh2", d_up, wl["w_up"], "nt", F32, b_shards=4)
    g["w_up"] = _matmul("b_wup", sv["h2"], d_up, "tn", BF16, out_shards=4)
    dx1, g["norm2_g"], g["mod3"], g["mod4"] = _normmod_bwd(dm, "b_norm2", sv["x1"], dh2, dx2, wl["norm2_g"],
                                                             mods[3], mods[4])
    dmv, g["mod2"] = _gate_bwd(dm, "b_gate2", dx1, sv["m"], mods[2])
    dmix = _matmul("b_dmix", dmv, wl["w_out"], "nt", F32)
    g["w_out"] = _matmul("b_wout", sv["mix"], dmv, "tn", BF16)
    proj = sv["proj"]
    dqh, dkh, dv = _attn_bwd(dm, "b_attn", sv["qh"], sv["kh"], proj, sv["attn"], dmix)
    d_aq, d_ak, g["q_g"], g["k_g"] = _qk_bwd(dm, "b_qk", proj, ct, st, wl["q_g"], wl["k_g"], dqh, dkh)
    do, d_hgt, g["hg_g"] = _hgc_bwd(dm, "b_hgc", sv["o_dir"], proj, wl["hg_g"], dmix)
    dq_dir, df_dir, dv_dir, g["lb"] = _hgrn_bwd(dm, "b_hgrn", proj, wl["lb"], sv["hist"], do)
    d_su, d_sv, g["sg_g"], g["sg_w"], g["sg_bcol"] = _sg_bwd(dm, "b_sg", proj, wl["sg_g"], wl["sg_w"],
                                                            wl["sg_bcol"], dmix)
    dproj = _dproj_assemble(dm, "b_dproj", d_aq, d_ak, dv, dq_dir, df_dir, dv_dir, d_hgt, d_su, d_sv)
    dh = _matmul("b_dh", dproj, wl["w_in"], "nt", F32, b_shards=4)
    g["w_in"] = _matmul("b_win", sv["h"], dproj, "tn", BF16, out_shards=4)
    dx, g["norm1_g"], g["mod0"], g["mod1"] = _normmod_bwd(dm, "b_norm1", sv["x"], dh, dx1, wl["norm1_g"],
                                                           mods[0], mods[1])
    return dx, g


def _sample_step(dm, x_all, tgt, mods, wls, final_g, tabs):
    saved = []
    x = x_all
    h = _normmod_fwd(dm, "norm1", x, wls[0]["norm1_g"], mods[0][0], mods[0][1])
    for l in range(dm.depth):
        x1, f, sv = _layer_fwd(dm, l, x, h, mods[l], wls[l], tabs)
        saved.append(sv)
        if l + 1 < dm.depth:
            x, h = _resnorm_fwd(dm, "resnorm1", x1, f, mods[l][5], wls[l + 1]["norm1_g"], mods[l + 1][0],
                                mods[l + 1][1])
        else:
            x = _res_fwd(dm, "res", x1, f, mods[l][5])
    dx, loss, dfg = _loss_head(dm, "loss_head", x, tgt, final_g)
    grads = [None] * dm.depth
    for l in reversed(range(dm.depth)):
        dx, grads[l] = _layer_bwd(dm, l, dx, saved[l], mods[l], wls[l], tabs)
    return loss, dx, grads, dfg


_ANY = pl.BlockSpec(memory_space=pl.ANY)


def _place():
    x, y, c = lax.axis_index("x"), lax.axis_index("y"), lax.axis_index("c")
    return x, y, c


def _all_gather8(name, blk):
    r, cdim = blk.shape

    def body(x_ref, out_ref, send_sems, recv_sems, local_sem):
        x, y, c = _place()
        me, sibling = (x, y, c), (x, y, 1 - c)
        chips = [(1 - x, y), (x, 1 - y), (1 - x, 1 - y)]

        def slot(px, py, pc):
            return out_ref.at[4 * px + 2 * py + pc]

        def copy(k, block, to, src=None):
            return pltpu.make_async_remote_copy(
                src_ref=slot(*block) if src is None else src, dst_ref=slot(*block),
                send_sem=send_sems.at[k], recv_sem=recv_sems.at[k], device_id=to, device_id_type=MESH)

        mine = pltpu.make_async_copy(x_ref, slot(*me), local_sem)
        mine.start()
        first = [copy(0, me, sibling, src=x_ref)]
        first += [copy(1 + j, me, (*chip, c), src=x_ref) for j, chip in enumerate(chips)]
        for cp in first:
            cp.start()
        passed = [copy(4 + j, (*chip, c), sibling) for j, chip in enumerate(chips)]
        for j, chip in enumerate(chips):
            copy(1 + j, (*chip, c), me).wait_recv()
            passed[j].start()
        copy(0, sibling, me).wait_recv()
        for j, chip in enumerate(chips):
            copy(4 + j, (*chip, 1 - c), me).wait_recv()
        for cp in first + passed:
            cp.wait_send()
        mine.wait()

    return pl.pallas_call(
        body, name=name, out_shape=S((8, r, cdim), blk.dtype), in_specs=[_ANY], out_specs=_ANY,
        scratch_shapes=[pltpu.SemaphoreType.DMA((7,)), pltpu.SemaphoreType.DMA((7,)), pltpu.SemaphoreType.DMA])(blk)


def _pair_swap_halves(name, g):
    n_s, _, r, cdim = g.shape

    def body(g_ref, recv_ref, send_sems, recv_sems):
        x, y, c = _place()
        remote = [pltpu.make_async_remote_copy(src_ref=g_ref.at[s, 1 - c], dst_ref=recv_ref.at[s],
                                               send_sem=send_sems.at[s], recv_sem=recv_sems.at[s],
                                               device_id=(x, y, 1 - c), device_id_type=MESH) for s in range(n_s)]
        for cp in remote:
            cp.start()
        for cp in remote:
            cp.wait()

    return pl.pallas_call(
        body, name=name, out_shape=S((n_s, r, cdim), g.dtype), in_specs=[_ANY], out_specs=_ANY,
        scratch_shapes=[pltpu.SemaphoreType.DMA((n_s,))] * 2)(g)


def _pair_sum(name, g, recv):
    n_s, _, r, cdim = g.shape
    tm = _ew_tile(r, cdim, 4)

    def body(g0_ref, g1_ref, r_ref, o_ref):
        own = jnp.where(lax.axis_index("c") == 0, g0_ref[...].astype(F32), g1_ref[...].astype(F32))
        o_ref[...] = (own + r_ref[...].astype(F32)).astype(o_ref.dtype)

    return pl.pallas_call(
        body, name=name, grid=(n_s, r // tm),
        in_specs=[pl.BlockSpec((None, None, tm, cdim), lambda s, i: (s, 0, i, 0)),
                  pl.BlockSpec((None, None, tm, cdim), lambda s, i: (s, 1, i, 0)),
                  pl.BlockSpec((None, tm, cdim), lambda s, i: (s, i, 0))],
        out_specs=pl.BlockSpec((None, tm, cdim), lambda s, i: (s, i, 0)), out_shape=S((n_s, r, cdim), BF16),
        compiler_params=_cparams(("parallel", "parallel")))(g, g, recv)


def _chip_exchange(name, p):
    _, r, cdim = p.shape

    def body(p_ref, recv_ref, send_sems, recv_sems):
        x, y, c = _place()
        peers = [(1 - x, y), (x, 1 - y), (1 - x, 1 - y)]
        remote = [pltpu.make_async_remote_copy(src_ref=p_ref.at[2 * px + py], dst_ref=recv_ref.at[k],
                                               send_sem=send_sems.at[k], recv_sem=recv_sems.at[k],
                                               device_id=(px, py, c), device_id_type=MESH)
                  for k, (px, py) in enumerate(peers)]
        for cp in remote:
            cp.start()
        for cp in remote:
            cp.wait()

    return pl.pallas_call(
        body, name=name, out_shape=S((3, r, cdim), p.dtype), in_specs=[_ANY], out_specs=_ANY,
        scratch_shapes=[pltpu.SemaphoreType.DMA((3,)), pltpu.SemaphoreType.DMA((3,))])(p)


def _chip_sum(name, p, recv):
    n_s, r, cdim = p.shape
    tm = _ew_tile(r, cdim, 9)

    def body(*refs):
        chip = 2 * lax.axis_index("x") + lax.axis_index("y")
        own = refs[n_s - 1][...].astype(F32)
        for s in range(n_s - 2, -1, -1):
            own = jnp.where(chip == s, refs[s][...].astype(F32), own)
        r0, r1, r2, o_ref = refs[n_s:]
        tot = ((own + r0[...].astype(F32)) + r1[...].astype(F32)) + r2[...].astype(F32)
        o_ref[0] = tot
        o_ref[1] = tot

    blk = lambda s: pl.BlockSpec((None, tm, cdim), functools.partial(lambda i, s: (s, i, 0), s=s))
    return pl.pallas_call(
        body, name=name, grid=(r // tm,), in_specs=[blk(s) for s in range(n_s)] + [blk(k) for k in range(3)],
        out_specs=pl.BlockSpec((2, tm, cdim), lambda i: (0, i, 0)), out_shape=S((2, r, cdim), F32),
        compiler_params=_cparams(("parallel",)))(*([p] * n_s), *([recv] * 3))


def _pair_share(name, t2):
    def body(t_ref, out_ref, send_sem, recv_sem):
        x, y, c = _place()
        remote = pltpu.make_async_remote_copy(src_ref=out_ref.at[c], dst_ref=out_ref.at[c], send_sem=send_sem,
                                              recv_sem=recv_sem, device_id=(x, y, 1 - c), device_id_type=MESH)
        remote.start()
        remote.wait()

    return pl.pallas_call(
        body, name=name, out_shape=S(t2.shape, t2.dtype), in_specs=[_ANY], out_specs=_ANY,
        input_output_aliases={0: 0},
        scratch_shapes=[pltpu.SemaphoreType.DMA, pltpu.SemaphoreType.DMA])(t2)


def _ew_tile(rows, cols, n_arrays):
    cap = min(1024, max(16, (24 * 1024 * 1024) // (n_arrays * 2 * cols * 4)))
    if rows <= 16:
        return rows
    mult = 16 if any(rows % t == 0 for t in range(16, cap + 1, 16)) else 8
    return _div_tile(rows, cap, mult)


def _reduce_scatter_grad(name, gb):
    _, _, r, cdim = gb.shape
    p = _pair_sum(name + "_sum2", gb, _pair_swap_halves(name + "_swap", gb))
    tot2 = _chip_sum(name + "_sum4", p, _chip_exchange(name + "_xchg", p))
    return _pair_share(name + "_share", tot2).reshape(2 * r, cdim)


def _ada_fwd(name, a16, w_ada, b_cols):
    depth, d, cols = w_ada.shape
    tn = _div_tile(cols, 1536, LANE)

    def body(a_ref, w_ref, b_ref, o_ref):
        a = a_ref[...]
        o_ref[...] = _dg(a * jax.nn.sigmoid(a), w_ref[...], _NN) + b_ref[...]

    return pl.pallas_call(
        body, name=name, grid=(depth, cols // tn),
        in_specs=[pl.BlockSpec((16, d), lambda l, j: (0, 0)), pl.BlockSpec((None, d, tn), lambda l, j: (l, 0, j)),
                  pl.BlockSpec((None, 1, tn), lambda l, j: (l, 0, j))],
        out_specs=pl.BlockSpec((None, 16, tn), lambda l, j: (l, 0, j)), out_shape=S((depth, 16, cols), F32),
        compiler_params=_cparams(("parallel", "parallel")))(a16, w_ada, b_cols)


def _ada_bwd_w(name, a_t, dmod):
    depth, _, cols = dmod.shape
    d = a_t.shape[0]
    tm, tn = _div_tile(d, 512, 8), _div_tile(cols, 1536, LANE)

    def body(a_ref, g_ref, o_ref):
        a = a_ref[...]
        o_ref[...] = _dg(a * jax.nn.sigmoid(a), g_ref[...], _NN)

    return pl.pallas_call(
        body, name=name, grid=(depth, d // tm, cols // tn),
        in_specs=[pl.BlockSpec((tm, 16), lambda l, i, j: (i, 0)), pl.BlockSpec((None, 16, tn), lambda l, i, j: (l, 0, j))],
        out_specs=pl.BlockSpec((None, tm, tn), lambda l, i, j: (l, i, j)), out_shape=S((depth, d, cols), F32),
        compiler_params=_cparams(("parallel", "parallel", "parallel")))(a_t, dmod)


def _ada_bwd_a(name, dmod, w_ada):
    depth, d, cols = w_ada.shape
    tn = _div_tile(cols, 1536, LANE)
    nj = cols // tn

    def body(g_ref, w_ref, o_ref):
        first = jnp.logical_and(pl.program_id(0) == 0, pl.program_id(1) == 0)
        p = _dg(g_ref[...], w_ref[...], _NT)

        @pl.when(first)
        def _():
            o_ref[...] = p

        @pl.when(jnp.logical_not(first))
        def _():
            o_ref[...] += p

    return pl.pallas_call(
        body, name=name, grid=(depth, nj),
        in_specs=[pl.BlockSpec((None, 16, tn), lambda l, j: (l, 0, j)), pl.BlockSpec((None, d, tn), lambda l, j: (l, 0, j))],
        out_specs=pl.BlockSpec((16, d), lambda l, j: (0, 0)), out_shape=S((16, d), F32),
        compiler_params=_cparams(("arbitrary", "arbitrary")))(dmod, w_ada)


def _lbs_fn(p):
    depth = p.shape[0]
    rows = [p[l] for l in range(depth)]
    mx = functools.reduce(jnp.maximum, rows)
    ex = [jnp.exp(r - mx) for r in rows]
    den = functools.reduce(lambda a, b: a + b, ex)
    sm = [e / den for e in ex]
    out, run = [], None
    for l in range(depth):
        run = sm[l] if run is None else run + sm[l]
        out.append(run - sm[0])
    return jnp.stack(out, axis=0)


def _lbs_fwd(name, p):
    def body(p_ref, o_ref):
        o_ref[...] = _lbs_fn(p_ref[...])

    return pl.pallas_call(body, name=name, out_shape=S(p.shape, F32))(p)


def _lbs_bwd(name, p, d_out):
    def body(p_ref, g_ref, o_ref):
        _, vjp = jax.vjp(_lbs_fn, p_ref[...])
        o_ref[...] = vjp(g_ref[...])[0]

    return pl.pallas_call(body, name=name, out_shape=S(p.shape, F32))(p, d_out)


def _sum8(name, g):
    _, r, cdim = g.shape
    tm = _ew_tile(r, cdim, 9)

    def body(g_ref, o_ref):
        acc = g_ref[0]
        for k in range(1, 8):
            acc = acc + g_ref[k]
        o_ref[...] = acc

    return pl.pallas_call(body, name=name, grid=(r // tm,),
                          in_specs=[pl.BlockSpec((8, tm, cdim), lambda i: (0, i, 0))],
                          out_specs=pl.BlockSpec((tm, cdim), lambda i: (i, 0)), out_shape=S((r, cdim), F32),
                          compiler_params=_cparams(("parallel",)))(g)


def _adamw(name, w, m, v, g):
    rows, cols = w.shape
    tm = _ew_tile(rows, cols, 7)

    def fn(i, w, m, v, g):
        m = ADAM_B1 * m + (1.0 - ADAM_B1) * g
        v = ADAM_B2 * v + (1.0 - ADAM_B2) * jnp.square(g)
        m_hat = m / (1.0 - ADAM_B1 ** ADAM_STEP)
        v_hat = v / (1.0 - ADAM_B2 ** ADAM_STEP)
        return -ADAM_LR * (m_hat / (jnp.sqrt(v_hat) + ADAM_EPS) + ADAM_WD * w), m, v

    return _rows(name, fn, rows, tm, [R(w), R(m), R(v), R(g)], [(cols, F32)] * 3)


def _silu_grad_mul(name, g, z):
    def body(g_ref, z_ref, o_ref):
        zz = z_ref[...]
        sg = jax.nn.sigmoid(zz)
        o_ref[...] = g_ref[...] * (sg * (1.0 + zz * (1.0 - sg)))

    return pl.pallas_call(body, name=name, out_shape=S(g.shape, F32))(g, z)


def _pack(arrs):
    parts, meta, off = [], [], 0
    for a in arrs:
        n = int(np.prod(a.shape))
        rows = -(-n // (8 * LANE)) * 8
        flat = a.reshape(-1).astype(F32)
        parts.append(jnp.pad(flat, (0, rows * LANE - n)).reshape(rows, LANE))
        meta.append((off, rows, n, a.shape))
        off += rows
    return jnp.concatenate(parts, axis=0), meta


def _unpack(buf, meta, lead=()):
    out = []
    for off, rows, n, shape in meta:
        seg = buf[..., off:off + rows, :].reshape(*lead, rows * LANE)[..., :n]
        out.append(seg.reshape(*lead, *shape))
    return out


_SMALL = ("c_ctx", "b_ada", "norm1_g", "q_norm_g", "k_norm_g", "hg_lower_bounds", "hg_norm_g", "sg_norm_g", "sg_w",
          "sg_b", "norm2_g", "conv_w", "conv_b", "final_norm_g")
_BIG = ("w_ada", "w_in", "w_out", "w_up", "w_down")
_WEIGHTS = ("c_ctx", "w_ada", "b_ada", "norm1_g", "w_in", "q_norm_g", "k_norm_g", "hg_lower_bounds", "hg_norm_g",
            "sg_norm_g", "sg_w", "sg_b", "w_out", "norm2_g", "w_up", "conv_w", "conv_b", "w_down", "final_norm_g")


def _dims_of(x, ctx, w_in, w_down):
    return Dims(d_model=x.shape[-1], seq=x.shape[1], ctx_len=ctx.shape[1], depth=w_in.shape[0],
                d_ff=w_down.shape[1] * 4)


def _step(dm, x, c, ctx, tgt, w, m, v):
    d, depth = dm.d_model, dm.depth
    xi, yi, ci = _place()
    chip = 2 * xi + yi
    me = 4 * xi + 2 * yi + ci
    n_chips = 4
    take_chips = lambda g8: g8[0::2]

    small_in, meta_in = _pack([c, w["conv_w"], w["hg_lower_bounds"]])
    gath = _all_gather8("ag_small_in", small_in)
    c_all, conv_sh, lb_sh = _unpack(gath, meta_in, lead=(8,))
    c_all = c_all.reshape(8, d)
    conv_w = take_chips(conv_sh).transpose(1, 2, 0, 3).reshape(depth, 3, 2 * dm.d_ff)
    lb_logits = take_chips(lb_sh).transpose(1, 2, 0, 3).reshape(2, depth, dm.hg_w)
    lb_p = lb_logits.transpose(1, 0, 2)
    lbs = _lbs_fwd("lbs_fwd", lb_p)

    a16 = jnp.concatenate([c_all, w["c_ctx"][None], jnp.zeros((7, d), F32)], axis=0)
    cols = w["w_ada"].shape[-1]
    b_cols = lax.dynamic_slice_in_dim(w["b_ada"], chip * cols, cols, axis=1)[:, None, :]
    mod_sh = _ada_fwd("ada_fwd", a16, w["w_ada"], b_cols)
    mod_g = take_chips(_all_gather8("ag_mod", mod_sh.reshape(depth * 16, cols)))
    mod_all = mod_g.reshape(n_chips, depth, 16, cols).transpose(1, 2, 0, 3).reshape(depth, 16, n_chips * cols)
    mod_lat = lax.dynamic_index_in_dim(mod_all, me, axis=1, keepdims=False)
    mod_ctx = mod_all[:, 8]
    mods = [[jnp.stack([mod_ctx[l, k * d:(k + 1) * d], mod_lat[l, k * d:(k + 1) * d]]) for k in range(N_MOD)]
            for l in range(depth)]

    def gather_rows(name, shard):
        half = shard.shape[0] // 2
        mine = lax.dynamic_slice_in_dim(shard.astype(BF16), ci * half, half, axis=0)
        return _all_gather8(name, mine)

    wls = []
    for l in range(depth):
        w_in = gather_rows("ag_w_in", w["w_in"][l])
        w_up = gather_rows("ag_w_up", w["w_up"][l])
        w_out = gather_rows("ag_w_out", w["w_out"][l])
        w_down = gather_rows("ag_w_down", w["w_down"][l])
        wls.append(dict(
            w_in=w_in.reshape(n_chips, d, -1), w_up=w_up.reshape(n_chips, d, -1),
            w_out=w_out.reshape(dm.d_mix, d), w_down=w_down.reshape(dm.d_ff, d),
            conv_w=conv_w[l], conv_b=w["conv_b"][l][None], norm1_g=w["norm1_g"][l][None],
            norm2_g=w["norm2_g"][l][None], q_g=w["q_norm_g"][l][None], k_g=w["k_norm_g"][l][None],
            hg_g=w["hg_norm_g"][l][None], sg_g=w["sg_norm_g"][l][None], sg_w=w["sg_w"][l],
            sg_bcol=w["sg_b"][l][:, :, None], lb=lbs[l].reshape(2, 1, dm.hg_w)))

    x_all = jnp.concatenate([ctx[0], x[0]], axis=0)
    loss_row, dx_all, grads, dfg = _sample_step(dm, x_all, tgt[0], mods, wls, w["final_norm_g"][None],
                                                _rope_tables(dm))
    loss = lax.psum(loss_row[0, 0], ("x", "y", "c"))
    grad_x = dx_all[dm.ctx_len:][None]

    g_big = {}
    for name in ("w_in", "w_up", "w_out", "w_down"):
        per_layer = []
        for l in range(depth):
            gb = grads[l][name]
            rows = w[name].shape[1] // 2
            per_layer.append(_reduce_scatter_grad(f"rs_{name}",gb.reshape(n_chips, 2, rows, gb.shape[-1])))
        g_big[name] = jnp.stack(per_layer)

    dmod_lat = jnp.stack([jnp.concatenate([grads[l][f"mod{k}"][1] for k in range(N_MOD)]) for l in range(depth)])
    dmod_ctx = jnp.stack([jnp.concatenate([grads[l][f"mod{k}"][0] for k in range(N_MOD)]) for l in range(depth)])
    d_lbs = jnp.stack([grads[l]["lb"].reshape(2, dm.hg_w) for l in range(depth)])
    d_lb_p = _lbs_bwd("lbs_bwd", lb_p, d_lbs).transpose(1, 0, 2)
    stk = lambda key: jnp.stack([grads[l][key] for l in range(depth)])
    part = {
        "b_ada": dmod_lat + dmod_ctx, "norm1_g": stk("norm1_g")[:, 0], "q_norm_g": stk("q_g")[:, 0],
        "k_norm_g": stk("k_g")[:, 0], "hg_lower_bounds": d_lb_p, "hg_norm_g": stk("hg_g")[:, 0],
        "sg_norm_g": stk("sg_g")[:, 0], "sg_w": stk("sg_w"), "sg_b": stk("sg_bcol")[..., 0],
        "norm2_g": stk("norm2_g")[:, 0], "conv_w": stk("conv_w"), "conv_b": stk("conv_b")[:, 0],
        "final_norm_g": dfg[0]}
    names = [n for n in _SMALL if n != "c_ctx"]
    packed, meta = _pack([part[n] for n in names] + [dmod_ctx, dmod_lat])
    gath = _all_gather8("ag_small_grads", packed)
    summed = _unpack(_sum8("sum_small_grads", gath), meta)
    g_small = dict(zip(names, summed[:len(names)]))
    dmod_ctx_tot = summed[len(names)]
    dmod_lat_all = _unpack(gath, meta[-1:], lead=(8,))[0]

    dmod16 = jnp.concatenate([dmod_lat_all.transpose(1, 0, 2), dmod_ctx_tot[:, None], jnp.zeros((depth, 7, 6 * d), F32)],
                             axis=1)
    dmod16 = lax.dynamic_slice_in_dim(dmod16, chip * cols, cols, axis=2)
    g_big["w_ada"] = _ada_bwd_w("ada_bwd_w", a16.T, dmod16)
    da16 = _ada_bwd_a("ada_bwd_a", dmod16, w["w_ada"])
    da_g = take_chips(_all_gather8("ag_dctx", da16))
    da_sum = _rows("sum_dctx", lambda i, a, b, c2, d2: (((a + b) + c2) + d2,), 16, 16,
                   [R(da_g[k]) for k in range(n_chips)], [(d, F32)])[0]
    g_small["c_ctx"] = _silu_grad_mul("dctx_silu", da_sum[8:9], w["c_ctx"][None])[0]

    g_small["conv_w"] = lax.dynamic_slice_in_dim(g_small["conv_w"], chip * w["conv_w"].shape[-1], w["conv_w"].shape[-1], axis=2)
    g_small["hg_lower_bounds"] = lax.dynamic_slice_in_dim(g_small["hg_lower_bounds"], chip * w["hg_lower_bounds"].shape[-1],
                                                          w["hg_lower_bounds"].shape[-1], axis=2)

    grads_out, deltas, new_m, new_v = {}, {}, {}, {}
    for name in _BIG:
        shp = w[name].shape
        flat = lambda a: a.reshape(-1, shp[-1])
        dl, nm, nv = _adamw(f"adamw_{name}", flat(w[name]), flat(m[name]), flat(v[name]), flat(g_big[name]))
        grads_out[name] = g_big[name].reshape(shp)
        deltas[name], new_m[name], new_v[name] = dl.reshape(shp), nm.reshape(shp), nv.reshape(shp)
    pw, meta_s = _pack([w[n] for n in _SMALL])
    pm, _ = _pack([m[n] for n in _SMALL])
    pv, _ = _pack([v[n] for n in _SMALL])
    pg, _ = _pack([g_small[n].reshape(w[n].shape) for n in _SMALL])
    dl, nm, nv = _adamw("adamw_small", pw, pm, pv, pg)
    for name, a, b, c2 in zip(_SMALL, _unpack(dl, meta_s), _unpack(nm, meta_s), _unpack(nv, meta_s)):
        grads_out[name] = g_small[name].reshape(w[name].shape)
        deltas[name], new_m[name], new_v[name] = a, b, c2
    return loss, grad_x, grads_out, deltas, new_m, new_v


def kernel(x, c, ctx, c_ctx, w_ada, b_ada, norm1_g, w_in, q_norm_g, k_norm_g, hg_lower_bounds, hg_norm_g, sg_norm_g, sg_w, sg_b, w_out, norm2_g, w_up, conv_w, conv_b, w_down, final_norm_g, loss_target, m_c_ctx, m_w_ada, m_b_ada, m_norm1_g, m_w_in, m_q_norm_g, m_k_norm_g, m_hg_lower_bounds, m_hg_norm_g, m_sg_norm_g, m_sg_w, m_sg_b, m_w_out, m_norm2_g, m_w_up, m_conv_w, m_conv_b, m_w_down, m_final_norm_g, v_c_ctx, v_w_ada, v_b_ada, v_norm1_g, v_w_in, v_q_norm_g, v_k_norm_g, v_hg_lower_bounds, v_hg_norm_g, v_sg_norm_g, v_sg_w, v_sg_b, v_w_out, v_norm2_g, v_w_up, v_conv_w, v_conv_b, v_w_down, v_final_norm_g):
    w = dict(c_ctx=c_ctx, w_ada=w_ada, b_ada=b_ada, norm1_g=norm1_g, w_in=w_in, q_norm_g=q_norm_g, k_norm_g=k_norm_g, hg_lower_bounds=hg_lower_bounds, hg_norm_g=hg_norm_g, sg_norm_g=sg_norm_g, sg_w=sg_w, sg_b=sg_b, w_out=w_out, norm2_g=norm2_g, w_up=w_up, conv_w=conv_w, conv_b=conv_b, w_down=w_down, final_norm_g=final_norm_g)
    m = dict(c_ctx=m_c_ctx, w_ada=m_w_ada, b_ada=m_b_ada, norm1_g=m_norm1_g, w_in=m_w_in, q_norm_g=m_q_norm_g, k_norm_g=m_k_norm_g, hg_lower_bounds=m_hg_lower_bounds, hg_norm_g=m_hg_norm_g, sg_norm_g=m_sg_norm_g, sg_w=m_sg_w, sg_b=m_sg_b, w_out=m_w_out, norm2_g=m_norm2_g, w_up=m_w_up, conv_w=m_conv_w, conv_b=m_conv_b, w_down=m_w_down, final_norm_g=m_final_norm_g)
    v = dict(c_ctx=v_c_ctx, w_ada=v_w_ada, b_ada=v_b_ada, norm1_g=v_norm1_g, w_in=v_w_in, q_norm_g=v_q_norm_g, k_norm_g=v_k_norm_g, hg_lower_bounds=v_hg_lower_bounds, hg_norm_g=v_hg_norm_g, sg_norm_g=v_sg_norm_g, sg_w=v_sg_w, sg_b=v_sg_b, w_out=v_w_out, norm2_g=v_norm2_g, w_up=v_w_up, conv_w=v_conv_w, conv_b=v_conv_b, w_down=v_w_down, final_norm_g=v_final_norm_g)
    dm = _dims_of(x, ctx, w_in, w_down)
    loss, grad_x, g, dl, nm, nv = _step(dm, x, c, ctx, loss_target, w, m, v)
    return (loss, grad_x, *[g[n] for n in _WEIGHTS], *[dl[n] for n in _WEIGHTS], *[nm[n] for n in _WEIGHTS],
            *[nv[n] for n in _WEIGHTS])
```

```python
import functools
import math
from typing import NamedTuple

import numpy as np
import jax
import jax.numpy as jnp
from jax import lax
from jax.experimental import pallas as pl
from jax.experimental.pallas import tpu as pltpu

F32, BF16 = jnp.float32, jnp.bfloat16
S = jax.ShapeDtypeStruct
MESH = pl.DeviceIdType.MESH

LANE = 128
EPS = 1e-6
F_MIN = 1e-30
ROPE_THETA = 10000.0
N_MOD = 6
ADAM_LR, ADAM_B1, ADAM_B2, ADAM_EPS, ADAM_WD, ADAM_STEP = 0.001, 0.9, 0.999, 1e-08, 0.01, 10
VMEM_LIMIT = 56 * 1024 * 1024


class Dims(NamedTuple):
    d_model: int = 2048
    seq: int = 4096
    ctx_len: int = 256
    grid_w: int = 64
    depth: int = 4
    attn_heads: int = 8
    kv_heads: int = 2
    hg_heads: int = 4
    hg_chunk: int = 16
    sg_groups: int = 4
    d_ff: int = 5632

    @property
    def n_tok(self):
        return self.seq + self.ctx_len

    @property
    def q_w(self):
        return self.attn_heads * LANE

    @property
    def kv_w(self):
        return self.kv_heads * LANE

    @property
    def hg_w(self):
        return self.hg_heads * LANE

    @property
    def sg_w(self):
        return self.sg_groups * LANE

    @property
    def d_mix(self):
        return self.q_w + self.hg_w + self.sg_w

    @property
    def in_sizes(self):
        return (self.q_w, self.kv_w, self.kv_w) + (self.hg_w,) * 5 + (self.sg_w,) * 2

    @property
    def in_cols(self):
        return sum(self.in_sizes)

    @property
    def in_offs(self):
        return tuple(int(v) for v in np.cumsum((0,) + self.in_sizes)[:-1])

    @property
    def row_tile(self):
        return min(256, self.ctx_len)


def _cparams(sem, vmem=VMEM_LIMIT):
    return pltpu.CompilerParams(dimension_semantics=sem, vmem_limit_bytes=vmem)


_ANY = pl.BlockSpec(memory_space=pl.ANY)


class _Side(NamedTuple):
    tag: str
    ins: tuple
    outs: tuple
    alias: tuple
    n_remote: int
    n_local: int
    make: object


def _merge_sides(*sides):
    sides = [s for s in sides if s is not None]
    if len(sides) <= 1:
        return sides[0] if sides else None
    offs, o_in, o_out, o_r, o_l = [], 0, 0, 0, 0
    for s in sides:
        offs.append((o_in, o_out, o_r, o_l))
        o_in, o_out, o_r, o_l = o_in + len(s.ins), o_out + len(s.outs), o_r + s.n_remote, o_l + s.n_local

    def make(sins, souts, sem, lsem):
        remote, local = [], []
        for s, (a, b, r, l) in zip(sides, offs):
            rr, ll = s.make(sins[a:a + len(s.ins)], souts[b:b + len(s.outs)],
                            functools.partial(lambda k, r: sem(r + k), r=r), functools.partial(lambda k, l: lsem(l + k), l=l))
            remote, local = remote + rr, local + ll
        return remote, local

    return _Side("_".join(s.tag for s in sides), sum((s.ins for s in sides), ()), sum((s.outs for s in sides), ()),
                 tuple((a + i, b + o) for s, (a, b, _, _) in zip(sides, offs) for i, o in s.alias), o_r, o_l, make)


def _split_outs(sides, outs):
    res, k = [], 0
    for s in sides:
        if s is not None:
            res.append(tuple(outs[k:k + len(s.outs)]))
            k += len(s.outs)
        else:
            res.append(())
    return res


def _side_scratch(side):
    dma = pltpu.SemaphoreType.DMA
    return [dma((max(1, side.n_remote),)), dma((max(1, side.n_remote),)), dma((max(1, side.n_local),))]


def _pcall(body, *, name, grid, in_specs, out_specs, out_shape, args, sem, scratch_shapes=(), side=None):
    single = not isinstance(out_shape, (list, tuple))
    out_shape_l = [out_shape] if single else list(out_shape)
    out_specs_l = [out_specs] if single else list(out_specs)
    if side is None:
        res = pl.pallas_call(body, name=name, grid=grid, in_specs=list(in_specs), out_specs=out_specs_l,
                             out_shape=out_shape_l, scratch_shapes=list(scratch_shapes),
                             compiler_params=_cparams(sem))(*args)
        return (res[0] if single else res), ()
    n_in, n_out, n_scr = len(in_specs), len(out_shape_l), len(scratch_shapes)
    s_in, s_out = len(side.ins), len(side.outs)
    g = tuple(grid)

    def wrapped(*refs):
        ins, sins = refs[:n_in], refs[n_in:n_in + s_in]
        o0 = n_in + s_in
        outs, souts = refs[o0:o0 + n_out], refs[o0 + n_out:o0 + n_out + s_out]
        scr = refs[o0 + n_out + s_out:o0 + n_out + s_out + n_scr]
        send_sems, recv_sems, local_sems = refs[-3:]
        ids = [pl.program_id(a) for a in range(len(g))]
        first = functools.reduce(jnp.logical_and, [i == 0 for i in ids])
        last = functools.reduce(jnp.logical_and, [i == n - 1 for i, n in zip(ids, g)])
        remote, local = side.make(sins, souts, lambda k: (send_sems.at[k], recv_sems.at[k]), lambda k: local_sems.at[k])

        @pl.when(first)
        def _():
            for cp in local:
                cp.start()
            for snd, _ in remote:
                snd.start()

        body(*ins, *outs, *scr)

        @pl.when(last)
        def _():
            for snd, arr in remote:
                snd.wait_send()
                arr.wait_recv()
            for cp in local:
                cp.wait()

    res = pl.pallas_call(
        wrapped, name=f"{name}_{side.tag}", grid=g, in_specs=list(in_specs) + [_ANY] * s_in,
        out_specs=out_specs_l + [_ANY] * s_out, out_shape=out_shape_l + list(side.outs),
        scratch_shapes=list(scratch_shapes) + _side_scratch(side),
        input_output_aliases={n_in + i: n_out + o for i, o in side.alias},
        compiler_params=_cparams(("arbitrary",) * len(g)))(*args, *side.ins)
    outs = res[:n_out]
    return (outs[0] if single else outs), tuple(res[n_out:])


def _run_side(name, side):
    s_in = len(side.ins)

    def body(*refs):
        sins, souts = refs[:s_in], refs[s_in:s_in + len(side.outs)]
        send_sems, recv_sems, local_sems = refs[-3:]
        remote, local = side.make(sins, souts, lambda k: (send_sems.at[k], recv_sems.at[k]), lambda k: local_sems.at[k])
        for cp in local:
            cp.start()
        for snd, _ in remote:
            snd.start()
        for snd, arr in remote:
            snd.wait_send()
            arr.wait_recv()
        for cp in local:
            cp.wait()

    res = pl.pallas_call(body, name=f"{name}_{side.tag}", in_specs=[_ANY] * s_in, out_specs=[_ANY] * len(side.outs),
                         out_shape=list(side.outs), scratch_shapes=_side_scratch(side),
                         input_output_aliases=dict(side.alias))(*side.ins)
    return tuple(res)


def _place():
    x, y, c = lax.axis_index("x"), lax.axis_index("y"), lax.axis_index("c")
    return x, y, c


def _rcopy(src, dst, sems, to):
    return pltpu.make_async_remote_copy(src_ref=src, dst_ref=dst, send_sem=sems[0], recv_sem=sems[1], device_id=to,
                                        device_id_type=pl.DeviceIdType.MESH)


def _gather_own_side(blks):
    def make(sins, souts, sem, lsem):
        x, y, c = _place()
        me = 4 * x + 2 * y + c
        peers = [(x, y, 1 - c), (1 - x, y, c), (x, 1 - y, c), (1 - x, 1 - y, c)]
        remote, local = [], []
        for w, (src, out) in enumerate(zip(sins, souts)):
            local.append(pltpu.make_async_copy(src, out.at[me], lsem(w)))
            for j, (px, py, pc) in enumerate(peers):
                remote.append((_rcopy(src, out.at[me], sem(4 * w + j), (px, py, pc)),
                               _rcopy(src, out.at[4 * px + 2 * py + pc], sem(4 * w + j), (px, py, pc))))
        return remote, local

    return _Side("gown", tuple(blks), tuple(S((8,) + b.shape, b.dtype) for b in blks), (), 4 * len(blks), len(blks),
                 make)


def _gather_pass_side(bufs):
    def make(sins, souts, sem, lsem):
        x, y, c = _place()
        chips = [(1 - x, y), (x, 1 - y), (1 - x, 1 - y)]
        remote = []
        for w, out in enumerate(souts):
            for j, (px, py) in enumerate(chips):
                mine, theirs = out.at[4 * px + 2 * py + c], out.at[4 * px + 2 * py + 1 - c]
                remote.append((_rcopy(mine, mine, sem(3 * w + j), (x, y, 1 - c)),
                               _rcopy(mine, theirs, sem(3 * w + j), (x, y, 1 - c))))
        return remote, []

    return _Side("gpass", tuple(bufs), tuple(S(b.shape, b.dtype) for b in bufs), tuple((i, i) for i in range(len(bufs))),
                 3 * len(bufs), 0, make)


def _swap_side(gbs):
    def make(sins, souts, sem, lsem):
        x, y, c = _place()
        remote = []
        for w, (g, recv) in enumerate(zip(sins, souts)):
            for s in range(g.shape[0]):
                cp = _rcopy(g.at[s, 1 - c], recv.at[s], sem(4 * w + s), (x, y, 1 - c))
                remote.append((cp, cp))
        return remote, []

    return _Side("swap", tuple(gbs), tuple(S((g.shape[0],) + g.shape[2:], g.dtype) for g in gbs), (),
                 4 * len(gbs), 0, make)


def _xchg_side(ps):
    def make(sins, souts, sem, lsem):
        x, y, c = _place()
        peers = [(1 - x, y), (x, 1 - y), (1 - x, 1 - y)]
        remote = []
        for w, (p, recv) in enumerate(zip(sins, souts)):
            for k, (px, py) in enumerate(peers):
                cp = _rcopy(p.at[2 * px + py], recv.at[k], sem(3 * w + k), (px, py, c))
                remote.append((cp, cp))
        return remote, []

    return _Side("xchg", tuple(ps), tuple(S((3,) + p.shape[1:], p.dtype) for p in ps), (), 3 * len(ps), 0, make)


def _share_side(t2s):
    def make(sins, souts, sem, lsem):
        x, y, c = _place()
        remote = []
        for w, out in enumerate(souts):
            cp = _rcopy(out.at[c], out.at[c], sem(w), (x, y, 1 - c))
            remote.append((cp, _rcopy(out.at[c], out.at[1 - c], sem(w), (x, y, 1 - c))))
        return remote, []

    return _Side("share", tuple(t2s), tuple(S(t.shape, t.dtype) for t in t2s), tuple((i, i) for i in range(len(t2s))),
                 len(t2s), 0, make)


_NN, _NT, _TN = ((1,), (0,)), ((1,), (1,)), ((0,), (0,))


def _dg(a, b, dims):
    return lax.dot_general(a.astype(BF16), b.astype(BF16), (dims, ((), ())), preferred_element_type=F32)


@jax.custom_vjp
def _bdot_nn(a, b):
    return _dg(a, b, _NN)


@jax.custom_vjp
def _bdot_nt(a, b):
    return _dg(a, b, _NT)


@jax.custom_vjp
def _bdot_tn(a, b):
    return _dg(a, b, _TN)


_bdot_nn.defvjp(lambda a, b: (_dg(a, b, _NN), (a, b)),
                lambda r, g: (_bdot_nt(g, r[1]).astype(r[0].dtype), _bdot_tn(r[0], g).astype(r[1].dtype)))
_bdot_nt.defvjp(lambda a, b: (_dg(a, b, _NT), (a, b)),
                lambda r, g: (_bdot_nn(g, r[1]).astype(r[0].dtype), _bdot_tn(g, r[0]).astype(r[1].dtype)))
_bdot_tn.defvjp(lambda a, b: (_dg(a, b, _TN), (a, b)),
                lambda r, g: (_bdot_nt(r[1], g).astype(r[0].dtype), _bdot_nn(r[0], g).astype(r[1].dtype)))


def _f32dot(a, b):
    return lax.dot_general(a, b, (_NN, ((), ())), precision=lax.Precision.HIGHEST, preferred_element_type=F32)


@jax.custom_vjp
def _tri_dot(tri, tri_t, x):
    return _f32dot(tri, x)


_tri_dot.defvjp(lambda tri, tri_t, x: (_f32dot(tri, x), (tri, tri_t)),
                lambda r, g: (jnp.zeros_like(r[0]), jnp.zeros_like(r[1]), _f32dot(r[1], g)))


@jax.custom_vjp
def _pair_swap(x):
    lane = lax.broadcasted_iota(jnp.int32, x.shape, x.ndim - 1)
    return jnp.where(lane % 2 == 0, pltpu.roll(x, LANE - 1, x.ndim - 1), pltpu.roll(x, 1, x.ndim - 1))


_pair_swap.defvjp(lambda x: (_pair_swap(x), None), lambda _, g: (_pair_swap(g),))


def R(a, w=None, cb=0, rmap=None):
    return ("r", a, a.shape[1] if w is None else w, cb, rmap)


def C(a):
    return ("c", a)


def _rows(name, fn, n_rows, tm, ins, outs, accs=()):
    n_in, n_out = len(ins), len(outs)
    in_specs, args = [], []
    for e in ins:
        if e[0] == "r":
            _, a, w, cb, rmap = e
            assert w % LANE == 0 or w == a.shape[1]
            if rmap is None:
                in_specs.append(pl.BlockSpec((tm, w), functools.partial(lambda i, cb: (i, cb), cb=cb)))
            else:
                in_specs.append(pl.BlockSpec((tm, w), functools.partial(lambda i, cb, rm: (rm(i), cb), cb=cb, rm=rmap)))
        else:
            a = e[1]
            in_specs.append(pl.BlockSpec(a.shape, functools.partial(lambda i, nd: (0,) * nd, nd=a.ndim)))
        args.append(a)
    out_shape = [S((n_rows, w), dt) for w, dt in outs] + [S(tuple(sh), F32) for sh in accs]
    out_specs = [pl.BlockSpec((tm, w), lambda i: (i, 0)) for w, _ in outs]
    out_specs += [pl.BlockSpec(tuple(sh), functools.partial(lambda i, nd: (0,) * nd, nd=len(sh))) for sh in accs]

    def body(*refs):
        i = pl.program_id(0)
        vals = fn(i, *[r[...] for r in refs[:n_in]])
        assert len(vals) == n_out + len(accs), (name, len(vals))
        for r, v in zip(refs[n_in:n_in + n_out], vals[:n_out]):
            r[...] = v.astype(r.dtype)
        for r, v in zip(refs[n_in + n_out:], vals[n_out:]):
            def init(r=r, v=v):
                r[...] = v.astype(F32)

            def add(r=r, v=v):
                r[...] += v.astype(F32)

            pl.when(i == 0)(init)
            pl.when(i != 0)(add)

    res = pl.pallas_call(body, name=name, grid=(n_rows // tm,), in_specs=in_specs, out_specs=out_specs,
                         out_shape=out_shape, compiler_params=_cparams(("arbitrary",)))(*args)
    return res


def _div_tile(n, cap, mult):
    if n <= cap:
        return n
    best = None
    for t in range(mult, cap + 1, mult):
        if n % t == 0:
            best = t
    assert best is not None, (n, cap, mult)
    return best


def _matmul(name, a, b, form, out_dtype, b_shards=1, out_shards=1, caps=(1088, 1408, 1408), side=None):
    if form == "tn":
        K, M = a.shape
    else:
        M, K = a.shape
    if form == "nn":
        N = b.shape[-1] * b_shards
    elif form == "nt":
        N = b.shape[-2]
    else:
        N = b.shape[1]
    n_per = N // (b_shards if form == "nn" else out_shards)
    k_per = K // (b_shards if form == "nt" else 1)
    tm = _div_tile(M, caps[0], 16 if form != "tn" else LANE)
    tn = _div_tile(n_per, caps[1], LANE)
    tk = _div_tile(k_per, caps[2], LANE if form != "tn" else 16)
    nk = K // tk
    grid = (M // tm, N // tn, nk)
    nps, kps = n_per // tn, k_per // tk

    if form == "tn":
        a_spec = pl.BlockSpec((tk, tm), lambda i, j, k: (k, i))
    else:
        a_spec = pl.BlockSpec((tm, tk), lambda i, j, k: (i, k))
    if form == "nn":
        if b_shards > 1:
            b_spec = pl.BlockSpec((None, tk, tn), lambda i, j, k: (j // nps, k, j % nps))
        else:
            b_spec = pl.BlockSpec((tk, tn), lambda i, j, k: (k, j))
    elif form == "nt":
        if b_shards > 1:
            b_spec = pl.BlockSpec((None, tn, tk), lambda i, j, k: (k // kps, j, k % kps))
        else:
            b_spec = pl.BlockSpec((tn, tk), lambda i, j, k: (j, k))
    else:
        b_spec = pl.BlockSpec((tk, tn), lambda i, j, k: (k, j))
    if out_shards > 1:
        o_spec = pl.BlockSpec((None, tm, tn), lambda i, j, k: (j // nps, i, j % nps))
        o_shape = S((out_shards, M, n_per), out_dtype)
    else:
        o_spec = pl.BlockSpec((tm, tn), lambda i, j, k: (i, j))
        o_shape = S((M, N), out_dtype)
    dims = {"nn": _NN, "nt": _NT, "tn": _TN}[form]

    def body(a_ref, b_ref, o_ref, acc_ref):
        k = pl.program_id(2)
        p = _dg(a_ref[...], b_ref[...], dims)
        if nk == 1:
            o_ref[...] = p.astype(o_ref.dtype)
        else:
            @pl.when(k == 0)
            def _():
                acc_ref[...] = p

            @pl.when(jnp.logical_and(k > 0, k < nk - 1))
            def _():
                acc_ref[...] += p

            @pl.when(k == nk - 1)
            def _():
                o_ref[...] = (acc_ref[...] + p).astype(o_ref.dtype)

    out, landed = _pcall(body, name=name, grid=grid, in_specs=[a_spec, b_spec], out_specs=o_spec, out_shape=o_shape,
                         scratch_shapes=[pltpu.VMEM((tm, tn), F32)], sem=("parallel", "parallel", "arbitrary"),
                         args=(a, b), side=side)
    return out if side is None else (out, landed)


def _rms(x, g):
    return x * lax.rsqrt(jnp.mean(x * x, axis=-1, keepdims=True) + EPS) * g


def _sel2(mm, is_ctx):
    return jnp.where(is_ctx, mm[0:1], mm[1:2])


def _put2(v, is_ctx):
    row = lax.broadcasted_iota(jnp.int32, (2, v.shape[-1]), 0)
    return jnp.where(row == jnp.where(is_ctx, 0, 1), v, 0.0)


def _normmod(x, g, sh, sc):
    return _rms(x, g) * (1.0 + sc) + sh


def _colsum(v):
    return jnp.sum(v, axis=0, keepdims=True)


def _normmod_fwd(dm, name, x, g, sh2, sc2):
    nc = dm.ctx_len // dm.row_tile

    def fn(i, x, g, sh2, sc2):
        is_ctx = i < nc
        return (_normmod(x, g, _sel2(sh2, is_ctx), _sel2(sc2, is_ctx)),)

    return _rows(name, fn, dm.n_tok, dm.row_tile, [R(x), C(g), C(sh2), C(sc2)], [(dm.d_model, BF16)])[0]


def _resnorm_fwd(dm, name, x, y, gate2, g, sh2, sc2):
    nc = dm.ctx_len // dm.row_tile

    def fn(i, x, y, gate2, g, sh2, sc2):
        is_ctx = i < nc
        x1 = x + _sel2(gate2, is_ctx) * y
        return x1, _normmod(x1, g, _sel2(sh2, is_ctx), _sel2(sc2, is_ctx))

    return _rows(name, fn, dm.n_tok, dm.row_tile, [R(x), R(y), C(gate2), C(g), C(sh2), C(sc2)],
                 [(dm.d_model, F32), (dm.d_model, BF16)])


def _res_fwd(dm, name, x, y, gate2):
    nc = dm.ctx_len // dm.row_tile

    def fn(i, x, y, gate2):
        return (x + _sel2(gate2, i < nc) * y,)

    return _rows(name, fn, dm.n_tok, dm.row_tile, [R(x), R(y), C(gate2)], [(dm.d_model, F32)])[0]


def _gate_bwd(dm, name, dx, y, gate2):
    nc = dm.ctx_len // dm.row_tile

    def fn(i, dx, y, gate2):
        is_ctx = i < nc
        return dx * _sel2(gate2, is_ctx), _put2(_colsum(dx * y), is_ctx)

    return _rows(name, fn, dm.n_tok, dm.row_tile, [R(dx), R(y), C(gate2)], [(dm.d_model, BF16)], [(2, dm.d_model)])


def _normmod_bwd(dm, name, x, dh, dres, g, sh2, sc2):
    nc = dm.ctx_len // dm.row_tile

    def fn(i, x, dh, dres, g, sh2, sc2):
        is_ctx = i < nc
        sh, sc = _sel2(sh2, is_ctx), _sel2(sc2, is_ctx)
        _, vjp = jax.vjp(_normmod, x, g, sh, sc)
        dx, dg, dsh, dsc = vjp(dh)
        return dres + dx, dg, _put2(dsh, is_ctx), _put2(dsc, is_ctx)

    return _rows(name, fn, dm.n_tok, dm.row_tile, [R(x), R(dh), R(dres), C(g), C(sh2), C(sc2)],
                 [(dm.d_model, F32)], [(1, dm.d_model), (2, dm.d_model), (2, dm.d_model)])


def _rope_tables(dm):
    t = jnp.arange(dm.seq)
    row = (t // dm.grid_w).astype(F32)
    col = (t % dm.grid_w).astype(F32)
    n_freq = LANE // 4
    inv = ROPE_THETA ** (-jnp.arange(n_freq, dtype=F32) / n_freq)
    ang = jnp.concatenate([row[:, None] * inv, col[:, None] * inv], axis=-1)
    cos, sin = jnp.cos(ang), jnp.sin(ang)
    ct = jnp.repeat(cos, 2, axis=-1)
    st = jnp.stack([-sin, sin], axis=-1).reshape(dm.seq, LANE)
    ct = jnp.concatenate([jnp.ones((dm.ctx_len, LANE), F32), ct], axis=0)
    st = jnp.concatenate([jnp.zeros((dm.ctx_len, LANE), F32), st], axis=0)
    return ct, st


def _qk_fn(n_q, n_k):
    def fn(aq, ak, ct, st, qg, kg):
        def head(x, g):
            y = _rms(x, g)
            return y * ct + _pair_swap(y) * st

        q = jnp.concatenate([head(aq[:, h * LANE:(h + 1) * LANE], qg) for h in range(n_q)], axis=1)
        k = jnp.concatenate([head(ak[:, h * LANE:(h + 1) * LANE], kg) for h in range(n_k)], axis=1)
        return q, k

    return fn


def _qk_fwd(dm, name, proj, ct, st, qg, kg):
    f = _qk_fn(dm.attn_heads, dm.kv_heads)
    o = dm.in_offs
    return _rows(name, lambda i, *a: f(*a), dm.n_tok, dm.row_tile,
                 [R(proj, dm.q_w, o[0] // dm.q_w), R(proj, dm.kv_w, o[1] // dm.kv_w), R(ct), R(st), C(qg), C(kg)],
                 [(dm.q_w, BF16), (dm.kv_w, BF16)])


def _qk_bwd(dm, name, proj, ct, st, qg, kg, dq, dk):
    f = _qk_fn(dm.attn_heads, dm.kv_heads)
    o = dm.in_offs

    def fn(i, aq, ak, ct, st, qg, kg, dq, dk):
        _, vjp = jax.vjp(lambda aq, ak, qg, kg: f(aq, ak, ct, st, qg, kg), aq, ak, qg, kg)
        return vjp((dq, dk))

    return _rows(name, fn, dm.n_tok, dm.row_tile,
                 [R(proj, dm.q_w, o[0] // dm.q_w), R(proj, dm.kv_w, o[1] // dm.kv_w), R(ct), R(st), C(qg), C(kg),
                  R(dq), R(dk)],
                 [(dm.q_w, BF16), (dm.kv_w, BF16)], [(1, LANE), (1, LANE)])


def _attn_probs(q, k, i, nc, ctx_len, n_tok):
    s = _dg(q, k, _NT) * (LANE ** -0.5)
    col = lax.broadcasted_iota(jnp.int32, (1, n_tok), 1)
    s = s + jnp.where(col < jnp.where(i < nc, ctx_len, n_tok), 0.0, -1e30)
    e = jnp.exp(s - jnp.max(s, axis=-1, keepdims=True))
    return e, 1.0 / jnp.sum(e, axis=-1, keepdims=True)


def _attn_fwd(dm, name, qh, kh, proj, side=None):
    tq, n = dm.row_tile, dm.n_tok
    nc, grp = dm.ctx_len // tq, dm.attn_heads // dm.kv_heads
    v_cb = dm.in_offs[2] // LANE

    def body(q_ref, k_ref, v_ref, o_ref):
        e, inv = _attn_probs(q_ref[...], k_ref[...], pl.program_id(1), nc, dm.ctx_len, n)
        o_ref[...] = (_dg(e, v_ref[...], _NN) * inv).astype(o_ref.dtype)

    out, landed = _pcall(
        body, name=name, grid=(dm.attn_heads, n // tq),
        in_specs=[pl.BlockSpec((tq, LANE), lambda h, i: (i, h)),
                  pl.BlockSpec((n, LANE), lambda h, i: (0, h // grp)),
                  pl.BlockSpec((n, LANE), lambda h, i: (0, v_cb + h // grp))],
        out_specs=pl.BlockSpec((tq, LANE), lambda h, i: (i, h)),
        out_shape=S((n, dm.q_w), BF16), sem=("parallel", "parallel"), args=(qh, kh, proj), side=side)
    return out if side is None else (out, landed)


def _attn_bwd(dm, name, qh, kh, proj, attn, dmix, side=None):
    tq, n = dm.row_tile, dm.n_tok
    nc, grp = dm.ctx_len // tq, dm.attn_heads // dm.kv_heads
    v_cb = dm.in_offs[2] // LANE

    def body(q_ref, k_ref, v_ref, o_ref, do_ref, dq_ref, dk_ref, dv_ref):
        first = jnp.logical_and(pl.program_id(1) == 0, pl.program_id(2) == 0)
        q, k, v, do = q_ref[...], k_ref[...], v_ref[...], do_ref[...]
        e, inv = _attn_probs(q, k, pl.program_id(2), nc, dm.ctx_len, n)
        delta = jnp.sum(do * o_ref[...].astype(F32), axis=-1, keepdims=True)
        ds = e * ((_dg(do, v, _NT) - delta) * (inv * (LANE ** -0.5)))
        dq_ref[...] = _dg(ds, k, _NN)
        dk = _dg(ds, q, _TN)
        dv = _dg(e, do * inv, _TN)

        @pl.when(first)
        def _():
            dk_ref[...] = dk
            dv_ref[...] = dv

        @pl.when(jnp.logical_not(first))
        def _():
            dk_ref[...] += dk
            dv_ref[...] += dv

    outs, landed = _pcall(
        body, name=name, grid=(dm.kv_heads, grp, n // tq),
        in_specs=[pl.BlockSpec((tq, LANE), lambda g, hh, i: (i, g * grp + hh)),
                  pl.BlockSpec((n, LANE), lambda g, hh, i: (0, g)),
                  pl.BlockSpec((n, LANE), lambda g, hh, i: (0, v_cb + g)),
                  pl.BlockSpec((tq, LANE), lambda g, hh, i: (i, g * grp + hh)),
                  pl.BlockSpec((tq, LANE), lambda g, hh, i: (i, g * grp + hh))],
        out_specs=[pl.BlockSpec((tq, LANE), lambda g, hh, i: (i, g * grp + hh)),
                   pl.BlockSpec((n, LANE), lambda g, hh, i: (0, g)),
                   pl.BlockSpec((n, LANE), lambda g, hh, i: (0, g))],
        out_shape=[S((n, dm.q_w), F32), S((n, dm.kv_w), F32), S((n, dm.kv_w), F32)],
        sem=("parallel", "arbitrary", "arbitrary"), args=(qh, kh, proj, attn, dmix), side=side)
    return outs if side is None else (outs, landed)


def _hg_chunk(d, st, qraw, fraw, v, lb):
    c = qraw.shape[0]
    sig = jax.nn.sigmoid(fraw)
    f = lb + (1.0 - lb) * sig
    logf = jnp.log(jnp.maximum(f, F_MIN))
    k = (1.0 - lb) * jax.nn.sigmoid(-fraw)
    q = qraw * jax.nn.sigmoid(qraw)
    r_i = lax.broadcasted_iota(jnp.int32, (c, c), 0)
    c_i = lax.broadcasted_iota(jnp.int32, (c, c), 1)
    sgn = 1 - 2 * d
    tri = ((r_i - c_i) * sgn >= 0).astype(F32)
    tri_t = ((c_i - r_i) * sgn >= 0).astype(F32)
    b = _tri_dot(tri, tri_t, logf)
    b_last = jnp.sum(logf, axis=0, keepdims=True)
    trow = lax.broadcasted_iota(jnp.int32, (c, 1), 0)
    o = _bdot_nt(q * jnp.exp(b), st)
    for s in range(c):
        m = (trow - s) * sgn >= 0
        e = jnp.exp(jnp.where(m, b - b[s:s + 1], 0.0))
        w = jnp.where(m, q * k[s:s + 1] * e, 0.0)
        o = o + jnp.sum(w, axis=-1, keepdims=True) * v[s:s + 1]
    st_new = st * jnp.exp(b_last) + _bdot_tn(v, k * jnp.exp(b_last - b))
    return o, st_new


def _hg_blk(dm, tb):
    nbc, nbl = dm.ctx_len // tb, dm.seq // tb

    def blk(d, j):
        rev = jnp.where(j < nbc, nbc - 1 - j, 2 * nbc + nbl - 1 - j)
        return jnp.where(d == 0, j, rev)

    return blk, nbc + nbl


def _hgrn_fwd(dm, name, proj, lb, side=None):
    tb, n, hh, ck, hw = dm.row_tile, dm.n_tok, dm.hg_heads, dm.hg_chunk, dm.hg_w
    blk, nblk = _hg_blk(dm, tb)
    ncb = tb // ck
    o = dm.in_offs
    q_cb, f_cb, v_cb = o[3] // hw, o[4] // hw, o[6] // hw

    def body(q_ref, f_ref, v_ref, lb_ref, o_ref, hist_ref, st_ref):
        d = pl.program_id(0)

        @pl.when(pl.program_id(1) == 0)
        def _():
            st_ref[...] = jnp.zeros_like(st_ref)

        lbv = lb_ref[...]

        def chunk(ci, carry):
            c = jnp.where(d == 0, ci, ncb - 1 - ci)
            rows = pl.ds(pl.multiple_of(c * ck, ck), ck)
            for h in range(hh):
                cols = slice(h * LANE, (h + 1) * LANE)
                st = st_ref[h]
                hist_ref[h, c] = st
                oc, stn = _hg_chunk(d, st, q_ref[rows, cols], f_ref[rows, cols], v_ref[rows, cols], lbv[:, cols])
                o_ref[rows, cols] = oc
                st_ref[h] = stn
            return carry

        lax.fori_loop(0, ncb, chunk, 0)

    outs, landed = _pcall(
        body, name=name, grid=(2, nblk),
        in_specs=[pl.BlockSpec((tb, hw), lambda d, j: (blk(d, j), q_cb)),
                  pl.BlockSpec((tb, hw), lambda d, j: (blk(d, j), f_cb + d)),
                  pl.BlockSpec((tb, hw), lambda d, j: (blk(d, j), v_cb)),
                  pl.BlockSpec((None, 1, hw), lambda d, j: (d, 0, 0))],
        out_specs=[pl.BlockSpec((None, tb, hw), lambda d, j: (d, blk(d, j), 0)),
                   pl.BlockSpec((None, hh, ncb, LANE, LANE), lambda d, j: (d, 0, blk(d, j), 0, 0))],
        out_shape=[S((2, n, hw), F32), S((2, hh, n // ck, LANE, LANE), F32)],
        scratch_shapes=[pltpu.VMEM((hh, LANE, LANE), F32)],
        sem=("parallel", "arbitrary"), args=(proj, proj, proj, lb), side=side)
    return outs if side is None else (outs, landed)


def _hgrn_bwd(dm, name, proj, lb, hist, do, side=None):
    tb, n, hh, ck, hw = dm.row_tile, dm.n_tok, dm.hg_heads, dm.hg_chunk, dm.hg_w
    blk, nblk = _hg_blk(dm, tb)
    ncb = tb // ck
    o = dm.in_offs
    q_cb, f_cb, v_cb = o[3] // hw, o[4] // hw, o[6] // hw

    def rblk(d, j):
        return blk(d, nblk - 1 - j)

    def body(q_ref, f_ref, v_ref, lb_ref, hist_ref, do_ref, dq_ref, df_ref, dv_ref, dlb_ref, dst_ref):
        d = pl.program_id(0)

        @pl.when(pl.program_id(1) == 0)
        def _():
            dst_ref[...] = jnp.zeros_like(dst_ref)
            dlb_ref[...] = jnp.zeros_like(dlb_ref)

        lbv = lb_ref[...]

        def chunk(ci, carry):
            cp = ncb - 1 - ci
            c = jnp.where(d == 0, cp, ncb - 1 - cp)
            rows = pl.ds(pl.multiple_of(c * ck, ck), ck)
            for h in range(hh):
                cols = slice(h * LANE, (h + 1) * LANE)
                _, vjp = jax.vjp(functools.partial(_hg_chunk, d), hist_ref[h, c], q_ref[rows, cols],
                                 f_ref[rows, cols], v_ref[rows, cols], lbv[:, cols])
                dst, dq, df, dv, dlb = vjp((do_ref[rows, cols], dst_ref[h]))
                dq_ref[rows, cols] = dq
                df_ref[rows, cols] = df
                dv_ref[rows, cols] = dv
                dlb_ref[:, cols] += dlb
                dst_ref[h] = dst
            return carry

        lax.fori_loop(0, ncb, chunk, 0)

    row3 = pl.BlockSpec((None, tb, hw), lambda d, j: (d, rblk(d, j), 0))
    outs, landed = _pcall(
        body, name=name, grid=(2, nblk),
        in_specs=[pl.BlockSpec((tb, hw), lambda d, j: (rblk(d, j), q_cb)),
                  pl.BlockSpec((tb, hw), lambda d, j: (rblk(d, j), f_cb + d)),
                  pl.BlockSpec((tb, hw), lambda d, j: (rblk(d, j), v_cb)),
                  pl.BlockSpec((None, 1, hw), lambda d, j: (d, 0, 0)),
                  pl.BlockSpec((None, hh, ncb, LANE, LANE), lambda d, j: (d, 0, rblk(d, j), 0, 0)),
                  pl.BlockSpec((tb, hw), lambda d, j: (rblk(d, j), 0))],
        out_specs=[row3, row3, row3, pl.BlockSpec((None, 1, hw), lambda d, j: (d, 0, 0))],
        out_shape=[S((2, n, hw), F32)] * 3 + [S((2, 1, hw), F32)],
        scratch_shapes=[pltpu.VMEM((hh, LANE, LANE), F32)],
        sem=("parallel", "arbitrary"), args=(proj, proj, proj, lb, hist, do), side=side)
    return outs if side is None else (outs, landed)


def _hgc_fn(n_h):
    def fn(o0, o1, gt, g):
        osum = o0 + o1
        y = jnp.concatenate([_rms(osum[:, h * LANE:(h + 1) * LANE], g) for h in range(n_h)], axis=1)
        return y * (gt * jax.nn.sigmoid(gt))

    return fn


def _hgc_fwd(dm, name, o_dir, proj, g):
    f = _hgc_fn(dm.hg_heads)
    cb = dm.in_offs[7] // dm.hg_w
    return _rows(name, lambda i, *a: (f(*a),), dm.n_tok, dm.row_tile,
                 [R(o_dir[0]), R(o_dir[1]), R(proj, dm.hg_w, cb), C(g)], [(dm.hg_w, BF16)])[0]


def _hgc_bwd(dm, name, o_dir, proj, g, dmix):
    f = _hgc_fn(dm.hg_heads)
    cb = dm.in_offs[7] // dm.hg_w

    def fn(i, o0, o1, gt, g, dy):
        _, vjp = jax.vjp(f, o0, o1, gt, g)
        do, _, dgt, dg = vjp(dy)
        return do, dgt, dg

    return _rows(name, fn, dm.n_tok, dm.row_tile,
                 [R(o_dir[0]), R(o_dir[1]), R(proj, dm.hg_w, cb), C(g), R(dmix, dm.hg_w, dm.q_w // dm.hg_w)],
                 [(dm.hg_w, F32), (dm.hg_w, BF16)], [(1, LANE)])


def _sg_fn(n_g):
    def fn(su, sv, ng, w, bcol):
        u = jax.nn.gelu(su)
        gv = jax.nn.gelu(sv)
        outs = []
        for g in range(n_g):
            sl = slice(g * LANE, (g + 1) * LANE)
            vn = _rms(gv[:, sl], ng[:, sl])
            outs.append(_bdot_nn(w[g], vn) + bcol[g])
        return u * jnp.concatenate(outs, axis=1)

    return fn


def _sg_fwd(dm, name, proj, ng, w, bcol):
    f = _sg_fn(dm.sg_groups)
    o = dm.in_offs
    return _rows(name, lambda i, *a: (f(*a),), dm.n_tok, LANE,
                 [R(proj, dm.sg_w, o[8] // dm.sg_w), R(proj, dm.sg_w, o[9] // dm.sg_w), C(ng), C(w), C(bcol)],
                 [(dm.sg_w, BF16)])[0]


def _sg_bwd(dm, name, proj, ng, w, bcol, dmix):
    f = _sg_fn(dm.sg_groups)
    o = dm.in_offs

    def fn(i, su, sv, ng, w, bcol, dy):
        _, vjp = jax.vjp(f, su, sv, ng, w, bcol)
        return vjp(dy)

    return _rows(name, fn, dm.n_tok, LANE,
                 [R(proj, dm.sg_w, o[8] // dm.sg_w), R(proj, dm.sg_w, o[9] // dm.sg_w), C(ng), C(w), C(bcol),
                  R(dmix, dm.sg_w, (dm.q_w + dm.hg_w) // dm.sg_w)],
                 [(dm.sg_w, BF16), (dm.sg_w, BF16)],
                 [(1, dm.sg_w), (dm.sg_groups, LANE, LANE), (dm.sg_groups, LANE, 1)])


def _conv3(x, prev, nxt, w0, w1, w2, zero_prev, zero_next):
    tm = x.shape[0]
    x = x.astype(F32)
    row = lax.broadcasted_iota(jnp.int32, (tm, 1), 0)
    up = jnp.where(zero_prev, 0.0, prev[prev.shape[0] - 1:].astype(F32))
    dn = jnp.where(zero_next, 0.0, nxt[0:1].astype(F32))
    x_m1 = jnp.where(row == 0, up, pltpu.roll(x, 1, 0))
    x_p1 = jnp.where(row == tm - 1, dn, pltpu.roll(x, tm - 1, 0))
    return w0 * x_m1 + w1 * x + w2 * x_p1, x_m1, x_p1


def _conv_edges(dm, tm):
    nbc, nb = dm.ctx_len // tm, dm.n_tok // tm

    def edges(i):
        return (jnp.logical_or(i == 0, i == nbc), jnp.logical_or(i == nbc - 1, i == nb - 1))

    return edges


def _halo_rows(dtype):
    return 16 if dtype == BF16 else 8


def _halo_specs(tm, tn, n_rows, hr, pick):
    last, per = n_rows // hr - 1, tm // hr
    return [pl.BlockSpec((tm, tn), lambda *ids: pick(*ids)),
            pl.BlockSpec((hr, tn), lambda *ids: (jnp.maximum(pick(*ids)[0] * per - 1, 0), pick(*ids)[1])),
            pl.BlockSpec((hr, tn), lambda *ids: (jnp.minimum((pick(*ids)[0] + 1) * per, last), pick(*ids)[1]))]


def _conv_fwd(dm, name, up, cw, cb):
    n, ff, tm = dm.n_tok, dm.d_ff, dm.row_tile
    tn = _div_tile(ff, 1408, LANE)
    nj = ff // tn
    edges = _conv_edges(dm, tm)

    def body(g_ref, gp_ref, gn_ref, v_ref, vp_ref, vn_ref, wg_ref, wv_ref, bg_ref, bv_ref, a_ref):
        zp, zn = edges(pl.program_id(0))
        wg, wv = wg_ref[...], wv_ref[...]
        yg = _conv3(g_ref[...], gp_ref[...], gn_ref[...], wg[0:1], wg[1:2], wg[2:3], zp, zn)[0] + bg_ref[...]
        yv = _conv3(v_ref[...], vp_ref[...], vn_ref[...], wv[0:1], wv[1:2], wv[2:3], zp, zn)[0] + bv_ref[...]
        a_ref[...] = (yg * jax.nn.sigmoid(yg) * yv).astype(a_ref.dtype)

    small = lambda off: pl.BlockSpec((3, tn), lambda i, j: (0, j + off))
    bias = lambda off: pl.BlockSpec((1, tn), lambda i, j: (0, j + off))
    return pl.pallas_call(
        body, name=name, grid=(n // tm, nj),
        in_specs=_halo_specs(tm, tn, n, _halo_rows(up.dtype), lambda i, j: (i, j))
        + _halo_specs(tm, tn, n, _halo_rows(up.dtype), lambda i, j: (i, j + nj))
        + [small(0), small(nj), bias(0), bias(nj)],
        out_specs=pl.BlockSpec((tm, tn), lambda i, j: (i, j)), out_shape=S((n, ff), BF16),
        compiler_params=_cparams(("parallel", "parallel")))(up, up, up, up, up, up, cw, cw, cb, cb)


def _conv_bwd_dy(dm, name, up, cw, cb, da, side=None):
    n, ff, tm = dm.n_tok, dm.d_ff, dm.row_tile
    tn = _div_tile(ff, 1408, LANE)
    nj = ff // tn
    edges = _conv_edges(dm, tm)

    def body(g_ref, gp_ref, gn_ref, v_ref, vp_ref, vn_ref, wg_ref, wv_ref, bg_ref, bv_ref, da_ref,
             dyg_ref, dyv_ref, dwg_ref, dwv_ref, dbg_ref, dbv_ref):
        i = pl.program_id(1)
        zp, zn = edges(i)
        wg, wv = wg_ref[...], wv_ref[...]
        g, v = g_ref[...].astype(F32), v_ref[...].astype(F32)
        cg, g_m1, g_p1 = _conv3(g, gp_ref[...], gn_ref[...], wg[0:1], wg[1:2], wg[2:3], zp, zn)
        cv, v_m1, v_p1 = _conv3(v, vp_ref[...], vn_ref[...], wv[0:1], wv[1:2], wv[2:3], zp, zn)
        yg, yv = cg + bg_ref[...], cv + bv_ref[...]
        sg = jax.nn.sigmoid(yg)
        da = da_ref[...]
        dyg = da * yv * (sg * (1.0 + yg * (1.0 - sg)))
        dyv = da * (yg * sg)
        dyg_ref[...] = dyg.astype(dyg_ref.dtype)
        dyv_ref[...] = dyv.astype(dyv_ref.dtype)
        row = lax.broadcasted_iota(jnp.int32, (3, tn), 0)

        def stack3(dy, a, b, c):
            return jnp.where(row == 0, _colsum(dy * a), jnp.where(row == 1, _colsum(dy * b), _colsum(dy * c)))

        upd = [(dwg_ref, stack3(dyg, g_m1, g, g_p1)), (dwv_ref, stack3(dyv, v_m1, v, v_p1)),
               (dbg_ref, _colsum(dyg)), (dbv_ref, _colsum(dyv))]

        @pl.when(i == 0)
        def _():
            for r, val in upd:
                r[...] = val

        @pl.when(i != 0)
        def _():
            for r, val in upd:
                r[...] += val

    hs = lambda off: _halo_specs(tm, tn, n, _halo_rows(up.dtype), lambda j, i: (i, j + off))
    small = lambda off: pl.BlockSpec((3, tn), lambda j, i: (0, j + off))
    bias = lambda off: pl.BlockSpec((1, tn), lambda j, i: (0, j + off))
    blk = pl.BlockSpec((tm, tn), lambda j, i: (i, j))
    outs, landed = _pcall(
        body, name=name, grid=(nj, n // tm),
        in_specs=hs(0) + hs(nj) + [small(0), small(nj), bias(0), bias(nj), blk],
        out_specs=[blk, blk, small(0), small(0), bias(0), bias(0)],
        out_shape=[S((n, ff), BF16), S((n, ff), BF16), S((3, ff), F32), S((3, ff), F32), S((1, ff), F32),
                   S((1, ff), F32)],
        sem=("parallel", "arbitrary"), args=(up, up, up, up, up, up, cw, cw, cb, cb, da), side=side)
    return outs if side is None else (outs, landed)


def _conv_bwd_dx(dm, name, dyg, dyv, cw):
    n, ff, tm = dm.n_tok, dm.d_ff, dm.row_tile
    tn = _div_tile(ff, 1408, LANE)
    nj = ff // tn
    edges = _conv_edges(dm, tm)

    def body(g_ref, gp_ref, gn_ref, v_ref, vp_ref, vn_ref, wg_ref, wv_ref, o_ref):
        zp, zn = edges(pl.program_id(0))
        half = pl.program_id(1) // nj
        x = jnp.where(half == 0, g_ref[...], v_ref[...])
        xp = jnp.where(half == 0, gp_ref[...], vp_ref[...])
        xn = jnp.where(half == 0, gn_ref[...], vn_ref[...])
        w = jnp.where(half == 0, wg_ref[...], wv_ref[...])
        o_ref[...] = _conv3(x, xp, xn, w[2:3], w[1:2], w[0:1], zp, zn)[0].astype(o_ref.dtype)

    hr = _halo_rows(dyg.dtype)
    g_specs = _halo_specs(tm, tn, n, hr, lambda i, j: (i, jnp.minimum(j, nj - 1)))
    v_specs = _halo_specs(tm, tn, n, hr, lambda i, j: (i, jnp.maximum(j - nj, 0)))
    return pl.pallas_call(
        body, name=name, grid=(n // tm, 2 * nj),
        in_specs=g_specs + v_specs + [pl.BlockSpec((3, tn), lambda i, j: (0, jnp.minimum(j, nj - 1))),
                                      pl.BlockSpec((3, tn), lambda i, j: (0, nj + jnp.maximum(j - nj, 0)))],
        out_specs=pl.BlockSpec((tm, tn), lambda i, j: (i, j)), out_shape=S((n, 2 * ff), BF16),
        compiler_params=_cparams(("parallel", "parallel")))(dyg, dyg, dyg, dyv, dyv, dyv, cw, cw)


def _loss_head(dm, name, x, tgt, g):
    tm = dm.row_tile
    nc = dm.ctx_len // tm

    def fn(i, x, t, g):
        def f(x, g):
            err = _rms(x, g) - t
            return 0.5 * jnp.sum(jnp.mean(err * err, axis=-1, keepdims=True), axis=0, keepdims=True)

        loss, vjp = jax.vjp(f, x, g)
        dx, dg = vjp(jnp.ones((1, 1), F32))
        live = i >= nc
        return (jnp.where(live, dx, 0.0), jnp.where(live, jnp.broadcast_to(loss, (1, LANE)), 0.0),
                jnp.where(live, dg, 0.0))

    return _rows(name, fn, dm.n_tok, tm, [R(x), R(tgt, rmap=lambda i: jnp.maximum(i - nc, 0)), C(g)],
                 [(dm.d_model, F32)], [(1, LANE), (1, dm.d_model)])


def _dproj_assemble(dm, name, d_aq, d_ak, dv, dq_dir, df_dir, dv_dir, d_hgt, d_su, d_sv):
    def fn(i, d_aq, d_ak, dv, q0, q1, f0, f1, v0, v1, d_hgt, d_su, d_sv):
        parts = [d_aq, d_ak, dv, q0 + q1, f0, f1, v0 + v1, d_hgt, d_su, d_sv]
        return (jnp.concatenate([p.astype(F32) for p in parts], axis=1),)

    ins = [R(d_aq), R(d_ak), R(dv), R(dq_dir[0]), R(dq_dir[1]), R(df_dir[0]), R(df_dir[1]), R(dv_dir[0]),
           R(dv_dir[1]), R(d_hgt), R(d_su), R(d_sv)]
    return _rows(name, fn, dm.n_tok, dm.row_tile, ins, [(dm.in_cols, BF16)])[0]


def _layer_fwd(dm, l, x, h, mods, wl, tabs, nxt=None):
    ct, st = tabs
    proj = _matmul("proj", h, wl["w_in"], "nn", F32, b_shards=4)
    qh, kh = _qk_fwd(dm, "qk", proj, ct, st, wl["q_g"], wl["k_g"])
    if nxt is None:
        attn = _attn_fwd(dm, "attn", qh, kh, proj)
        o_dir, hist = _hgrn_fwd(dm, "hgrn", proj, wl["lb"])
    else:
        attn, g1 = _attn_fwd(dm, "attn", qh, kh, proj, side=_gather_own_side([nxt["w_up"], nxt["w_out"]]))
        sides = [_gather_own_side([nxt["w_in"], nxt["w_down"]]), _gather_pass_side(list(g1))]
        (o_dir, hist), landed = _hgrn_fwd(dm, "hgrn", proj, wl["lb"], side=_merge_sides(*sides))
        g2, g1 = _split_outs(sides, landed)
    hg = _hgc_fwd(dm, "hgc", o_dir, proj, wl["hg_g"])
    sg = _sg_fwd(dm, "sg", proj, wl["sg_g"], wl["sg_w"], wl["sg_bcol"])
    mix = jnp.concatenate([attn, hg, sg], axis=1)
    m = _matmul("out", mix, wl["w_out"], "nn", F32)
    x1, h2 = _resnorm_fwd(dm, "resnorm2", x, m, mods[2], wl["norm2_g"], mods[3], mods[4])
    if nxt is None:
        up, gathered = _matmul("up", h2, wl["w_up"], "nn", BF16, b_shards=4), None
    else:
        up, g2 = _matmul("up", h2, wl["w_up"], "nn", BF16, b_shards=4, side=_gather_pass_side(list(g2)))
        gathered = dict(w_up=g1[0], w_out=g1[1], w_in=g2[0], w_down=g2[1])
    a = _conv_fwd(dm, "conv", up, wl["conv_w"], wl["conv_b"])
    f = _matmul("down", a, wl["w_down"], "nn", F32)
    saved = dict(x=x, h=h, proj=proj, qh=qh, kh=kh, attn=attn, o_dir=o_dir, hist=hist, mix=mix, m=m, x1=x1, h2=h2,
                 up=up, a=a, f=f)
    return x1, f, saved, gathered


def _blocks42(g):
    return g.reshape(4, 2, -1, g.shape[-1])


def _layer_bwd(dm, l, dx2, sv, mods, wl, tabs, rs=None):
    ct, st = tabs
    g = {}
    df, g["mod5"] = _gate_bwd(dm, "b_gate5", dx2, sv["f"], mods[5])
    da = _matmul("b_da", df, wl["w_down"], "nt", F32)
    g["w_down"] = _matmul("b_wdown", sv["a"], df, "tn", BF16)
    if rs is None:
        dyg, dyv, dwg, dwv, dbg, dbv = _conv_bwd_dy(dm, "b_convdy", sv["up"], wl["conv_w"], wl["conv_b"], da)
    else:
        jobs1 = [((l, "w_down"), _blocks42(g.pop("w_down")))] + rs["pending"]
        rs["pending"] = []
        (dyg, dyv, dwg, dwv, dbg, dbv), recv = _conv_bwd_dy(dm, "b_convdy", sv["up"], wl["conv_w"], wl["conv_b"], da,
                                                             side=_swap_side([gb for _, gb in jobs1]))
        ps1 = [_pair_sum("rs_sum2", gb, r) for (_, gb), r in zip(jobs1, recv)]
    g["conv_w"] = jnp.concatenate([dwg, dwv], axis=1)
    g["conv_b"] = jnp.concatenate([dbg, dbv], axis=1)
    d_up = _conv_bwd_dx(dm, "b_convdx", dyg, dyv, wl["conv_w"])
    if rs is None:
        dh2 = _matmul("b_dh2", d_up, wl["w_up"], "nt", F32, b_shards=4)
    else:
        dh2, recv = _matmul("b_dh2", d_up, wl["w_up"], "nt", F32, b_shards=4, side=_xchg_side(ps1))
        ts1 = [_chip_sum("rs_sum4", p, r) for p, r in zip(ps1, recv)]
    g["w_up"] = _matmul("b_wup", sv["h2"], d_up, "tn", BF16, out_shards=4)
    dx1, g["norm2_g"], g["mod3"], g["mod4"] = _normmod_bwd(dm, "b_norm2", sv["x1"], dh2, dx2, wl["norm2_g"],
                                                             mods[3], mods[4])
    dmv, g["mod2"] = _gate_bwd(dm, "b_gate2", dx1, sv["m"], mods[2])
    dmix = _matmul("b_dmix", dmv, wl["w_out"], "nt", F32)
    g["w_out"] = _matmul("b_wout", sv["mix"], dmv, "tn", BF16)
    proj = sv["proj"]
    if rs is None:
        dqh, dkh, dv = _attn_bwd(dm, "b_attn", sv["qh"], sv["kh"], proj, sv["attn"], dmix)
    else:
        jobs2 = [((l, "w_up"), _blocks42(g.pop("w_up"))), ((l, "w_out"), _blocks42(g.pop("w_out")))]
        sides = [_share_side(ts1), _swap_side([gb for _, gb in jobs2])]
        (dqh, dkh, dv), landed = _attn_bwd(dm, "b_attn", sv["qh"], sv["kh"], proj, sv["attn"], dmix,
                                           side=_merge_sides(*sides))
        fin, recv = _split_outs(sides, landed)
        for (key, gb), t in zip(jobs1, fin):
            rs["done"][key] = t.reshape(-1, t.shape[-1])
        ps2 = [_pair_sum("rs_sum2", gb, r) for (_, gb), r in zip(jobs2, recv)]
    d_aq, d_ak, g["q_g"], g["k_g"] = _qk_bwd(dm, "b_qk", proj, ct, st, wl["q_g"], wl["k_g"], dqh, dkh)
    do, d_hgt, g["hg_g"] = _hgc_bwd(dm, "b_hgc", sv["o_dir"], proj, wl["hg_g"], dmix)
    if rs is None:
        dq_dir, df_dir, dv_dir, g["lb"] = _hgrn_bwd(dm, "b_hgrn", proj, wl["lb"], sv["hist"], do)
    else:
        (dq_dir, df_dir, dv_dir, g["lb"]), recv = _hgrn_bwd(dm, "b_hgrn", proj, wl["lb"], sv["hist"], do,
                                                            side=_xchg_side(ps2))
        ts2 = [_chip_sum("rs_sum4", p, r) for p, r in zip(ps2, recv)]
    d_su, d_sv, g["sg_g"], g["sg_w"], g["sg_bcol"] = _sg_bwd(dm, "b_sg", proj, wl["sg_g"], wl["sg_w"],
                                                            wl["sg_bcol"], dmix)
    dproj = _dproj_assemble(dm, "b_dproj", d_aq, d_ak, dv, dq_dir, df_dir, dv_dir, d_hgt, d_su, d_sv)
    if rs is None:
        dh = _matmul("b_dh", dproj, wl["w_in"], "nt", F32, b_shards=4)
    else:
        dh, fin = _matmul("b_dh", dproj, wl["w_in"], "nt", F32, b_shards=4, side=_share_side(ts2))
        for (key, gb), t in zip(jobs2, fin):
            rs["done"][key] = t.reshape(-1, t.shape[-1])
    g["w_in"] = _matmul("b_win", sv["h"], dproj, "tn", BF16, out_shards=4)
    if rs is not None:
        rs["pending"] = [((l, "w_in"), _blocks42(g.pop("w_in")))]
    dx, g["norm1_g"], g["mod0"], g["mod1"] = _normmod_bwd(dm, "b_norm1", sv["x"], dh, dx1, wl["norm1_g"],
                                                           mods[0], mods[1])
    return dx, g


def _sample_step(dm, x_all, tgt, mods, wls, final_g, tabs, half_blocks=None, rs=None):
    def laid_out(g):
        return dict(w_in=g["w_in"].reshape(4, dm.d_model, -1), w_up=g["w_up"].reshape(4, dm.d_model, -1),
                    w_out=g["w_out"].reshape(dm.d_mix, dm.d_model), w_down=g["w_down"].reshape(dm.d_ff, dm.d_model))

    if half_blocks is not None:
        wls = [dict(wl) for wl in wls]
        wls[0].update(laid_out({k: _all_gather8("ag_" + k, v) for k, v in half_blocks[0].items()}))
    saved = []
    x = x_all
    h = _normmod_fwd(dm, "norm1", x, wls[0]["norm1_g"], mods[0][0], mods[0][1])
    for l in range(dm.depth):
        nxt = half_blocks[l + 1] if half_blocks is not None and l + 1 < dm.depth else None
        x1, f, sv, gathered = _layer_fwd(dm, l, x, h, mods[l], wls[l], tabs, nxt=nxt)
        if gathered is not None:
            wls[l + 1].update(laid_out(gathered))
        saved.append(sv)
        if l + 1 < dm.depth:
            x, h = _resnorm_fwd(dm, "resnorm1", x1, f, mods[l][5], wls[l + 1]["norm1_g"], mods[l + 1][0],
                                mods[l + 1][1])
        else:
            x = _res_fwd(dm, "res", x1, f, mods[l][5])
    dx, loss, dfg = _loss_head(dm, "loss_head", x, tgt, final_g)
    grads = [None] * dm.depth
    for l in reversed(range(dm.depth)):
        dx, grads[l] = _layer_bwd(dm, l, dx, saved[l], mods[l], wls[l], tabs, rs=rs)
    if rs is not None:
        for key, gb in rs["pending"]:
            rs["done"][key] = _reduce_scatter_grad("rs_tail", gb)
        rs["pending"] = []
    return loss, dx, grads, dfg


def _all_gather8(name, blk):
    r, cdim = blk.shape

    def body(x_ref, out_ref, send_sems, recv_sems, local_sem):
        x, y, c = _place()
        me, sibling = (x, y, c), (x, y, 1 - c)
        chips = [(1 - x, y), (x, 1 - y), (1 - x, 1 - y)]

        def slot(px, py, pc):
            return out_ref.at[4 * px + 2 * py + pc]

        def copy(k, block, to, src=None):
            return pltpu.make_async_remote_copy(
                src_ref=slot(*block) if src is None else src, dst_ref=slot(*block),
                send_sem=send_sems.at[k], recv_sem=recv_sems.at[k], device_id=to, device_id_type=MESH)

        mine = pltpu.make_async_copy(x_ref, slot(*me), local_sem)
        mine.start()
        first = [copy(0, me, sibling, src=x_ref)]
        first += [copy(1 + j, me, (*chip, c), src=x_ref) for j, chip in enumerate(chips)]
        for cp in first:
            cp.start()
        passed = [copy(4 + j, (*chip, c), sibling) for j, chip in enumerate(chips)]
        for j, chip in enumerate(chips):
            copy(1 + j, (*chip, c), me).wait_recv()
            passed[j].start()
        copy(0, sibling, me).wait_recv()
        for j, chip in enumerate(chips):
            copy(4 + j, (*chip, 1 - c), me).wait_recv()
        for cp in first + passed:
            cp.wait_send()
        mine.wait()

    return pl.pallas_call(
        body, name=name, out_shape=S((8, r, cdim), blk.dtype), in_specs=[_ANY], out_specs=_ANY,
        scratch_shapes=[pltpu.SemaphoreType.DMA((7,)), pltpu.SemaphoreType.DMA((7,)), pltpu.SemaphoreType.DMA])(blk)


def _pair_swap_halves(name, g):
    n_s, _, r, cdim = g.shape

    def body(g_ref, recv_ref, send_sems, recv_sems):
        x, y, c = _place()
        remote = [pltpu.make_async_remote_copy(src_ref=g_ref.at[s, 1 - c], dst_ref=recv_ref.at[s],
                                               send_sem=send_sems.at[s], recv_sem=recv_sems.at[s],
                                               device_id=(x, y, 1 - c), device_id_type=MESH) for s in range(n_s)]
        for cp in remote:
            cp.start()
        for cp in remote:
            cp.wait()

    return pl.pallas_call(
        body, name=name, out_shape=S((n_s, r, cdim), g.dtype), in_specs=[_ANY], out_specs=_ANY,
        scratch_shapes=[pltpu.SemaphoreType.DMA((n_s,))] * 2)(g)


def _pair_sum(name, g, recv):
    n_s, _, r, cdim = g.shape
    tm = _ew_tile(r, cdim, 4)

    def body(g0_ref, g1_ref, r_ref, o_ref):
        own = jnp.where(lax.axis_index("c") == 0, g0_ref[...].astype(F32), g1_ref[...].astype(F32))
        o_ref[...] = (own + r_ref[...].astype(F32)).astype(o_ref.dtype)

    return pl.pallas_call(
        body, name=name, grid=(n_s, r // tm),
        in_specs=[pl.BlockSpec((None, None, tm, cdim), lambda s, i: (s, 0, i, 0)),
                  pl.BlockSpec((None, None, tm, cdim), lambda s, i: (s, 1, i, 0)),
                  pl.BlockSpec((None, tm, cdim), lambda s, i: (s, i, 0))],
        out_specs=pl.BlockSpec((None, tm, cdim), lambda s, i: (s, i, 0)), out_shape=S((n_s, r, cdim), BF16),
        compiler_params=_cparams(("parallel", "parallel")))(g, g, recv)


def _chip_exchange(name, p):
    _, r, cdim = p.shape

    def body(p_ref, recv_ref, send_sems, recv_sems):
        x, y, c = _place()
        peers = [(1 - x, y), (x, 1 - y), (1 - x, 1 - y)]
        remote = [pltpu.make_async_remote_copy(src_ref=p_ref.at[2 * px + py], dst_ref=recv_ref.at[k],
                                               send_sem=send_sems.at[k], recv_sem=recv_sems.at[k],
                                               device_id=(px, py, c), device_id_type=MESH)
                  for k, (px, py) in enumerate(peers)]
        for cp in remote:
            cp.start()
        for cp in remote:
            cp.wait()

    return pl.pallas_call(
        body, name=name, out_shape=S((3, r, cdim), p.dtype), in_specs=[_ANY], out_specs=_ANY,
        scratch_shapes=[pltpu.SemaphoreType.DMA((3,)), pltpu.SemaphoreType.DMA((3,))])(p)


def _chip_sum(name, p, recv):
    n_s, r, cdim = p.shape
    tm = _ew_tile(r, cdim, 9)

    def body(*refs):
        chip = 2 * lax.axis_index("x") + lax.axis_index("y")
        own = refs[n_s - 1][...].astype(F32)
        for s in range(n_s - 2, -1, -1):
            own = jnp.where(chip == s, refs[s][...].astype(F32), own)
        r0, r1, r2, o_ref = refs[n_s:]
        tot = ((own + r0[...].astype(F32)) + r1[...].astype(F32)) + r2[...].astype(F32)
        o_ref[0] = tot
        o_ref[1] = tot

    blk = lambda s: pl.BlockSpec((None, tm, cdim), functools.partial(lambda i, s: (s, i, 0), s=s))
    return pl.pallas_call(
        body, name=name, grid=(r // tm,), in_specs=[blk(s) for s in range(n_s)] + [blk(k) for k in range(3)],
        out_specs=pl.BlockSpec((2, tm, cdim), lambda i: (0, i, 0)), out_shape=S((2, r, cdim), F32),
        compiler_params=_cparams(("parallel",)))(*([p] * n_s), *([recv] * 3))


def _pair_share(name, t2):
    def body(t_ref, out_ref, send_sem, recv_sem):
        x, y, c = _place()
        remote = pltpu.make_async_remote_copy(src_ref=out_ref.at[c], dst_ref=out_ref.at[c], send_sem=send_sem,
                                              recv_sem=recv_sem, device_id=(x, y, 1 - c), device_id_type=MESH)
        remote.start()
        remote.wait()

    return pl.pallas_call(
        body, name=name, out_shape=S(t2.shape, t2.dtype), in_specs=[_ANY], out_specs=_ANY,
        input_output_aliases={0: 0},
        scratch_shapes=[pltpu.SemaphoreType.DMA, pltpu.SemaphoreType.DMA])(t2)


def _ew_tile(rows, cols, n_arrays):
    cap = min(1024, max(16, (24 * 1024 * 1024) // (n_arrays * 2 * cols * 4)))
    if rows <= 16:
        return rows
    mult = 16 if any(rows % t == 0 for t in range(16, cap + 1, 16)) else 8
    return _div_tile(rows, cap, mult)


def _reduce_scatter_grad(name, gb):
    _, _, r, cdim = gb.shape
    p = _pair_sum(name + "_sum2", gb, _pair_swap_halves(name + "_swap", gb))
    tot2 = _chip_sum(name + "_sum4", p, _chip_exchange(name + "_xchg", p))
    return _pair_share(name + "_share", tot2).reshape(2 * r, cdim)


def _ada_fwd(name, a16, w_ada, b_cols):
    depth, d, cols = w_ada.shape
    tn = _div_tile(cols, 1536, LANE)

    def body(a_ref, w_ref, b_ref, o_ref):
        a = a_ref[...]
        o_ref[...] = _dg(a * jax.nn.sigmoid(a), w_ref[...], _NN) + b_ref[...]

    return pl.pallas_call(
        body, name=name, grid=(depth, cols // tn),
        in_specs=[pl.BlockSpec((16, d), lambda l, j: (0, 0)), pl.BlockSpec((None, d, tn), lambda l, j: (l, 0, j)),
                  pl.BlockSpec((None, 1, tn), lambda l, j: (l, 0, j))],
        out_specs=pl.BlockSpec((None, 16, tn), lambda l, j: (l, 0, j)), out_shape=S((depth, 16, cols), F32),
        compiler_params=_cparams(("parallel", "parallel")))(a16, w_ada, b_cols)


def _ada_bwd_w(name, a_t, dmod):
    depth, _, cols = dmod.shape
    d = a_t.shape[0]
    tm, tn = _div_tile(d, 512, 8), _div_tile(cols, 1536, LANE)

    def body(a_ref, g_ref, o_ref):
        a = a_ref[...]
        o_ref[...] = _dg(a * jax.nn.sigmoid(a), g_ref[...], _NN)

    return pl.pallas_call(
        body, name=name, grid=(depth, d // tm, cols // tn),
        in_specs=[pl.BlockSpec((tm, 16), lambda l, i, j: (i, 0)), pl.BlockSpec((None, 16, tn), lambda l, i, j: (l, 0, j))],
        out_specs=pl.BlockSpec((None, tm, tn), lambda l, i, j: (l, i, j)), out_shape=S((depth, d, cols), F32),
        compiler_params=_cparams(("parallel", "parallel", "parallel")))(a_t, dmod)


def _ada_bwd_a(name, dmod, w_ada):
    depth, d, cols = w_ada.shape
    tn = _div_tile(cols, 1536, LANE)
    nj = cols // tn

    def body(g_ref, w_ref, o_ref):
        first = jnp.logical_and(pl.program_id(0) == 0, pl.program_id(1) == 0)
        p = _dg(g_ref[...], w_ref[...], _NT)

        @pl.when(first)
        def _():
            o_ref[...] = p

        @pl.when(jnp.logical_not(first))
        def _():
            o_ref[...] += p

    return pl.pallas_call(
        body, name=name, grid=(depth, nj),
        in_specs=[pl.BlockSpec((None, 16, tn), lambda l, j: (l, 0, j)), pl.BlockSpec((None, d, tn), lambda l, j: (l, 0, j))],
        out_specs=pl.BlockSpec((16, d), lambda l, j: (0, 0)), out_shape=S((16, d), F32),
        compiler_params=_cparams(("arbitrary", "arbitrary")))(dmod, w_ada)


def _lbs_fn(p):
    depth = p.shape[0]
    rows = [p[l] for l in range(depth)]
    mx = functools.reduce(jnp.maximum, rows)
    ex = [jnp.exp(r - mx) for r in rows]
    den = functools.reduce(lambda a, b: a + b, ex)
    sm = [e / den for e in ex]
    out, run = [], None
    for l in range(depth):
        run = sm[l] if run is None else run + sm[l]
        out.append(run - sm[0])
    return jnp.stack(out, axis=0)


def _lbs_fwd(name, p):
    def body(p_ref, o_ref):
        o_ref[...] = _lbs_fn(p_ref[...])

    return pl.pallas_call(body, name=name, out_shape=S(p.shape, F32))(p)


def _lbs_bwd(name, p, d_out):
    def body(p_ref, g_ref, o_ref):
        _, vjp = jax.vjp(_lbs_fn, p_ref[...])
        o_ref[...] = vjp(g_ref[...])[0]

    return pl.pallas_call(body, name=name, out_shape=S(p.shape, F32))(p, d_out)


def _sum8(name, g):
    _, r, cdim = g.shape
    tm = _ew_tile(r, cdim, 9)

    def body(g_ref, o_ref):
        acc = g_ref[0]
        for k in range(1, 8):
            acc = acc + g_ref[k]
        o_ref[...] = acc

    return pl.pallas_call(body, name=name, grid=(r // tm,),
                          in_specs=[pl.BlockSpec((8, tm, cdim), lambda i: (0, i, 0))],
                          out_specs=pl.BlockSpec((tm, cdim), lambda i: (i, 0)), out_shape=S((r, cdim), F32),
                          compiler_params=_cparams(("parallel",)))(g)


def _adamw(name, w, m, v, g):
    rows, cols = w.shape
    tm = _ew_tile(rows, cols, 7)

    def fn(i, w, m, v, g):
        m = ADAM_B1 * m + (1.0 - ADAM_B1) * g
        v = ADAM_B2 * v + (1.0 - ADAM_B2) * jnp.square(g)
        m_hat = m / (1.0 - ADAM_B1 ** ADAM_STEP)
        v_hat = v / (1.0 - ADAM_B2 ** ADAM_STEP)
        return -ADAM_LR * (m_hat / (jnp.sqrt(v_hat) + ADAM_EPS) + ADAM_WD * w), m, v

    return _rows(name, fn, rows, tm, [R(w), R(m), R(v), R(g)], [(cols, F32)] * 3)


def _silu_grad_mul(name, g, z):
    def body(g_ref, z_ref, o_ref):
        zz = z_ref[...]
        sg = jax.nn.sigmoid(zz)
        o_ref[...] = g_ref[...] * (sg * (1.0 + zz * (1.0 - sg)))

    return pl.pallas_call(body, name=name, out_shape=S(g.shape, F32))(g, z)


def _pack(arrs):
    parts, meta, off = [], [], 0
    for a in arrs:
        n = int(np.prod(a.shape))
        rows = -(-n // (8 * LANE)) * 8
        flat = a.reshape(-1).astype(F32)
        parts.append(jnp.pad(flat, (0, rows * LANE - n)).reshape(rows, LANE))
        meta.append((off, rows, n, a.shape))
        off += rows
    return jnp.concatenate(parts, axis=0), meta


def _unpack(buf, meta, lead=()):
    out = []
    for off, rows, n, shape in meta:
        seg = buf[..., off:off + rows, :].reshape(*lead, rows * LANE)[..., :n]
        out.append(seg.reshape(*lead, *shape))
    return out


_SMALL = ("c_ctx", "b_ada", "norm1_g", "q_norm_g", "k_norm_g", "hg_lower_bounds", "hg_norm_g", "sg_norm_g", "sg_w",
          "sg_b", "norm2_g", "conv_w", "conv_b", "final_norm_g")
_BIG = ("w_ada", "w_in", "w_out", "w_up", "w_down")
_WEIGHTS = ("c_ctx", "w_ada", "b_ada", "norm1_g", "w_in", "q_norm_g", "k_norm_g", "hg_lower_bounds", "hg_norm_g",
            "sg_norm_g", "sg_w", "sg_b", "w_out", "norm2_g", "w_up", "conv_w", "conv_b", "w_down", "final_norm_g")


def _dims_of(x, ctx, w_in, w_down):
    return Dims(d_model=x.shape[-1], seq=x.shape[1], ctx_len=ctx.shape[1], depth=w_in.shape[0],
                d_ff=w_down.shape[1] * 4)


def _step(dm, x, c, ctx, tgt, w, m, v):
    d, depth = dm.d_model, dm.depth
    xi, yi, ci = _place()
    chip = 2 * xi + yi
    me = 4 * xi + 2 * yi + ci
    n_chips = 4
    take_chips = lambda g8: g8[0::2]

    small_in, meta_in = _pack([c, w["conv_w"], w["hg_lower_bounds"]])
    gath = _all_gather8("ag_small_in", small_in)
    c_all, conv_sh, lb_sh = _unpack(gath, meta_in, lead=(8,))
    c_all = c_all.reshape(8, d)
    conv_w = take_chips(conv_sh).transpose(1, 2, 0, 3).reshape(depth, 3, 2 * dm.d_ff)
    lb_logits = take_chips(lb_sh).transpose(1, 2, 0, 3).reshape(2, depth, dm.hg_w)
    lb_p = lb_logits.transpose(1, 0, 2)
    lbs = _lbs_fwd("lbs_fwd", lb_p)

    a16 = jnp.concatenate([c_all, w["c_ctx"][None], jnp.zeros((7, d), F32)], axis=0)
    cols = w["w_ada"].shape[-1]
    b_cols = lax.dynamic_slice_in_dim(w["b_ada"], chip * cols, cols, axis=1)[:, None, :]
    mod_sh = _ada_fwd("ada_fwd", a16, w["w_ada"], b_cols)
    mod_g = take_chips(_all_gather8("ag_mod", mod_sh.reshape(depth * 16, cols)))
    mod_all = mod_g.reshape(n_chips, depth, 16, cols).transpose(1, 2, 0, 3).reshape(depth, 16, n_chips * cols)
    mod_lat = lax.dynamic_index_in_dim(mod_all, me, axis=1, keepdims=False)
    mod_ctx = mod_all[:, 8]
    mods = [[jnp.stack([mod_ctx[l, k * d:(k + 1) * d], mod_lat[l, k * d:(k + 1) * d]]) for k in range(N_MOD)]
            for l in range(depth)]

    def my_half(shard):
        half = shard.shape[0] // 2
        return lax.dynamic_slice_in_dim(shard.astype(BF16), ci * half, half, axis=0)

    wls, half_blocks = [], []
    for l in range(depth):
        half_blocks.append({name: my_half(w[name][l]) for name in ("w_in", "w_up", "w_out", "w_down")})
        wls.append(dict(
            conv_w=conv_w[l], conv_b=w["conv_b"][l][None], norm1_g=w["norm1_g"][l][None],
            norm2_g=w["norm2_g"][l][None], q_g=w["q_norm_g"][l][None], k_g=w["k_norm_g"][l][None],
            hg_g=w["hg_norm_g"][l][None], sg_g=w["sg_norm_g"][l][None], sg_w=w["sg_w"][l],
            sg_bcol=w["sg_b"][l][:, :, None], lb=lbs[l].reshape(2, 1, dm.hg_w)))

    x_all = jnp.concatenate([ctx[0], x[0]], axis=0)
    rs = dict(pending=[], done={})
    loss_row, dx_all, grads, dfg = _sample_step(dm, x_all, tgt[0], mods, wls, w["final_norm_g"][None],
                                                _rope_tables(dm), half_blocks=half_blocks, rs=rs)
    loss = lax.psum(loss_row[0, 0], ("x", "y", "c"))
    grad_x = dx_all[dm.ctx_len:][None]

    g_big = {name: jnp.stack([rs["done"][(l, name)] for l in range(depth)])
             for name in ("w_in", "w_up", "w_out", "w_down")}

    dmod_lat = jnp.stack([jnp.concatenate([grads[l][f"mod{k}"][1] for k in range(N_MOD)]) for l in range(depth)])
    dmod_ctx = jnp.stack([jnp.concatenate([grads[l][f"mod{k}"][0] for k in range(N_MOD)]) for l in range(depth)])
    d_lbs = jnp.stack([grads[l]["lb"].reshape(2, dm.hg_w) for l in range(depth)])
    d_lb_p = _lbs_bwd("lbs_bwd", lb_p, d_lbs).transpose(1, 0, 2)
    stk = lambda key: jnp.stack([grads[l][key] for l in range(depth)])
    part = {
        "b_ada": dmod_lat + dmod_ctx, "norm1_g": stk("norm1_g")[:, 0], "q_norm_g": stk("q_g")[:, 0],
        "k_norm_g": stk("k_g")[:, 0], "hg_lower_bounds": d_lb_p, "hg_norm_g": stk("hg_g")[:, 0],
        "sg_norm_g": stk("sg_g")[:, 0], "sg_w": stk("sg_w"), "sg_b": stk("sg_bcol")[..., 0],
        "norm2_g": stk("norm2_g")[:, 0], "conv_w": stk("conv_w"), "conv_b": stk("conv_b")[:, 0],
        "final_norm_g": dfg[0]}
    names = [n for n in _SMALL if n != "c_ctx"]
    packed, meta = _pack([part[n] for n in names] + [dmod_ctx, dmod_lat])
    gath = _all_gather8("ag_small_grads", packed)
    summed = _unpack(_sum8("sum_small_grads", gath), meta)
    g_small = dict(zip(names, summed[:len(names)]))
    dmod_ctx_tot = summed[len(names)]
    dmod_lat_all = _unpack(gath, meta[-1:], lead=(8,))[0]

    dmod16 = jnp.concatenate([dmod_lat_all.transpose(1, 0, 2), dmod_ctx_tot[:, None], jnp.zeros((depth, 7, 6 * d), F32)],
                             axis=1)
    dmod16 = lax.dynamic_slice_in_dim(dmod16, chip * cols, cols, axis=2)
    g_big["w_ada"] = _ada_bwd_w("ada_bwd_w", a16.T, dmod16)
    da16 = _ada_bwd_a("ada_bwd_a", dmod16, w["w_ada"])
    da_g = take_chips(_all_gather8("ag_dctx", da16))
    da_sum = _rows("sum_dctx", lambda i, a, b, c2, d2: (((a + b) + c2) + d2,), 16, 16,
                   [R(da_g[k]) for k in range(n_chips)], [(d, F32)])[0]
    g_small["c_ctx"] = _silu_grad_mul("dctx_silu", da_sum[8:9], w["c_ctx"][None])[0]

    g_small["conv_w"] = lax.dynamic_slice_in_dim(g_small["conv_w"], chip * w["conv_w"].shape[-1], w["conv_w"].shape[-1], axis=2)
    g_small["hg_lower_bounds"] = lax.dynamic_slice_in_dim(g_small["hg_lower_bounds"], chip * w["hg_lower_bounds"].shape[-1],
                                                          w["hg_lower_bounds"].shape[-1], axis=2)

    grads_out, deltas, new_m, new_v = {}, {}, {}, {}
    for name in _BIG:
        shp = w[name].shape
        flat = lambda a: a.reshape(-1, shp[-1])
        dl, nm, nv = _adamw(f"adamw_{name}", flat(w[name]), flat(m[name]), flat(v[name]), flat(g_big[name]))
        grads_out[name] = g_big[name].reshape(shp)
        deltas[name], new_m[name], new_v[name] = dl.reshape(shp), nm.reshape(shp), nv.reshape(shp)
    pw, meta_s = _pack([w[n] for n in _SMALL])
    pm, _ = _pack([m[n] for n in _SMALL])
    pv, _ = _pack([v[n] for n in _SMALL])
    pg, _ = _pack([g_small[n].reshape(w[n].shape) for n in _SMALL])
    dl, nm, nv = _adamw("adamw_small", pw, pm, pv, pg)
    for name, a, b, c2 in zip(_SMALL, _unpack(dl, meta_s), _unpack(nm, meta_s), _unpack(nv, meta_s)):
        grads_out[name] = g_small[name].reshape(w[name].shape)
        deltas[name], new_m[name], new_v[name] = a, b, c2
    return loss, grad_x, grads_out, deltas, new_m, new_v


def kernel(x, c, ctx, c_ctx, w_ada, b_ada, norm1_g, w_in, q_norm_g, k_norm_g, hg_lower_bounds, hg_norm_g, sg_norm_g, sg_w, sg_b, w_out, norm2_g, w_up, conv_w, conv_b, w_down, final_norm_g, loss_target, m_c_ctx, m_w_ada, m_b_ada, m_norm1_g, m_w_in, m_q_norm_g, m_k_norm_g, m_hg_lower_bounds, m_hg_norm_g, m_sg_norm_g, m_sg_w, m_sg_b, m_w_out, m_norm2_g, m_w_up, m_conv_w, m_conv_b, m_w_down, m_final_norm_g, v_c_ctx, v_w_ada, v_b_ada, v_norm1_g, v_w_in, v_q_norm_g, v_k_norm_g, v_hg_lower_bounds, v_hg_norm_g, v_sg_norm_g, v_sg_w, v_sg_b, v_w_out, v_norm2_g, v_w_up, v_conv_w, v_conv_b, v_w_down, v_final_norm_g):
    w = dict(c_ctx=c_ctx, w_ada=w_ada, b_ada=b_ada, norm1_g=norm1_g, w_in=w_in, q_norm_g=q_norm_g, k_norm_g=k_norm_g, hg_lower_bounds=hg_lower_bounds, hg_norm_g=hg_norm_g, sg_norm_g=sg_norm_g, sg_w=sg_w, sg_b=sg_b, w_out=w_out, norm2_g=norm2_g, w_up=w_up, conv_w=conv_w, conv_b=conv_b, w_down=w_down, final_norm_g=final_norm_g)
    m = dict(c_ctx=m_c_ctx, w_ada=m_w_ada, b_ada=m_b_ada, norm1_g=m_norm1_g, w_in=m_w_in, q_norm_g=m_q_norm_g, k_norm_g=m_k_norm_g, hg_lower_bounds=m_hg_lower_bounds, hg_norm_g=m_hg_norm_g, sg_norm_g=m_sg_norm_g, sg_w=m_sg_w, sg_b=m_sg_b, w_out=m_w_out, norm2_g=m_norm2_g, w_up=m_w_up, conv_w=m_conv_w, conv_b=m_conv_b, w_down=m_w_down, final_norm_g=m_final_norm_g)
    v = dict(c_ctx=v_c_ctx, w_ada=v_w_ada, b_ada=v_b_ada, norm1_g=v_norm1_g, w_in=v_w_in, q_norm_g=v_q_norm_g, k_norm_g=v_k_norm_g, hg_lower_bounds=v_hg_lower_bounds, hg_norm_g=v_hg_norm_g, sg_norm_g=v_sg_norm_g, sg_w=v_sg_w, sg_b=v_sg_b, w_out=v_w_out, norm2_g=v_norm2_g, w_up=v_w_up, conv_w=v_conv_w, conv_b=v_conv_b, w_down=v_w_down, final_norm_g=v_final_norm_g)
    dm = _dims_of(x, ctx, w_in, w_down)
    loss, grad_x, g, dl, nm, nv = _step(dm, x, c, ctx, loss_target, w, m, v)
    return (loss, grad_x, *[g[n] for n in _WEIGHTS], *[dl[n] for n in _WEIGHTS], *[nm[n] for n in _WEIGHTS],
            *[nv[n] for n in _WEIGHTS])
```

```python
import functools
import math
from typing import NamedTuple

import numpy as np
import jax
import jax.numpy as jnp
from jax import lax
from jax.experimental import pallas as pl
from jax.experimental.pallas import tpu as pltpu

F32, BF16 = jnp.float32, jnp.bfloat16
S = jax.ShapeDtypeStruct
MESH = pl.DeviceIdType.MESH

LANE = 128
EPS = 1e-6
F_MIN = 1e-30
ROPE_THETA = 10000.0
N_MOD = 6
ADAM_LR, ADAM_B1, ADAM_B2, ADAM_EPS, ADAM_WD, ADAM_STEP = 0.001, 0.9, 0.999, 1e-08, 0.01, 10
VMEM_LIMIT = 56 * 1024 * 1024


class Dims(NamedTuple):
    d_model: int = 2048
    seq: int = 4096
    ctx_len: int = 256
    grid_w: int = 64
    depth: int = 4
    attn_heads: int = 8
    kv_heads: int = 2
    hg_heads: int = 4
    hg_chunk: int = 16
    sg_groups: int = 4
    d_ff: int = 5632

    @property
    def n_tok(self):
        return self.seq + self.ctx_len

    @property
    def q_w(self):
        return self.attn_heads * LANE

    @property
    def kv_w(self):
        return self.kv_heads * LANE

    @property
    def hg_w(self):
        return self.hg_heads * LANE

    @property
    def sg_w(self):
        return self.sg_groups * LANE

    @property
    def d_mix(self):
        return self.q_w + self.hg_w + self.sg_w

    @property
    def in_sizes(self):
        return (self.q_w, self.kv_w, self.kv_w) + (self.hg_w,) * 5 + (self.sg_w,) * 2

    @property
    def in_cols(self):
        return sum(self.in_sizes)

    @property
    def in_offs(self):
        return tuple(int(v) for v in np.cumsum((0,) + self.in_sizes)[:-1])

    @property
    def row_tile(self):
        return min(256, self.ctx_len)


def _cparams(sem, vmem=VMEM_LIMIT):
    return pltpu.CompilerParams(dimension_semantics=sem, vmem_limit_bytes=vmem)


_ANY = pl.BlockSpec(memory_space=pl.ANY)


class _Side(NamedTuple):
    tag: str
    ins: tuple
    outs: tuple
    alias: tuple
    n_remote: int
    n_local: int
    make: object


def _merge_sides(*sides):
    sides = [s for s in sides if s is not None]
    if len(sides) <= 1:
        return sides[0] if sides else None
    offs, o_in, o_out, o_r, o_l = [], 0, 0, 0, 0
    for s in sides:
        offs.append((o_in, o_out, o_r, o_l))
        o_in, o_out, o_r, o_l = o_in + len(s.ins), o_out + len(s.outs), o_r + s.n_remote, o_l + s.n_local

    def make(sins, souts, sem, lsem):
        remote, local = [], []
        for s, (a, b, r, l) in zip(sides, offs):
            rr, ll = s.make(sins[a:a + len(s.ins)], souts[b:b + len(s.outs)],
                            functools.partial(lambda k, r: sem(r + k), r=r), functools.partial(lambda k, l: lsem(l + k), l=l))
            remote, local = remote + rr, local + ll
        return remote, local

    return _Side("_".join(s.tag for s in sides), sum((s.ins for s in sides), ()), sum((s.outs for s in sides), ()),
                 tuple((a + i, b + o) for s, (a, b, _, _) in zip(sides, offs) for i, o in s.alias), o_r, o_l, make)


def _split_outs(sides, outs):
    res, k = [], 0
    for s in sides:
        if s is not None:
            res.append(tuple(outs[k:k + len(s.outs)]))
            k += len(s.outs)
        else:
            res.append(())
    return res


def _side_scratch(side):
    dma = pltpu.SemaphoreType.DMA
    return [dma((max(1, side.n_remote),)), dma((max(1, side.n_remote),)), dma((max(1, side.n_local),))]


def _pcall(body, *, name, grid, in_specs, out_specs, out_shape, args, sem, scratch_shapes=(), side=None):
    single = not isinstance(out_shape, (list, tuple))
    out_shape_l = [out_shape] if single else list(out_shape)
    out_specs_l = [out_specs] if single else list(out_specs)
    if side is None:
        res = pl.pallas_call(body, name=name, grid=grid, in_specs=list(in_specs), out_specs=out_specs_l,
                             out_shape=out_shape_l, scratch_shapes=list(scratch_shapes),
                             compiler_params=_cparams(sem))(*args)
        return (res[0] if single else res), ()
    n_in, n_out, n_scr = len(in_specs), len(out_shape_l), len(scratch_shapes)
    s_in, s_out = len(side.ins), len(side.outs)
    g = tuple(grid)

    def wrapped(*refs):
        ins, sins = refs[:n_in], refs[n_in:n_in + s_in]
        o0 = n_in + s_in
        outs, souts = refs[o0:o0 + n_out], refs[o0 + n_out:o0 + n_out + s_out]
        scr = refs[o0 + n_out + s_out:o0 + n_out + s_out + n_scr]
        send_sems, recv_sems, local_sems = refs[-3:]
        ids = [pl.program_id(a) for a in range(len(g))]
        first = functools.reduce(jnp.logical_and, [i == 0 for i in ids])
        last = functools.reduce(jnp.logical_and, [i == n - 1 for i, n in zip(ids, g)])
        remote, local = side.make(sins, souts, lambda k: (send_sems.at[k], recv_sems.at[k]), lambda k: local_sems.at[k])

        @pl.when(first)
        def _():
            for cp in local:
                cp.start()
            for snd, _ in remote:
                snd.start()

        body(*ins, *outs, *scr)

        @pl.when(last)
        def _():
            for snd, arr in remote:
                snd.wait_send()
                arr.wait_recv()
            for cp in local:
                cp.wait()

    res = pl.pallas_call(
        wrapped, name=f"{name}_{side.tag}", grid=g, in_specs=list(in_specs) + [_ANY] * s_in,
        out_specs=out_specs_l + [_ANY] * s_out, out_shape=out_shape_l + list(side.outs),
        scratch_shapes=list(scratch_shapes) + _side_scratch(side),
        input_output_aliases={n_in + i: n_out + o for i, o in side.alias},
        compiler_params=_cparams(("arbitrary",) * len(g)))(*args, *side.ins)
    outs = res[:n_out]
    return (outs[0] if single else outs), tuple(res[n_out:])


def _run_side(name, side):
    s_in = len(side.ins)

    def body(*refs):
        sins, souts = refs[:s_in], refs[s_in:s_in + len(side.outs)]
        send_sems, recv_sems, local_sems = refs[-3:]
        remote, local = side.make(sins, souts, lambda k: (send_sems.at[k], recv_sems.at[k]), lambda k: local_sems.at[k])
        for cp in local:
            cp.start()
        for snd, _ in remote:
            snd.start()
        for snd, arr in remote:
            snd.wait_send()
            arr.wait_recv()
        for cp in local:
            cp.wait()

    res = pl.pallas_call(body, name=f"{name}_{side.tag}", in_specs=[_ANY] * s_in, out_specs=[_ANY] * len(side.outs),
                         out_shape=list(side.outs), scratch_shapes=_side_scratch(side),
                         input_output_aliases=dict(side.alias))(*side.ins)
    return tuple(res)


def _place():
    x, y, c = lax.axis_index("x"), lax.axis_index("y"), lax.axis_index("c")
    return x, y, c


def _rcopy(src, dst, sems, to):
    return pltpu.make_async_remote_copy(src_ref=src, dst_ref=dst, send_sem=sems[0], recv_sem=sems[1], device_id=to,
                                        device_id_type=pl.DeviceIdType.MESH)


def _gather_own_side(blks):
    def make(sins, souts, sem, lsem):
        x, y, c = _place()
        me = 4 * x + 2 * y + c
        peers = [(x, y, 1 - c), (1 - x, y, c), (x, 1 - y, c), (1 - x, 1 - y, c)]
        remote, local = [], []
        for w, (src, out) in enumerate(zip(sins, souts)):
            local.append(pltpu.make_async_copy(src, out.at[me], lsem(w)))
            for j, (px, py, pc) in enumerate(peers):
                remote.append((_rcopy(src, out.at[me], sem(4 * w + j), (px, py, pc)),
                               _rcopy(src, out.at[4 * px + 2 * py + pc], sem(4 * w + j), (px, py, pc))))
        return remote, local

    return _Side("gown", tuple(blks), tuple(S((8,) + b.shape, b.dtype) for b in blks), (), 4 * len(blks), len(blks),
                 make)


def _gather_pass_side(bufs):
    def make(sins, souts, sem, lsem):
        x, y, c = _place()
        chips = [(1 - x, y), (x, 1 - y), (1 - x, 1 - y)]
        remote = []
        for w, out in enumerate(souts):
            for j, (px, py) in enumerate(chips):
                mine, theirs = out.at[4 * px + 2 * py + c], out.at[4 * px + 2 * py + 1 - c]
                remote.append((_rcopy(mine, mine, sem(3 * w + j), (x, y, 1 - c)),
                               _rcopy(mine, theirs, sem(3 * w + j), (x, y, 1 - c))))
        return remote, []

    return _Side("gpass", tuple(bufs), tuple(S(b.shape, b.dtype) for b in bufs), tuple((i, i) for i in range(len(bufs))),
                 3 * len(bufs), 0, make)


def _swap_side(gbs):
    def make(sins, souts, sem, lsem):
        x, y, c = _place()
        remote = []
        for w, (g, recv) in enumerate(zip(sins, souts)):
            for s in range(g.shape[0]):
                cp = _rcopy(g.at[s, 1 - c], recv.at[s], sem(4 * w + s), (x, y, 1 - c))
                remote.append((cp, cp))
        return remote, []

    return _Side("swap", tuple(gbs), tuple(S((g.shape[0],) + g.shape[2:], g.dtype) for g in gbs), (),
                 4 * len(gbs), 0, make)


def _xchg_side(ps):
    def make(sins, souts, sem, lsem):
        x, y, c = _place()
        peers = [(1 - x, y), (x, 1 - y), (1 - x, 1 - y)]
        remote = []
        for w, (p, recv) in enumerate(zip(sins, souts)):
            for k, (px, py) in enumerate(peers):
                cp = _rcopy(p.at[2 * px + py], recv.at[k], sem(3 * w + k), (px, py, c))
                remote.append((cp, cp))
        return remote, []

    return _Side("xchg", tuple(ps), tuple(S((3,) + p.shape[1:], p.dtype) for p in ps), (), 3 * len(ps), 0, make)


def _share_side(t2s):
    def make(sins, souts, sem, lsem):
        x, y, c = _place()
        remote = []
        for w, out in enumerate(souts):
            cp = _rcopy(out.at[c], out.at[c], sem(w), (x, y, 1 - c))
            remote.append((cp, _rcopy(out.at[c], out.at[1 - c], sem(w), (x, y, 1 - c))))
        return remote, []

    return _Side("share", tuple(t2s), tuple(S(t.shape, t.dtype) for t in t2s), tuple((i, i) for i in range(len(t2s))),
                 len(t2s), 0, make)


_NN, _NT, _TN = ((1,), (0,)), ((1,), (1,)), ((0,), (0,))


def _dg(a, b, dims):
    return lax.dot_general(a.astype(BF16), b.astype(BF16), (dims, ((), ())), preferred_element_type=F32)


@jax.custom_vjp
def _bdot_nn(a, b):
    return _dg(a, b, _NN)


@jax.custom_vjp
def _bdot_nt(a, b):
    return _dg(a, b, _NT)


@jax.custom_vjp
def _bdot_tn(a, b):
    return _dg(a, b, _TN)


_bdot_nn.defvjp(lambda a, b: (_dg(a, b, _NN), (a, b)),
                lambda r, g: (_bdot_nt(g, r[1]).astype(r[0].dtype), _bdot_tn(r[0], g).astype(r[1].dtype)))
_bdot_nt.defvjp(lambda a, b: (_dg(a, b, _NT), (a, b)),
                lambda r, g: (_bdot_nn(g, r[1]).astype(r[0].dtype), _bdot_tn(g, r[0]).astype(r[1].dtype)))
_bdot_tn.defvjp(lambda a, b: (_dg(a, b, _TN), (a, b)),
                lambda r, g: (_bdot_nt(r[1], g).astype(r[0].dtype), _bdot_nn(r[0], g).astype(r[1].dtype)))


def _f32dot(a, b):
    return lax.dot_general(a, b, (_NN, ((), ())), precision=lax.Precision.HIGHEST, preferred_element_type=F32)


@jax.custom_vjp
def _tri_dot(tri, tri_t, x):
    return _f32dot(tri, x)


_tri_dot.defvjp(lambda tri, tri_t, x: (_f32dot(tri, x), (tri, tri_t)),
                lambda r, g: (jnp.zeros_like(r[0]), jnp.zeros_like(r[1]), _f32dot(r[1], g)))


@jax.custom_vjp
def _pair_swap(x):
    lane = lax.broadcasted_iota(jnp.int32, x.shape, x.ndim - 1)
    return jnp.where(lane % 2 == 0, pltpu.roll(x, LANE - 1, x.ndim - 1), pltpu.roll(x, 1, x.ndim - 1))


_pair_swap.defvjp(lambda x: (_pair_swap(x), None), lambda _, g: (_pair_swap(g),))


def R(a, w=None, cb=0, rmap=None):
    return ("r", a, a.shape[1] if w is None else w, cb, rmap)


def C(a):
    return ("c", a)


def _rows(name, fn, n_rows, tm, ins, outs, accs=()):
    n_in, n_out = len(ins), len(outs)
    in_specs, args = [], []
    for e in ins:
        if e[0] == "r":
            _, a, w, cb, rmap = e
            assert w % LANE == 0 or w == a.shape[1]
            if rmap is None:
                in_specs.append(pl.BlockSpec((tm, w), functools.partial(lambda i, cb: (i, cb), cb=cb)))
            else:
                in_specs.append(pl.BlockSpec((tm, w), functools.partial(lambda i, cb, rm: (rm(i), cb), cb=cb, rm=rmap)))
        else:
            a = e[1]
            in_specs.append(pl.BlockSpec(a.shape, functools.partial(lambda i, nd: (0,) * nd, nd=a.ndim)))
        args.append(a)
    out_shape = [S((n_rows, w), dt) for w, dt in outs] + [S(tuple(sh), F32) for sh in accs]
    out_specs = [pl.BlockSpec((tm, w), lambda i: (i, 0)) for w, _ in outs]
    out_specs += [pl.BlockSpec(tuple(sh), functools.partial(lambda i, nd: (0,) * nd, nd=len(sh))) for sh in accs]

    def body(*refs):
        i = pl.program_id(0)
        vals = fn(i, *[r[...] for r in refs[:n_in]])
        assert len(vals) == n_out + len(accs), (name, len(vals))
        for r, v in zip(refs[n_in:n_in + n_out], vals[:n_out]):
            r[...] = v.astype(r.dtype)
        for r, v in zip(refs[n_in + n_out:], vals[n_out:]):
            def init(r=r, v=v):
                r[...] = v.astype(F32)

            def add(r=r, v=v):
                r[...] += v.astype(F32)

            pl.when(i == 0)(init)
            pl.when(i != 0)(add)

    res = pl.pallas_call(body, name=name, grid=(n_rows // tm,), in_specs=in_specs, out_specs=out_specs,
                         out_shape=out_shape, compiler_params=_cparams(("arbitrary",)))(*args)
    return res


def _div_tile(n, cap, mult):
    if n <= cap:
        return n
    best = None
    for t in range(mult, cap + 1, mult):
        if n % t == 0:
            best = t
    assert best is not None, (n, cap, mult)
    return best


def _matmul(name, a, b, form, out_dtype, b_shards=1, out_shards=1, caps=(1088, 1408, 2048), side=None):
    if form == "tn":
        K, M = a.shape
    else:
        M, K = a.shape
    if form == "nn":
        N = b.shape[-1] * b_shards
    elif form == "nt":
        N = b.shape[-2]
    else:
        N = b.shape[1]
    n_per = N // (b_shards if form == "nn" else out_shards)
    k_per = K // (b_shards if form == "nt" else 1)
    tm = _div_tile(M, caps[0], 16 if form != "tn" else LANE)
    tn = _div_tile(n_per, caps[1], LANE)
    tk = _div_tile(k_per, caps[2], LANE if form != "tn" else 16)
    nk = K // tk
    grid = (M // tm, N // tn, nk)
    nps, kps = n_per // tn, k_per // tk

    if form == "tn":
        a_spec = pl.BlockSpec((tk, tm), lambda i, j, k: (k, i))
    else:
        a_spec = pl.BlockSpec((tm, tk), lambda i, j, k: (i, k))
    if form == "nn":
        if b_shards > 1:
            b_spec = pl.BlockSpec((None, tk, tn), lambda i, j, k: (j // nps, k, j % nps))
        else:
            b_spec = pl.BlockSpec((tk, tn), lambda i, j, k: (k, j))
    elif form == "nt":
        if b_shards > 1:
            b_spec = pl.BlockSpec((None, tn, tk), lambda i, j, k: (k // kps, j, k % kps))
        else:
            b_spec = pl.BlockSpec((tn, tk), lambda i, j, k: (j, k))
    else:
        b_spec = pl.BlockSpec((tk, tn), lambda i, j, k: (k, j))
    if out_shards > 1:
        o_spec = pl.BlockSpec((None, tm, tn), lambda i, j, k: (j // nps, i, j % nps))
        o_shape = S((out_shards, M, n_per), out_dtype)
    else:
        o_spec = pl.BlockSpec((tm, tn), lambda i, j, k: (i, j))
        o_shape = S((M, N), out_dtype)
    dims = {"nn": _NN, "nt": _NT, "tn": _TN}[form]

    def body(a_ref, b_ref, o_ref, acc_ref=None):
        k = pl.program_id(2)
        p = _dg(a_ref[...], b_ref[...], dims)
        if nk == 1:
            o_ref[...] = p.astype(o_ref.dtype)
        else:
            @pl.when(k == 0)
            def _():
                acc_ref[...] = p

            @pl.when(jnp.logical_and(k > 0, k < nk - 1))
            def _():
                acc_ref[...] += p

            @pl.when(k == nk - 1)
            def _():
                o_ref[...] = (acc_ref[...] + p).astype(o_ref.dtype)

    out, landed = _pcall(body, name=name, grid=grid, in_specs=[a_spec, b_spec], out_specs=o_spec, out_shape=o_shape,
                         scratch_shapes=[pltpu.VMEM((tm, tn), F32)] if nk > 1 else [],
                         sem=("parallel", "parallel", "arbitrary"),
                         args=(a, b), side=side)
    return out if side is None else (out, landed)


def _rms(x, g):
    return x * lax.rsqrt(jnp.mean(x * x, axis=-1, keepdims=True) + EPS) * g


def _sel2(mm, is_ctx):
    return jnp.where(is_ctx, mm[0:1], mm[1:2])


def _put2(v, is_ctx):
    row = lax.broadcasted_iota(jnp.int32, (2, v.shape[-1]), 0)
    return jnp.where(row == jnp.where(is_ctx, 0, 1), v, 0.0)


def _normmod(x, g, sh, sc):
    return _rms(x, g) * (1.0 + sc) + sh


def _colsum(v):
    return jnp.sum(v, axis=0, keepdims=True)


def _normmod_fwd(dm, name, x, g, sh2, sc2):
    nc = dm.ctx_len // dm.row_tile

    def fn(i, x, g, sh2, sc2):
        is_ctx = i < nc
        return (_normmod(x, g, _sel2(sh2, is_ctx), _sel2(sc2, is_ctx)),)

    return _rows(name, fn, dm.n_tok, dm.row_tile, [R(x), C(g), C(sh2), C(sc2)], [(dm.d_model, BF16)])[0]


def _resnorm_fwd(dm, name, x, y, gate2, g, sh2, sc2):
    nc = dm.ctx_len // dm.row_tile

    def fn(i, x, y, gate2, g, sh2, sc2):
        is_ctx = i < nc
        x1 = x + _sel2(gate2, is_ctx) * y
        return x1, _normmod(x1, g, _sel2(sh2, is_ctx), _sel2(sc2, is_ctx))

    return _rows(name, fn, dm.n_tok, dm.row_tile, [R(x), R(y), C(gate2), C(g), C(sh2), C(sc2)],
                 [(dm.d_model, F32), (dm.d_model, BF16)])


def _res_fwd(dm, name, x, y, gate2):
    nc = dm.ctx_len // dm.row_tile

    def fn(i, x, y, gate2):
        return (x + _sel2(gate2, i < nc) * y,)

    return _rows(name, fn, dm.n_tok, dm.row_tile, [R(x), R(y), C(gate2)], [(dm.d_model, F32)])[0]


def _gate_bwd(dm, name, dx, y, gate2):
    nc = dm.ctx_len // dm.row_tile

    def fn(i, dx, y, gate2):
        is_ctx = i < nc
        return dx * _sel2(gate2, is_ctx), _put2(_colsum(dx * y), is_ctx)

    return _rows(name, fn, dm.n_tok, dm.row_tile, [R(dx), R(y), C(gate2)], [(dm.d_model, BF16)], [(2, dm.d_model)])


def _normmod_bwd(dm, name, x, dh, dres, g, sh2, sc2):
    nc = dm.ctx_len // dm.row_tile

    def fn(i, x, dh, dres, g, sh2, sc2):
        is_ctx = i < nc
        sh, sc = _sel2(sh2, is_ctx), _sel2(sc2, is_ctx)
        _, vjp = jax.vjp(_normmod, x, g, sh, sc)
        dx, dg, dsh, dsc = vjp(dh)
        return dres + dx, dg, _put2(dsh, is_ctx), _put2(dsc, is_ctx)

    return _rows(name, fn, dm.n_tok, dm.row_tile, [R(x), R(dh), R(dres), C(g), C(sh2), C(sc2)],
                 [(dm.d_model, F32)], [(1, dm.d_model), (2, dm.d_model), (2, dm.d_model)])


def _rope_tables(dm):
    t = jnp.arange(dm.seq)
    row = (t // dm.grid_w).astype(F32)
    col = (t % dm.grid_w).astype(F32)
    n_freq = LANE // 4
    inv = ROPE_THETA ** (-jnp.arange(n_freq, dtype=F32) / n_freq)
    ang = jnp.concatenate([row[:, None] * inv, col[:, None] * inv], axis=-1)
    cos, sin = jnp.cos(ang), jnp.sin(ang)
    ct = jnp.repeat(cos, 2, axis=-1)
    st = jnp.stack([-sin, sin], axis=-1).reshape(dm.seq, LANE)
    ct = jnp.concatenate([jnp.ones((dm.ctx_len, LANE), F32), ct], axis=0)
    st = jnp.concatenate([jnp.zeros((dm.ctx_len, LANE), F32), st], axis=0)
    return ct, st


def _qk_fn(n_q, n_k):
    def fn(aq, ak, ct, st, qg, kg):
        def head(x, g):
            y = _rms(x, g)
            return y * ct + _pair_swap(y) * st

        q = jnp.concatenate([head(aq[:, h * LANE:(h + 1) * LANE], qg) for h in range(n_q)], axis=1)
        k = jnp.concatenate([head(ak[:, h * LANE:(h + 1) * LANE], kg) for h in range(n_k)], axis=1)
        return q, k

    return fn


def _qk_fwd(dm, name, proj, ct, st, qg, kg):
    f = _qk_fn(dm.attn_heads, dm.kv_heads)
    o = dm.in_offs
    return _rows(name, lambda i, *a: f(*a), dm.n_tok, dm.row_tile,
                 [R(proj, dm.q_w, o[0] // dm.q_w), R(proj, dm.kv_w, o[1] // dm.kv_w), R(ct), R(st), C(qg), C(kg)],
                 [(dm.q_w, BF16), (dm.kv_w, BF16)])


def _qk_bwd(dm, name, proj, ct, st, qg, kg, dq, dk):
    f = _qk_fn(dm.attn_heads, dm.kv_heads)
    o = dm.in_offs

    def fn(i, aq, ak, ct, st, qg, kg, dq, dk):
        _, vjp = jax.vjp(lambda aq, ak, qg, kg: f(aq, ak, ct, st, qg, kg), aq, ak, qg, kg)
        return vjp((dq, dk))

    return _rows(name, fn, dm.n_tok, dm.row_tile,
                 [R(proj, dm.q_w, o[0] // dm.q_w), R(proj, dm.kv_w, o[1] // dm.kv_w), R(ct), R(st), C(qg), C(kg),
                  R(dq), R(dk)],
                 [(dm.q_w, BF16), (dm.kv_w, BF16)], [(1, LANE), (1, LANE)])


def _attn_probs(q, k, i, nc, ctx_len, n_tok):
    s = _dg(q, k, _NT) * (LANE ** -0.5)
    col = lax.broadcasted_iota(jnp.int32, (1, n_tok), 1)
    s = s + jnp.where(col < jnp.where(i < nc, ctx_len, n_tok), 0.0, -1e30)
    e = jnp.exp(s - jnp.max(s, axis=-1, keepdims=True))
    return e, 1.0 / jnp.sum(e, axis=-1, keepdims=True)


def _attn_fwd(dm, name, qh, kh, proj, side=None):
    tq, n = dm.row_tile, dm.n_tok
    nc, grp = dm.ctx_len // tq, dm.attn_heads // dm.kv_heads
    v_cb = dm.in_offs[2] // LANE

    def body(q_ref, k_ref, v_ref, o_ref):
        e, inv = _attn_probs(q_ref[...], k_ref[...], pl.program_id(1), nc, dm.ctx_len, n)
        o_ref[...] = (_dg(e, v_ref[...], _NN) * inv).astype(o_ref.dtype)

    out, landed = _pcall(
        body, name=name, grid=(dm.attn_heads, n // tq),
        in_specs=[pl.BlockSpec((tq, LANE), lambda h, i: (i, h)),
                  pl.BlockSpec((n, LANE), lambda h, i: (0, h // grp)),
                  pl.BlockSpec((n, LANE), lambda h, i: (0, v_cb + h // grp))],
        out_specs=pl.BlockSpec((tq, LANE), lambda h, i: (i, h)),
        out_shape=S((n, dm.q_w), BF16), sem=("parallel", "parallel"), args=(qh, kh, proj), side=side)
    return out if side is None else (out, landed)


def _attn_bwd(dm, name, qh, kh, proj, attn, dmix, side=None):
    tq, n = dm.row_tile, dm.n_tok
    nc, grp = dm.ctx_len // tq, dm.attn_heads // dm.kv_heads
    v_cb = dm.in_offs[2] // LANE

    def body(q_ref, k_ref, v_ref, o_ref, do_ref, dq_ref, dk_ref, dv_ref):
        first = jnp.logical_and(pl.program_id(1) == 0, pl.program_id(2) == 0)
        q, k, v, do = q_ref[...], k_ref[...], v_ref[...], do_ref[...]
        e, inv = _attn_probs(q, k, pl.program_id(2), nc, dm.ctx_len, n)
        delta = jnp.sum(do * o_ref[...].astype(F32), axis=-1, keepdims=True)
        ds = e * ((_dg(do, v, _NT) - delta) * (inv * (LANE ** -0.5)))
        dq_ref[...] = _dg(ds, k, _NN)
        dk = _dg(ds, q, _TN)
        dv = _dg(e, do * inv, _TN)

        @pl.when(first)
        def _():
            dk_ref[...] = dk
            dv_ref[...] = dv

        @pl.when(jnp.logical_not(first))
        def _():
            dk_ref[...] += dk
            dv_ref[...] += dv

    outs, landed = _pcall(
        body, name=name, grid=(dm.kv_heads, grp, n // tq),
        in_specs=[pl.BlockSpec((tq, LANE), lambda g, hh, i: (i, g * grp + hh)),
                  pl.BlockSpec((n, LANE), lambda g, hh, i: (0, g)),
                  pl.BlockSpec((n, LANE), lambda g, hh, i: (0, v_cb + g)),
                  pl.BlockSpec((tq, LANE), lambda g, hh, i: (i, g * grp + hh)),
                  pl.BlockSpec((tq, LANE), lambda g, hh, i: (i, g * grp + hh))],
        out_specs=[pl.BlockSpec((tq, LANE), lambda g, hh, i: (i, g * grp + hh)),
                   pl.BlockSpec((n, LANE), lambda g, hh, i: (0, g)),
                   pl.BlockSpec((n, LANE), lambda g, hh, i: (0, g))],
        out_shape=[S((n, dm.q_w), F32), S((n, dm.kv_w), F32), S((n, dm.kv_w), F32)],
        sem=("parallel", "arbitrary", "arbitrary"), args=(qh, kh, proj, attn, dmix), side=side)
    return outs if side is None else (outs, landed)


def _hg_chunk(d, st, qraw, fraw, v, lb):
    c = qraw.shape[0]
    sig = jax.nn.sigmoid(fraw)
    f = lb + (1.0 - lb) * sig
    logf = jnp.log(jnp.maximum(f, F_MIN))
    k = (1.0 - lb) * jax.nn.sigmoid(-fraw)
    q = qraw * jax.nn.sigmoid(qraw)
    r_i = lax.broadcasted_iota(jnp.int32, (c, c), 0)
    c_i = lax.broadcasted_iota(jnp.int32, (c, c), 1)
    sgn = 1 - 2 * d
    tri = ((r_i - c_i) * sgn >= 0).astype(F32)
    tri_t = ((c_i - r_i) * sgn >= 0).astype(F32)
    b = _tri_dot(tri, tri_t, logf)
    b_last = jnp.sum(logf, axis=0, keepdims=True)
    trow = lax.broadcasted_iota(jnp.int32, (c, 1), 0)
    o = _bdot_nt(q * jnp.exp(b), st)
    for s in range(c):
        m = (trow - s) * sgn >= 0
        e = jnp.exp(jnp.where(m, b - b[s:s + 1], 0.0))
        w = jnp.where(m, q * k[s:s + 1] * e, 0.0)
        o = o + jnp.sum(w, axis=-1, keepdims=True) * v[s:s + 1]
    st_new = st * jnp.exp(b_last) + _bdot_tn(v, k * jnp.exp(b_last - b))
    return o, st_new


def _hg_blk(dm, tb):
    nbc, nbl = dm.ctx_len // tb, dm.seq // tb

    def blk(d, j):
        rev = jnp.where(j < nbc, nbc - 1 - j, 2 * nbc + nbl - 1 - j)
        return jnp.where(d == 0, j, rev)

    return blk, nbc + nbl


def _hgrn_fwd(dm, name, proj, lb, side=None):
    tb, n, hh, ck, hw = dm.row_tile, dm.n_tok, dm.hg_heads, dm.hg_chunk, dm.hg_w
    blk, nblk = _hg_blk(dm, tb)
    ncb = tb // ck
    o = dm.in_offs
    q_cb, f_cb, v_cb = o[3] // hw, o[4] // hw, o[6] // hw

    def body(q0, f0, v0, q1, f1, v1, lb_ref, o0, o1, h0, h1, st_ref):
        @pl.when(pl.program_id(0) == 0)
        def _():
            st_ref[...] = jnp.zeros_like(st_ref)

        lbv = lb_ref[...]

        def chunk(ci, carry):
            for d, (q_ref, f_ref, v_ref, o_ref, hist_ref) in enumerate(((q0, f0, v0, o0, h0), (q1, f1, v1, o1, h1))):
                c = ci if d == 0 else ncb - 1 - ci
                rows = pl.ds(pl.multiple_of(c * ck, ck), ck)
                for h in range(hh):
                    cols = slice(h * LANE, (h + 1) * LANE)
                    st = st_ref[d, h]
                    hist_ref[h, c] = st
                    oc, stn = _hg_chunk(d, st, q_ref[rows, cols], f_ref[rows, cols], v_ref[rows, cols],
                                        lbv[d][:, cols])
                    o_ref[rows, cols] = oc
                    st_ref[d, h] = stn
            return carry

        lax.fori_loop(0, ncb, chunk, 0)

    def ins(d):
        return [pl.BlockSpec((tb, hw), lambda j: (blk(d, j), q_cb)),
                pl.BlockSpec((tb, hw), lambda j: (blk(d, j), f_cb + d)),
                pl.BlockSpec((tb, hw), lambda j: (blk(d, j), v_cb))]

    outs, landed = _pcall(
        body, name=name, grid=(nblk,),
        in_specs=ins(0) + ins(1) + [pl.BlockSpec((2, 1, hw), lambda j: (0, 0, 0))],
        out_specs=[pl.BlockSpec((tb, hw), lambda j: (blk(0, j), 0)), pl.BlockSpec((tb, hw), lambda j: (blk(1, j), 0)),
                   pl.BlockSpec((hh, ncb, LANE, LANE), lambda j: (0, blk(0, j), 0, 0)),
                   pl.BlockSpec((hh, ncb, LANE, LANE), lambda j: (0, blk(1, j), 0, 0))],
        out_shape=[S((n, hw), F32)] * 2 + [S((hh, n // ck, LANE, LANE), F32)] * 2,
        scratch_shapes=[pltpu.VMEM((2, hh, LANE, LANE), F32)],
        sem=("arbitrary",), args=(proj,) * 6 + (lb,), side=side)
    res = ((outs[0], outs[1]), (outs[2], outs[3]))
    return res if side is None else (res, landed)


def _hgrn_bwd(dm, name, proj, lb, hist, do, side=None):
    tb, n, hh, ck, hw = dm.row_tile, dm.n_tok, dm.hg_heads, dm.hg_chunk, dm.hg_w
    blk, nblk = _hg_blk(dm, tb)
    ncb = tb // ck
    o = dm.in_offs
    q_cb, f_cb, v_cb = o[3] // hw, o[4] // hw, o[6] // hw

    def rblk(d, j):
        return blk(d, nblk - 1 - j)

    def body(q0, f0, v0, h0, do0, q1, f1, v1, h1, do1, lb_ref, dq0, df0, dv0, dq1, df1, dv1, dlb_ref, dst_ref):
        @pl.when(pl.program_id(0) == 0)
        def _():
            dst_ref[...] = jnp.zeros_like(dst_ref)
            dlb_ref[...] = jnp.zeros_like(dlb_ref)

        lbv = lb_ref[...]
        per_dir = ((q0, f0, v0, h0, do0, dq0, df0, dv0), (q1, f1, v1, h1, do1, dq1, df1, dv1))

        def chunk(ci, carry):
            for d, (q_ref, f_ref, v_ref, hist_ref, do_ref, dq_ref, df_ref, dv_ref) in enumerate(per_dir):
                c = ncb - 1 - ci if d == 0 else ci
                rows = pl.ds(pl.multiple_of(c * ck, ck), ck)
                for h in range(hh):
                    cols = slice(h * LANE, (h + 1) * LANE)
                    _, vjp = jax.vjp(functools.partial(_hg_chunk, d), hist_ref[h, c], q_ref[rows, cols],
                                     f_ref[rows, cols], v_ref[rows, cols], lbv[d][:, cols])
                    dst, dq, df, dv, dlb = vjp((do_ref[rows, cols], dst_ref[d, h]))
                    dq_ref[rows, cols] = dq
                    df_ref[rows, cols] = df
                    dv_ref[rows, cols] = dv
                    dlb_ref[d, :, cols] += dlb
                    dst_ref[d, h] = dst
            return carry

        lax.fori_loop(0, ncb, chunk, 0)

    def ins(d):
        return [pl.BlockSpec((tb, hw), lambda j: (rblk(d, j), q_cb)),
                pl.BlockSpec((tb, hw), lambda j: (rblk(d, j), f_cb + d)),
                pl.BlockSpec((tb, hw), lambda j: (rblk(d, j), v_cb)),
                pl.BlockSpec((hh, ncb, LANE, LANE), lambda j: (0, rblk(d, j), 0, 0)),
                pl.BlockSpec((tb, hw), lambda j: (rblk(d, j), 0))]

    rows_of = lambda d: pl.BlockSpec((tb, hw), lambda j: (rblk(d, j), 0))
    outs, landed = _pcall(
        body, name=name, grid=(nblk,),
        in_specs=ins(0) + ins(1) + [pl.BlockSpec((2, 1, hw), lambda j: (0, 0, 0))],
        out_specs=[rows_of(0)] * 3 + [rows_of(1)] * 3 + [pl.BlockSpec((2, 1, hw), lambda j: (0, 0, 0))],
        out_shape=[S((n, hw), F32)] * 6 + [S((2, 1, hw), F32)],
        scratch_shapes=[pltpu.VMEM((2, hh, LANE, LANE), F32)],
        sem=("arbitrary",),
        args=(proj, proj, proj, hist[0], do, proj, proj, proj, hist[1], do, lb), side=side)
    res = ((outs[0], outs[3]), (outs[1], outs[4]), (outs[2], outs[5]), outs[6])
    return res if side is None else (res, landed)


def _hgc_fn(n_h):
    def fn(o0, o1, gt, g):
        osum = o0 + o1
        y = jnp.concatenate([_rms(osum[:, h * LANE:(h + 1) * LANE], g) for h in range(n_h)], axis=1)
        return y * (gt * jax.nn.sigmoid(gt))

    return fn


def _hgc_fwd(dm, name, o_dir, proj, g):
    f = _hgc_fn(dm.hg_heads)
    cb = dm.in_offs[7] // dm.hg_w
    return _rows(name, lambda i, *a: (f(*a),), dm.n_tok, dm.row_tile,
                 [R(o_dir[0]), R(o_dir[1]), R(proj, dm.hg_w, cb), C(g)], [(dm.hg_w, BF16)])[0]


def _hgc_bwd(dm, name, o_dir, proj, g, dmix):
    f = _hgc_fn(dm.hg_heads)
    cb = dm.in_offs[7] // dm.hg_w

    def fn(i, o0, o1, gt, g, dy):
        _, vjp = jax.vjp(f, o0, o1, gt, g)
        do, _, dgt, dg = vjp(dy)
        return do, dgt, dg

    return _rows(name, fn, dm.n_tok, dm.row_tile,
                 [R(o_dir[0]), R(o_dir[1]), R(proj, dm.hg_w, cb), C(g), R(dmix, dm.hg_w, dm.q_w // dm.hg_w)],
                 [(dm.hg_w, F32), (dm.hg_w, BF16)], [(1, LANE)])


def _sg_fn(n_g):
    def fn(su, sv, ng, w, bcol):
        u = jax.nn.gelu(su)
        gv = jax.nn.gelu(sv)
        outs = []
        for g in range(n_g):
            sl = slice(g * LANE, (g + 1) * LANE)
            vn = _rms(gv[:, sl], ng[:, sl])
            outs.append(_bdot_nn(w[g], vn) + bcol[g])
        return u * jnp.concatenate(outs, axis=1)

    return fn


def _sg_fwd(dm, name, proj, ng, w, bcol):
    f = _sg_fn(dm.sg_groups)
    o = dm.in_offs
    return _rows(name, lambda i, *a: (f(*a),), dm.n_tok, LANE,
                 [R(proj, dm.sg_w, o[8] // dm.sg_w), R(proj, dm.sg_w, o[9] // dm.sg_w), C(ng), C(w), C(bcol)],
                 [(dm.sg_w, BF16)])[0]


def _sg_bwd(dm, name, proj, ng, w, bcol, dmix):
    f = _sg_fn(dm.sg_groups)
    o = dm.in_offs

    def fn(i, su, sv, ng, w, bcol, dy):
        _, vjp = jax.vjp(f, su, sv, ng, w, bcol)
        return vjp(dy)

    return _rows(name, fn, dm.n_tok, LANE,
                 [R(proj, dm.sg_w, o[8] // dm.sg_w), R(proj, dm.sg_w, o[9] // dm.sg_w), C(ng), C(w), C(bcol),
                  R(dmix, dm.sg_w, (dm.q_w + dm.hg_w) // dm.sg_w)],
                 [(dm.sg_w, BF16), (dm.sg_w, BF16)],
                 [(1, dm.sg_w), (dm.sg_groups, LANE, LANE), (dm.sg_groups, LANE, 1)])


def _conv3(x, prev, nxt, w0, w1, w2, zero_prev, zero_next):
    tm = x.shape[0]
    x = x.astype(F32)
    row = lax.broadcasted_iota(jnp.int32, (tm, 1), 0)
    up = jnp.where(zero_prev, 0.0, prev[prev.shape[0] - 1:].astype(F32))
    dn = jnp.where(zero_next, 0.0, nxt[0:1].astype(F32))
    x_m1 = jnp.where(row == 0, up, pltpu.roll(x, 1, 0))
    x_p1 = jnp.where(row == tm - 1, dn, pltpu.roll(x, tm - 1, 0))
    return w0 * x_m1 + w1 * x + w2 * x_p1, x_m1, x_p1


def _conv_edges(dm, tm):
    nbc, nb = dm.ctx_len // tm, dm.n_tok // tm

    def edges(i):
        return (jnp.logical_or(i == 0, i == nbc), jnp.logical_or(i == nbc - 1, i == nb - 1))

    return edges


def _halo_rows(dtype):
    return 16 if dtype == BF16 else 8


def _halo_specs(tm, tn, n_rows, hr, pick):
    last, per = n_rows // hr - 1, tm // hr
    return [pl.BlockSpec((tm, tn), lambda *ids: pick(*ids)),
            pl.BlockSpec((hr, tn), lambda *ids: (jnp.maximum(pick(*ids)[0] * per - 1, 0), pick(*ids)[1])),
            pl.BlockSpec((hr, tn), lambda *ids: (jnp.minimum((pick(*ids)[0] + 1) * per, last), pick(*ids)[1]))]


def _conv_fwd(dm, name, up, cw, cb):
    n, ff, tm = dm.n_tok, dm.d_ff, dm.row_tile
    tn = _div_tile(ff, 1408, LANE)
    nj = ff // tn
    edges = _conv_edges(dm, tm)

    def body(g_ref, gp_ref, gn_ref, v_ref, vp_ref, vn_ref, wg_ref, wv_ref, bg_ref, bv_ref, a_ref):
        zp, zn = edges(pl.program_id(0))
        wg, wv = wg_ref[...], wv_ref[...]
        yg = _conv3(g_ref[...], gp_ref[...], gn_ref[...], wg[0:1], wg[1:2], wg[2:3], zp, zn)[0] + bg_ref[...]
        yv = _conv3(v_ref[...], vp_ref[...], vn_ref[...], wv[0:1], wv[1:2], wv[2:3], zp, zn)[0] + bv_ref[...]
        a_ref[...] = (yg * jax.nn.sigmoid(yg) * yv).astype(a_ref.dtype)

    small = lambda off: pl.BlockSpec((3, tn), lambda i, j: (0, j + off))
    bias = lambda off: pl.BlockSpec((1, tn), lambda i, j: (0, j + off))
    return pl.pallas_call(
        body, name=name, grid=(n // tm, nj),
        in_specs=_halo_specs(tm, tn, n, _halo_rows(up.dtype), lambda i, j: (i, j))
        + _halo_specs(tm, tn, n, _halo_rows(up.dtype), lambda i, j: (i, j + nj))
        + [small(0), small(nj), bias(0), bias(nj)],
        out_specs=pl.BlockSpec((tm, tn), lambda i, j: (i, j)), out_shape=S((n, ff), BF16),
        compiler_params=_cparams(("parallel", "parallel")))(up, up, up, up, up, up, cw, cw, cb, cb)


def _conv_bwd_dy(dm, name, up, cw, cb, da, side=None):
    n, ff, tm = dm.n_tok, dm.d_ff, dm.row_tile
    tn = _div_tile(ff, 1408, LANE)
    nj = ff // tn
    edges = _conv_edges(dm, tm)

    def body(g_ref, gp_ref, gn_ref, v_ref, vp_ref, vn_ref, wg_ref, wv_ref, bg_ref, bv_ref, da_ref,
             dyg_ref, dyv_ref, dwg_ref, dwv_ref, dbg_ref, dbv_ref):
        i = pl.program_id(1)
        zp, zn = edges(i)
        wg, wv = wg_ref[...], wv_ref[...]
        g, v = g_ref[...].astype(F32), v_ref[...].astype(F32)
        cg, g_m1, g_p1 = _conv3(g, gp_ref[...], gn_ref[...], wg[0:1], wg[1:2], wg[2:3], zp, zn)
        cv, v_m1, v_p1 = _conv3(v, vp_ref[...], vn_ref[...], wv[0:1], wv[1:2], wv[2:3], zp, zn)
        yg, yv = cg + bg_ref[...], cv + bv_ref[...]
        sg = jax.nn.sigmoid(yg)
        da = da_ref[...]
        dyg = da * yv * (sg * (1.0 + yg * (1.0 - sg)))
        dyv = da * (yg * sg)
        dyg_ref[...] = dyg.astype(dyg_ref.dtype)
        dyv_ref[...] = dyv.astype(dyv_ref.dtype)
        row = lax.broadcasted_iota(jnp.int32, (3, tn), 0)

        def stack3(dy, a, b, c):
            return jnp.where(row == 0, _colsum(dy * a), jnp.where(row == 1, _colsum(dy * b), _colsum(dy * c)))

        upd = [(dwg_ref, stack3(dyg, g_m1, g, g_p1)), (dwv_ref, stack3(dyv, v_m1, v, v_p1)),
               (dbg_ref, _colsum(dyg)), (dbv_ref, _colsum(dyv))]

        @pl.when(i == 0)
        def _():
            for r, val in upd:
                r[...] = val

        @pl.when(i != 0)
        def _():
            for r, val in upd:
                r[...] += val

    hs = lambda off: _halo_specs(tm, tn, n, _halo_rows(up.dtype), lambda j, i: (i, j + off))
    small = lambda off: pl.BlockSpec((3, tn), lambda j, i: (0, j + off))
    bias = lambda off: pl.BlockSpec((1, tn), lambda j, i: (0, j + off))
    blk = pl.BlockSpec((tm, tn), lambda j, i: (i, j))
    outs, landed = _pcall(
        body, name=name, grid=(nj, n // tm),
        in_specs=hs(0) + hs(nj) + [small(0), small(nj), bias(0), bias(nj), blk],
        out_specs=[blk, blk, small(0), small(0), bias(0), bias(0)],
        out_shape=[S((n, ff), BF16), S((n, ff), BF16), S((3, ff), F32), S((3, ff), F32), S((1, ff), F32),
                   S((1, ff), F32)],
        sem=("parallel", "arbitrary"), args=(up, up, up, up, up, up, cw, cw, cb, cb, da), side=side)
    return outs if side is None else (outs, landed)


def _conv_bwd_dx(dm, name, dyg, dyv, cw):
    n, ff, tm = dm.n_tok, dm.d_ff, dm.row_tile
    tn = _div_tile(ff, 1408, LANE)
    nj = ff // tn
    edges = _conv_edges(dm, tm)

    def body(g_ref, gp_ref, gn_ref, v_ref, vp_ref, vn_ref, wg_ref, wv_ref, o_ref):
        zp, zn = edges(pl.program_id(0))
        half = pl.program_id(1) // nj
        x = jnp.where(half == 0, g_ref[...], v_ref[...])
        xp = jnp.where(half == 0, gp_ref[...], vp_ref[...])
        xn = jnp.where(half == 0, gn_ref[...], vn_ref[...])
        w = jnp.where(half == 0, wg_ref[...], wv_ref[...])
        o_ref[...] = _conv3(x, xp, xn, w[2:3], w[1:2], w[0:1], zp, zn)[0].astype(o_ref.dtype)

    hr = _halo_rows(dyg.dtype)
    g_specs = _halo_specs(tm, tn, n, hr, lambda i, j: (i, jnp.minimum(j, nj - 1)))
    v_specs = _halo_specs(tm, tn, n, hr, lambda i, j: (i, jnp.maximum(j - nj, 0)))
    return pl.pallas_call(
        body, name=name, grid=(n // tm, 2 * nj),
        in_specs=g_specs + v_specs + [pl.BlockSpec((3, tn), lambda i, j: (0, jnp.minimum(j, nj - 1))),
                                      pl.BlockSpec((3, tn), lambda i, j: (0, nj + jnp.maximum(j - nj, 0)))],
        out_specs=pl.BlockSpec((tm, tn), lambda i, j: (i, j)), out_shape=S((n, 2 * ff), BF16),
        compiler_params=_cparams(("parallel", "parallel")))(dyg, dyg, dyg, dyv, dyv, dyv, cw, cw)


def _loss_head(dm, name, x, tgt, g):
    tm = dm.row_tile
    nc = dm.ctx_len // tm

    def fn(i, x, t, g):
        def f(x, g):
            err = _rms(x, g) - t
            return 0.5 * jnp.sum(jnp.mean(err * err, axis=-1, keepdims=True), axis=0, keepdims=True)

        loss, vjp = jax.vjp(f, x, g)
        dx, dg = vjp(jnp.ones((1, 1), F32))
        live = i >= nc
        return (jnp.where(live, dx, 0.0), jnp.where(live, jnp.broadcast_to(loss, (1, LANE)), 0.0),
                jnp.where(live, dg, 0.0))

    return _rows(name, fn, dm.n_tok, tm, [R(x), R(tgt, rmap=lambda i: jnp.maximum(i - nc, 0)), C(g)],
                 [(dm.d_model, F32)], [(1, LANE), (1, dm.d_model)])


def _dproj_assemble(dm, name, d_aq, d_ak, dv, dq_dir, df_dir, dv_dir, d_hgt, d_su, d_sv):
    def fn(i, d_aq, d_ak, dv, q0, q1, f0, f1, v0, v1, d_hgt, d_su, d_sv):
        parts = [d_aq, d_ak, dv, q0 + q1, f0, f1, v0 + v1, d_hgt, d_su, d_sv]
        return (jnp.concatenate([p.astype(F32) for p in parts], axis=1),)

    ins = [R(d_aq), R(d_ak), R(dv), R(dq_dir[0]), R(dq_dir[1]), R(df_dir[0]), R(df_dir[1]), R(dv_dir[0]),
           R(dv_dir[1]), R(d_hgt), R(d_su), R(d_sv)]
    return _rows(name, fn, dm.n_tok, dm.row_tile, ins, [(dm.in_cols, BF16)])[0]


def _layer_fwd(dm, l, x, h, mods, wl, tabs, blocks=None):
    ct, st = tabs
    d = dm.d_model
    w_in_next = None
    proj = _matmul("proj", h, wl["w_in"], "nn", F32, b_shards=4)
    qh, kh = _qk_fwd(dm, "qk", proj, ct, st, wl["q_g"], wl["k_g"])
    if blocks is None:
        attn = _attn_fwd(dm, "attn", qh, kh, proj)
        o_dir, hist = _hgrn_fwd(dm, "hgrn", proj, wl["lb"])
    else:
        attn, g1 = _attn_fwd(dm, "attn", qh, kh, proj, side=_gather_own_side([blocks["w_up"], blocks["w_out"]]))
        second = [blocks["w_down"]] + ([blocks["w_in_next"]] if "w_in_next" in blocks else [])
        sides = [_gather_own_side(second), _gather_pass_side(list(g1))]
        (o_dir, hist), landed = _hgrn_fwd(dm, "hgrn", proj, wl["lb"], side=_merge_sides(*sides))
        g2, g1 = _split_outs(sides, landed)
        wl = dict(wl, w_up=g1[0].reshape(4, d, -1), w_out=g1[1].reshape(dm.d_mix, d))
    hg = _hgc_fwd(dm, "hgc", o_dir, proj, wl["hg_g"])
    sg = _sg_fwd(dm, "sg", proj, wl["sg_g"], wl["sg_w"], wl["sg_bcol"])
    mix = jnp.concatenate([attn, hg, sg], axis=1)
    m = _matmul("out", mix, wl["w_out"], "nn", F32)
    x1, h2 = _resnorm_fwd(dm, "resnorm2", x, m, mods[2], wl["norm2_g"], mods[3], mods[4])
    if blocks is None:
        up = _matmul("up", h2, wl["w_up"], "nn", BF16, b_shards=4)
    else:
        up, g2 = _matmul("up", h2, wl["w_up"], "nn", BF16, b_shards=4, side=_gather_pass_side(list(g2)))
        wl = dict(wl, w_down=g2[0].reshape(dm.d_ff, d))
        if len(g2) > 1:
            w_in_next = g2[1].reshape(4, d, -1)
    a = _conv_fwd(dm, "conv", up, wl["conv_w"], wl["conv_b"])
    f = _matmul("down", a, wl["w_down"], "nn", F32)
    saved = dict(x=x, h=h, proj=proj, qh=qh, kh=kh, attn=attn, o_dir=o_dir, hist=hist, mix=mix, m=m, x1=x1, h2=h2,
                 up=up, a=a, f=f)
    return x1, f, saved, wl, w_in_next


def _blocks42(g):
    return g.reshape(4, 2, -1, g.shape[-1])


def _layer_bwd(dm, l, dx2, sv, mods, wl, tabs, rs=None):
    ct, st = tabs
    g = {}
    df, g["mod5"] = _gate_bwd(dm, "b_gate5", dx2, sv["f"], mods[5])
    da = _matmul("b_da", df, wl["w_down"], "nt", F32)
    g["w_down"] = _matmul("b_wdown", sv["a"], df, "tn", BF16)
    if rs is None:
        dyg, dyv, dwg, dwv, dbg, dbv = _conv_bwd_dy(dm, "b_convdy", sv["up"], wl["conv_w"], wl["conv_b"], da)
    else:
        jobs1 = [((l, "w_down"), _blocks42(g.pop("w_down")))] + rs["pending"]
        rs["pending"] = []
        (dyg, dyv, dwg, dwv, dbg, dbv), recv = _conv_bwd_dy(dm, "b_convdy", sv["up"], wl["conv_w"], wl["conv_b"], da,
                                                             side=_swap_side([gb for _, gb in jobs1]))
        ps1 = [_pair_sum("rs_sum2", gb, r) for (_, gb), r in zip(jobs1, recv)]
    g["conv_w"] = jnp.concatenate([dwg, dwv], axis=1)
    g["conv_b"] = jnp.concatenate([dbg, dbv], axis=1)
    d_up = _conv_bwd_dx(dm, "b_convdx", dyg, dyv, wl["conv_w"])
    if rs is None:
        dh2 = _matmul("b_dh2", d_up, wl["w_up"], "nt", F32, b_shards=4)
    else:
        dh2, recv = _matmul("b_dh2", d_up, wl["w_up"], "nt", F32, b_shards=4, side=_xchg_side(ps1))
        ts1 = [_chip_sum("rs_sum4", p, r) for p, r in zip(ps1, recv)]
    g["w_up"] = _matmul("b_wup", sv["h2"], d_up, "tn", BF16, out_shards=4)
    dx1, g["norm2_g"], g["mod3"], g["mod4"] = _normmod_bwd(dm, "b_norm2", sv["x1"], dh2, dx2, wl["norm2_g"],
                                                             mods[3], mods[4])
    dmv, g["mod2"] = _gate_bwd(dm, "b_gate2", dx1, sv["m"], mods[2])
    dmix = _matmul("b_dmix", dmv, wl["w_out"], "nt", F32)
    g["w_out"] = _matmul("b_wout", sv["mix"], dmv, "tn", BF16)
    proj = sv["proj"]
    if rs is None:
        dqh, dkh, dv = _attn_bwd(dm, "b_attn", sv["qh"], sv["kh"], proj, sv["attn"], dmix)
    else:
        jobs2 = [((l, "w_up"), _blocks42(g.pop("w_up"))), ((l, "w_out"), _blocks42(g.pop("w_out")))]
        sides = [_share_side(ts1), _swap_side([gb for _, gb in jobs2])]
        (dqh, dkh, dv), landed = _attn_bwd(dm, "b_attn", sv["qh"], sv["kh"], proj, sv["attn"], dmix,
                                           side=_merge_sides(*sides))
        fin, recv = _split_outs(sides, landed)
        for (key, gb), t in zip(jobs1, fin):
            rs["done"][key] = t.reshape(-1, t.shape[-1])
        ps2 = [_pair_sum("rs_sum2", gb, r) for (_, gb), r in zip(jobs2, recv)]
    d_aq, d_ak, g["q_g"], g["k_g"] = _qk_bwd(dm, "b_qk", proj, ct, st, wl["q_g"], wl["k_g"], dqh, dkh)
    do, d_hgt, g["hg_g"] = _hgc_bwd(dm, "b_hgc", sv["o_dir"], proj, wl["hg_g"], dmix)
    if rs is None:
        dq_dir, df_dir, dv_dir, g["lb"] = _hgrn_bwd(dm, "b_hgrn", proj, wl["lb"], sv["hist"], do)
    else:
        (dq_dir, df_dir, dv_dir, g["lb"]), recv = _hgrn_bwd(dm, "b_hgrn", proj, wl["lb"], sv["hist"], do,
                                                            side=_xchg_side(ps2))
        ts2 = [_chip_sum("rs_sum4", p, r) for p, r in zip(ps2, recv)]
    d_su, d_sv, g["sg_g"], g["sg_w"], g["sg_bcol"] = _sg_bwd(dm, "b_sg", proj, wl["sg_g"], wl["sg_w"],
                                                            wl["sg_bcol"], dmix)
    dproj = _dproj_assemble(dm, "b_dproj", d_aq, d_ak, dv, dq_dir, df_dir, dv_dir, d_hgt, d_su, d_sv)
    if rs is None:
        dh = _matmul("b_dh", dproj, wl["w_in"], "nt", F32, b_shards=4)
    else:
        dh, fin = _matmul("b_dh", dproj, wl["w_in"], "nt", F32, b_shards=4, side=_share_side(ts2))
        for (key, gb), t in zip(jobs2, fin):
            rs["done"][key] = t.reshape(-1, t.shape[-1])
    g["w_in"] = _matmul("b_win", sv["h"], dproj, "tn", BF16, out_shards=4)
    if rs is not None:
        rs["pending"] = [((l, "w_in"), _blocks42(g.pop("w_in")))]
    dx, g["norm1_g"], g["mod0"], g["mod1"] = _normmod_bwd(dm, "b_norm1", sv["x"], dh, dx1, wl["norm1_g"],
                                                           mods[0], mods[1])
    return dx, g


def _sample_step(dm, x_all, tgt, mods, wls, final_g, tabs, half_blocks=None, rs=None):
    wls = [dict(wl) for wl in wls]
    if half_blocks is not None:
        wls[0]["w_in"] = _all_gather8("ag_w_in", half_blocks[0]["w_in"]).reshape(4, dm.d_model, -1)
    saved = []
    x = x_all
    h = _normmod_fwd(dm, "norm1", x, wls[0]["norm1_g"], mods[0][0], mods[0][1])
    for l in range(dm.depth):
        blocks = None
        if half_blocks is not None:
            blocks = {k: half_blocks[l][k] for k in ("w_up", "w_out", "w_down")}
            if l + 1 < dm.depth:
                blocks["w_in_next"] = half_blocks[l + 1]["w_in"]
        x1, f, sv, wls[l], w_in_next = _layer_fwd(dm, l, x, h, mods[l], wls[l], tabs, blocks=blocks)
        if w_in_next is not None:
            wls[l + 1]["w_in"] = w_in_next
        saved.append(sv)
        if l + 1 < dm.depth:
            x, h = _resnorm_fwd(dm, "resnorm1", x1, f, mods[l][5], wls[l + 1]["norm1_g"], mods[l + 1][0],
                                mods[l + 1][1])
        else:
            x = _res_fwd(dm, "res", x1, f, mods[l][5])
    dx, loss, dfg = _loss_head(dm, "loss_head", x, tgt, final_g)
    grads = [None] * dm.depth
    for l in reversed(range(dm.depth)):
        dx, grads[l] = _layer_bwd(dm, l, dx, saved[l], mods[l], wls[l], tabs, rs=rs)
    if rs is not None:
        for key, gb in rs["pending"]:
            rs["done"][key] = _reduce_scatter_grad("rs_tail", gb)
        rs["pending"] = []
    return loss, dx, grads, dfg


def _all_gather8(name, blk):
    r, cdim = blk.shape

    def body(x_ref, out_ref, send_sems, recv_sems, local_sem):
        x, y, c = _place()
        me, sibling = (x, y, c), (x, y, 1 - c)
        chips = [(1 - x, y), (x, 1 - y), (1 - x, 1 - y)]

        def slot(px, py, pc):
            return out_ref.at[4 * px + 2 * py + pc]

        def copy(k, block, to, src=None):
            return pltpu.make_async_remote_copy(
                src_ref=slot(*block) if src is None else src, dst_ref=slot(*block),
                send_sem=send_sems.at[k], recv_sem=recv_sems.at[k], device_id=to, device_id_type=MESH)

        mine = pltpu.make_async_copy(x_ref, slot(*me), local_sem)
        mine.start()
        first = [copy(0, me, sibling, src=x_ref)]
        first += [copy(1 + j, me, (*chip, c), src=x_ref) for j, chip in enumerate(chips)]
        for cp in first:
            cp.start()
        passed = [copy(4 + j, (*chip, c), sibling) for j, chip in enumerate(chips)]
        for j, chip in enumerate(chips):
            copy(1 + j, (*chip, c), me).wait_recv()
            passed[j].start()
        copy(0, sibling, me).wait_recv()
        for j, chip in enumerate(chips):
            copy(4 + j, (*chip, 1 - c), me).wait_recv()
        for cp in first + passed:
            cp.wait_send()
        mine.wait()

    return pl.pallas_call(
        body, name=name, out_shape=S((8, r, cdim), blk.dtype), in_specs=[_ANY], out_specs=_ANY,
        scratch_shapes=[pltpu.SemaphoreType.DMA((7,)), pltpu.SemaphoreType.DMA((7,)), pltpu.SemaphoreType.DMA])(blk)


def _pair_swap_halves(name, g):
    n_s, _, r, cdim = g.shape

    def body(g_ref, recv_ref, send_sems, recv_sems):
        x, y, c = _place()
        remote = [pltpu.make_async_remote_copy(src_ref=g_ref.at[s, 1 - c], dst_ref=recv_ref.at[s],
                                               send_sem=send_sems.at[s], recv_sem=recv_sems.at[s],
                                               device_id=(x, y, 1 - c), device_id_type=MESH) for s in range(n_s)]
        for cp in remote:
            cp.start()
        for cp in remote:
            cp.wait()

    return pl.pallas_call(
        body, name=name, out_shape=S((n_s, r, cdim), g.dtype), in_specs=[_ANY], out_specs=_ANY,
        scratch_shapes=[pltpu.SemaphoreType.DMA((n_s,))] * 2)(g)


def _pair_sum(name, g, recv):
    n_s, _, r, cdim = g.shape
    tm = _ew_tile(r, cdim, 4)

    def body(g0_ref, g1_ref, r_ref, o_ref):
        own = jnp.where(lax.axis_index("c") == 0, g0_ref[...].astype(F32), g1_ref[...].astype(F32))
        o_ref[...] = (own + r_ref[...].astype(F32)).astype(o_ref.dtype)

    return pl.pallas_call(
        body, name=name, grid=(n_s, r // tm),
        in_specs=[pl.BlockSpec((None, None, tm, cdim), lambda s, i: (s, 0, i, 0)),
                  pl.BlockSpec((None, None, tm, cdim), lambda s, i: (s, 1, i, 0)),
                  pl.BlockSpec((None, tm, cdim), lambda s, i: (s, i, 0))],
        out_specs=pl.BlockSpec((None, tm, cdim), lambda s, i: (s, i, 0)), out_shape=S((n_s, r, cdim), BF16),
        compiler_params=_cparams(("parallel", "parallel")))(g, g, recv)


def _chip_exchange(name, p):
    _, r, cdim = p.shape

    def body(p_ref, recv_ref, send_sems, recv_sems):
        x, y, c = _place()
        peers = [(1 - x, y), (x, 1 - y), (1 - x, 1 - y)]
        remote = [pltpu.make_async_remote_copy(src_ref=p_ref.at[2 * px + py], dst_ref=recv_ref.at[k],
                                               send_sem=send_sems.at[k], recv_sem=recv_sems.at[k],
                                               device_id=(px, py, c), device_id_type=MESH)
                  for k, (px, py) in enumerate(peers)]
        for cp in remote:
            cp.start()
        for cp in remote:
            cp.wait()

    return pl.pallas_call(
        body, name=name, out_shape=S((3, r, cdim), p.dtype), in_specs=[_ANY], out_specs=_ANY,
        scratch_shapes=[pltpu.SemaphoreType.DMA((3,)), pltpu.SemaphoreType.DMA((3,))])(p)


def _chip_sum(name, p, recv):
    n_s, r, cdim = p.shape
    tm = _ew_tile(r, cdim, 9)

    def body(*refs):
        chip = 2 * lax.axis_index("x") + lax.axis_index("y")
        own = refs[n_s - 1][...].astype(F32)
        for s in range(n_s - 2, -1, -1):
            own = jnp.where(chip == s, refs[s][...].astype(F32), own)
        r0, r1, r2, o_ref = refs[n_s:]
        tot = ((own + r0[...].astype(F32)) + r1[...].astype(F32)) + r2[...].astype(F32)
        o_ref[0] = tot
        o_ref[1] = tot

    blk = lambda s: pl.BlockSpec((None, tm, cdim), functools.partial(lambda i, s: (s, i, 0), s=s))
    return pl.pallas_call(
        body, name=name, grid=(r // tm,), in_specs=[blk(s) for s in range(n_s)] + [blk(k) for k in range(3)],
        out_specs=pl.BlockSpec((2, tm, cdim), lambda i: (0, i, 0)), out_shape=S((2, r, cdim), F32),
        compiler_params=_cparams(("parallel",)))(*([p] * n_s), *([recv] * 3))


def _pair_share(name, t2):
    def body(t_ref, out_ref, send_sem, recv_sem):
        x, y, c = _place()
        remote = pltpu.make_async_remote_copy(src_ref=out_ref.at[c], dst_ref=out_ref.at[c], send_sem=send_sem,
                                              recv_sem=recv_sem, device_id=(x, y, 1 - c), device_id_type=MESH)
        remote.start()
        remote.wait()

    return pl.pallas_call(
        body, name=name, out_shape=S(t2.shape, t2.dtype), in_specs=[_ANY], out_specs=_ANY,
        input_output_aliases={0: 0},
        scratch_shapes=[pltpu.SemaphoreType.DMA, pltpu.SemaphoreType.DMA])(t2)


def _ew_tile(rows, cols, n_arrays):
    cap = min(1024, max(16, (24 * 1024 * 1024) // (n_arrays * 2 * cols * 4)))
    if rows <= 16:
        return rows
    mult = 16 if any(rows % t == 0 for t in range(16, cap + 1, 16)) else 8
    return _div_tile(rows, cap, mult)


def _reduce_scatter_grad(name, gb):
    _, _, r, cdim = gb.shape
    p = _pair_sum(name + "_sum2", gb, _pair_swap_halves(name + "_swap", gb))
    tot2 = _chip_sum(name + "_sum4", p, _chip_exchange(name + "_xchg", p))
    return _pair_share(name + "_share", tot2).reshape(2 * r, cdim)


def _ada_fwd(name, a16, w_ada, b_cols):
    depth, d, cols = w_ada.shape
    tn = _div_tile(cols, 1536, LANE)

    def body(a_ref, w_ref, b_ref, o_ref):
        a = a_ref[...]
        o_ref[...] = _dg(a * jax.nn.sigmoid(a), w_ref[...], _NN) + b_ref[...]

    return pl.pallas_call(
        body, name=name, grid=(depth, cols // tn),
        in_specs=[pl.BlockSpec((16, d), lambda l, j: (0, 0)), pl.BlockSpec((None, d, tn), lambda l, j: (l, 0, j)),
                  pl.BlockSpec((None, 1, tn), lambda l, j: (l, 0, j))],
        out_specs=pl.BlockSpec((None, 16, tn), lambda l, j: (l, 0, j)), out_shape=S((depth, 16, cols), F32),
        compiler_params=_cparams(("parallel", "parallel")))(a16, w_ada, b_cols)


def _ada_bwd_w(name, a_t, dmod):
    depth, _, cols = dmod.shape
    d = a_t.shape[0]
    tm, tn = _div_tile(d, 512, 8), _div_tile(cols, 1536, LANE)

    def body(a_ref, g_ref, o_ref):
        a = a_ref[...]
        o_ref[...] = _dg(a * jax.nn.sigmoid(a), g_ref[...], _NN)

    return pl.pallas_call(
        body, name=name, grid=(depth, d // tm, cols // tn),
        in_specs=[pl.BlockSpec((tm, 16), lambda l, i, j: (i, 0)), pl.BlockSpec((None, 16, tn), lambda l, i, j: (l, 0, j))],
        out_specs=pl.BlockSpec((None, tm, tn), lambda l, i, j: (l, i, j)), out_shape=S((depth, d, cols), F32),
        compiler_params=_cparams(("parallel", "parallel", "parallel")))(a_t, dmod)


def _ada_bwd_a(name, dmod, w_ada):
    depth, d, cols = w_ada.shape
    tn = _div_tile(cols, 1536, LANE)
    nj = cols // tn

    def body(g_ref, w_ref, o_ref):
        first = jnp.logical_and(pl.program_id(0) == 0, pl.program_id(1) == 0)
        p = _dg(g_ref[...], w_ref[...], _NT)

        @pl.when(first)
        def _():
            o_ref[...] = p

        @pl.when(jnp.logical_not(first))
        def _():
            o_ref[...] += p

    return pl.pallas_call(
        body, name=name, grid=(depth, nj),
        in_specs=[pl.BlockSpec((None, 16, tn), lambda l, j: (l, 0, j)), pl.BlockSpec((None, d, tn), lambda l, j: (l, 0, j))],
        out_specs=pl.BlockSpec((16, d), lambda l, j: (0, 0)), out_shape=S((16, d), F32),
        compiler_params=_cparams(("arbitrary", "arbitrary")))(dmod, w_ada)


def _lbs_fn(p):
    depth = p.shape[0]
    rows = [p[l] for l in range(depth)]
    mx = functools.reduce(jnp.maximum, rows)
    ex = [jnp.exp(r - mx) for r in rows]
    den = functools.reduce(lambda a, b: a + b, ex)
    sm = [e / den for e in ex]
    out, run = [], None
    for l in range(depth):
        run = sm[l] if run is None else run + sm[l]
        out.append(run - sm[0])
    return jnp.stack(out, axis=0)


def _lbs_fwd(name, p):
    def body(p_ref, o_ref):
        o_ref[...] = _lbs_fn(p_ref[...])

    return pl.pallas_call(body, name=name, out_shape=S(p.shape, F32))(p)


def _lbs_bwd(name, p, d_out):
    def body(p_ref, g_ref, o_ref):
        _, vjp = jax.vjp(_lbs_fn, p_ref[...])
        o_ref[...] = vjp(g_ref[...])[0]

    return pl.pallas_call(body, name=name, out_shape=S(p.shape, F32))(p, d_out)


def _sum8(name, g):
    _, r, cdim = g.shape
    tm = _ew_tile(r, cdim, 9)

    def body(g_ref, o_ref):
        acc = g_ref[0]
        for k in range(1, 8):
            acc = acc + g_ref[k]
        o_ref[...] = acc

    return pl.pallas_call(body, name=name, grid=(r // tm,),
                          in_specs=[pl.BlockSpec((8, tm, cdim), lambda i: (0, i, 0))],
                          out_specs=pl.BlockSpec((tm, cdim), lambda i: (i, 0)), out_shape=S((r, cdim), F32),
                          compiler_params=_cparams(("parallel",)))(g)


def _adamw(name, w, m, v, g):
    rows, cols = w.shape
    tm = _ew_tile(rows, cols, 7)

    def fn(i, w, m, v, g):
        m = ADAM_B1 * m + (1.0 - ADAM_B1) * g
        v = ADAM_B2 * v + (1.0 - ADAM_B2) * jnp.square(g)
        m_hat = m / (1.0 - ADAM_B1 ** ADAM_STEP)
        v_hat = v / (1.0 - ADAM_B2 ** ADAM_STEP)
        return -ADAM_LR * (m_hat / (jnp.sqrt(v_hat) + ADAM_EPS) + ADAM_WD * w), m, v

    return _rows(name, fn, rows, tm, [R(w), R(m), R(v), R(g)], [(cols, F32)] * 3)


def _silu_grad_mul(name, g, z):
    def body(g_ref, z_ref, o_ref):
        zz = z_ref[...]
        sg = jax.nn.sigmoid(zz)
        o_ref[...] = g_ref[...] * (sg * (1.0 + zz * (1.0 - sg)))

    return pl.pallas_call(body, name=name, out_shape=S(g.shape, F32))(g, z)


def _pack(arrs):
    parts, meta, off = [], [], 0
    for a in arrs:
        n = int(np.prod(a.shape))
        rows = -(-n // (8 * LANE)) * 8
        flat = a.reshape(-1).astype(F32)
        parts.append(jnp.pad(flat, (0, rows * LANE - n)).reshape(rows, LANE))
        meta.append((off, rows, n, a.shape))
        off += rows
    return jnp.concatenate(parts, axis=0), meta


def _unpack(buf, meta, lead=()):
    out = []
    for off, rows, n, shape in meta:
        seg = buf[..., off:off + rows, :].reshape(*lead, rows * LANE)[..., :n]
        out.append(seg.reshape(*lead, *shape))
    return out


_SMALL = ("c_ctx", "b_ada", "norm1_g", "q_norm_g", "k_norm_g", "hg_lower_bounds", "hg_norm_g", "sg_norm_g", "sg_w",
          "sg_b", "norm2_g", "conv_w", "conv_b", "final_norm_g")
_BIG = ("w_ada", "w_in", "w_out", "w_up", "w_down")
_WEIGHTS = ("c_ctx", "w_ada", "b_ada", "norm1_g", "w_in", "q_norm_g", "k_norm_g", "hg_lower_bounds", "hg_norm_g",
            "sg_norm_g", "sg_w", "sg_b", "w_out", "norm2_g", "w_up", "conv_w", "conv_b", "w_down", "final_norm_g")


def _dims_of(x, ctx, w_in, w_down):
    return Dims(d_model=x.shape[-1], seq=x.shape[1], ctx_len=ctx.shape[1], depth=w_in.shape[0],
                d_ff=w_down.shape[1] * 4)


def _step(dm, x, c, ctx, tgt, w, m, v):
    d, depth = dm.d_model, dm.depth
    xi, yi, ci = _place()
    chip = 2 * xi + yi
    me = 4 * xi + 2 * yi + ci
    n_chips = 4
    take_chips = lambda g8: g8[0::2]

    small_in, meta_in = _pack([c, w["conv_w"], w["hg_lower_bounds"]])
    gath = _all_gather8("ag_small_in", small_in)
    c_all, conv_sh, lb_sh = _unpack(gath, meta_in, lead=(8,))
    c_all = c_all.reshape(8, d)
    conv_w = take_chips(conv_sh).transpose(1, 2, 0, 3).reshape(depth, 3, 2 * dm.d_ff)
    lb_logits = take_chips(lb_sh).transpose(1, 2, 0, 3).reshape(2, depth, dm.hg_w)
    lb_p = lb_logits.transpose(1, 0, 2)
    lbs = _lbs_fwd("lbs_fwd", lb_p)

    a16 = jnp.concatenate([c_all, w["c_ctx"][None], jnp.zeros((7, d), F32)], axis=0)
    cols = w["w_ada"].shape[-1]
    b_cols = lax.dynamic_slice_in_dim(w["b_ada"], chip * cols, cols, axis=1)[:, None, :]
    mod_sh = _ada_fwd("ada_fwd", a16, w["w_ada"], b_cols)
    mod_g = take_chips(_all_gather8("ag_mod", mod_sh.reshape(depth * 16, cols)))
    mod_all = mod_g.reshape(n_chips, depth, 16, cols).transpose(1, 2, 0, 3).reshape(depth, 16, n_chips * cols)
    mod_lat = lax.dynamic_index_in_dim(mod_all, me, axis=1, keepdims=False)
    mod_ctx = mod_all[:, 8]
    mods = [[jnp.stack([mod_ctx[l, k * d:(k + 1) * d], mod_lat[l, k * d:(k + 1) * d]]) for k in range(N_MOD)]
            for l in range(depth)]

    def my_half(shard):
        half = shard.shape[0] // 2
        return lax.dynamic_slice_in_dim(shard, ci * half, half, axis=0).astype(BF16)

    wls, half_blocks = [], []
    for l in range(depth):
        half_blocks.append({name: my_half(w[name][l]) for name in ("w_in", "w_up", "w_out", "w_down")})
        wls.append(dict(
            conv_w=conv_w[l], conv_b=w["conv_b"][l][None], norm1_g=w["norm1_g"][l][None],
            norm2_g=w["norm2_g"][l][None], q_g=w["q_norm_g"][l][None], k_g=w["k_norm_g"][l][None],
            hg_g=w["hg_norm_g"][l][None], sg_g=w["sg_norm_g"][l][None], sg_w=w["sg_w"][l],
            sg_bcol=w["sg_b"][l][:, :, None], lb=lbs[l].reshape(2, 1, dm.hg_w)))

    x_all = jnp.concatenate([ctx[0], x[0]], axis=0)
    rs = dict(pending=[], done={})
    loss_row, dx_all, grads, dfg = _sample_step(dm, x_all, tgt[0], mods, wls, w["final_norm_g"][None],
                                                _rope_tables(dm), half_blocks=half_blocks, rs=rs)
    loss = lax.psum(loss_row[0, 0], ("x", "y", "c"))
    grad_x = dx_all[dm.ctx_len:][None]

    g_big = {name: jnp.stack([rs["done"][(l, name)] for l in range(depth)])
             for name in ("w_in", "w_up", "w_out", "w_down")}

    dmod_lat = jnp.stack([jnp.concatenate([grads[l][f"mod{k}"][1] for k in range(N_MOD)]) for l in range(depth)])
    dmod_ctx = jnp.stack([jnp.concatenate([grads[l][f"mod{k}"][0] for k in range(N_MOD)]) for l in range(depth)])
    d_lbs = jnp.stack([grads[l]["lb"].reshape(2, dm.hg_w) for l in range(depth)])
    d_lb_p = _lbs_bwd("lbs_bwd", lb_p, d_lbs).transpose(1, 0, 2)
    stk = lambda key: jnp.stack([grads[l][key] for l in range(depth)])
    part = {
        "b_ada": dmod_lat + dmod_ctx, "norm1_g": stk("norm1_g")[:, 0], "q_norm_g": stk("q_g")[:, 0],
        "k_norm_g": stk("k_g")[:, 0], "hg_lower_bounds": d_lb_p, "hg_norm_g": stk("hg_g")[:, 0],
        "sg_norm_g": stk("sg_g")[:, 0], "sg_w": stk("sg_w"), "sg_b": stk("sg_bcol")[..., 0],
        "norm2_g": stk("norm2_g")[:, 0], "conv_w": stk("conv_w"), "conv_b": stk("conv_b")[:, 0],
        "final_norm_g": dfg[0]}
    names = [n for n in _SMALL if n != "c_ctx"]
    packed, meta = _pack([part[n] for n in names] + [dmod_ctx, dmod_lat])
    gath = _all_gather8("ag_small_grads", packed)
    summed = _unpack(_sum8("sum_small_grads", gath), meta)
    g_small = dict(zip(names, summed[:len(names)]))
    dmod_ctx_tot = summed[len(names)]
    dmod_lat_all = _unpack(gath, meta[-1:], lead=(8,))[0]

    dmod16 = jnp.concatenate([dmod_lat_all.transpose(1, 0, 2), dmod_ctx_tot[:, None], jnp.zeros((depth, 7, 6 * d), F32)],
                             axis=1)
    dmod16 = lax.dynamic_slice_in_dim(dmod16, chip * cols, cols, axis=2)
    g_big["w_ada"] = _ada_bwd_w("ada_bwd_w", a16.T, dmod16)
    da16 = _ada_bwd_a("ada_bwd_a", dmod16, w["w_ada"])
    da_g = take_chips(_all_gather8("ag_dctx", da16))
    da_sum = _rows("sum_dctx", lambda i, a, b, c2, d2: (((a + b) + c2) + d2,), 16, 16,
                   [R(da_g[k]) for k in range(n_chips)], [(d, F32)])[0]
    g_small["c_ctx"] = _silu_grad_mul("dctx_silu", da_sum[8:9], w["c_ctx"][None])[0]

    g_small["conv_w"] = lax.dynamic_slice_in_dim(g_small["conv_w"], chip * w["conv_w"].shape[-1], w["conv_w"].shape[-1], axis=2)
    g_small["hg_lower_bounds"] = lax.dynamic_slice_in_dim(g_small["hg_lower_bounds"], chip * w["hg_lower_bounds"].shape[-1],
                                                          w["hg_lower_bounds"].shape[-1], axis=2)

    grads_out, deltas, new_m, new_v = {}, {}, {}, {}
    for name in _BIG:
        shp = w[name].shape
        flat = lambda a: a.reshape(-1, shp[-1])
        dl, nm, nv = _adamw(f"adamw_{name}", flat(w[name]), flat(m[name]), flat(v[name]), flat(g_big[name]))
        grads_out[name] = g_big[name].reshape(shp)
        deltas[name], new_m[name], new_v[name] = dl.reshape(shp), nm.reshape(shp), nv.reshape(shp)
    pw, meta_s = _pack([w[n] for n in _SMALL])
    pm, _ = _pack([m[n] for n in _SMALL])
    pv, _ = _pack([v[n] for n in _SMALL])
    pg, _ = _pack([g_small[n].reshape(w[n].shape) for n in _SMALL])
    dl, nm, nv = _adamw("adamw_small", pw, pm, pv, pg)
    for name, a, b, c2 in zip(_SMALL, _unpack(dl, meta_s), _unpack(nm, meta_s), _unpack(nv, meta_s)):
        grads_out[name] = g_small[name].reshape(w[name].shape)
        deltas[name], new_m[name], new_v[name] = a, b, c2
    return loss, grad_x, grads_out, deltas, new_m, new_v


def kernel(x, c, ctx, c_ctx, w_ada, b_ada, norm1_g, w_in, q_norm_g, k_norm_g, hg_lower_bounds, hg_norm_g, sg_norm_g, sg_w, sg_b, w_out, norm2_g, w_up, conv_w, conv_b, w_down, final_norm_g, loss_target, m_c_ctx, m_w_ada, m_b_ada, m_norm1_g, m_w_in, m_q_norm_g, m_k_norm_g, m_hg_lower_bounds, m_hg_norm_g, m_sg_norm_g, m_sg_w, m_sg_b, m_w_out, m_norm2_g, m_w_up, m_conv_w, m_conv_b, m_w_down, m_final_norm_g, v_c_ctx, v_w_ada, v_b_ada, v_norm1_g, v_w_in, v_q_norm_g, v_k_norm_g, v_hg_lower_bounds, v_hg_norm_g, v_sg_norm_g, v_sg_w, v_sg_b, v_w_out, v_norm2_g, v_w_up, v_conv_w, v_conv_b, v_w_down, v_final_norm_g):
    w = dict(c_ctx=c_ctx, w_ada=w_ada, b_ada=b_ada, norm1_g=norm1_g, w_in=w_in, q_norm_g=q_norm_g, k_norm_g=k_norm_g, hg_lower_bounds=hg_lower_bounds, hg_norm_g=hg_norm_g, sg_norm_g=sg_norm_g, sg_w=sg_w, sg_b=sg_b, w_out=w_out, norm2_g=norm2_g, w_up=w_up, conv_w=conv_w, conv_b=conv_b, w_down=w_down, final_norm_g=final_norm_g)
    m = dict(c_ctx=m_c_ctx, w_ada=m_w_ada, b_ada=m_b_ada, norm1_g=m_norm1_g, w_in=m_w_in, q_norm_g=m_q_norm_g, k_norm_g=m_k_norm_g, hg_lower_bounds=m_hg_lower_bounds, hg_norm_g=m_hg_norm_g, sg_norm_g=m_sg_norm_g, sg_w=m_sg_w, sg_b=m_sg_b, w_out=m_w_out, norm2_g=m_norm2_g, w_up=m_w_up, conv_w=m_conv_w, conv_b=m_conv_b, w_down=m_w_down, final_norm_g=m_final_norm_g)
    v = dict(c_ctx=v_c_ctx, w_ada=v_w_ada, b_ada=v_b_ada, norm1_g=v_norm1_g, w_in=v_w_in, q_norm_g=v_q_norm_g, k_norm_g=v_k_norm_g, hg_lower_bounds=v_hg_lower_bounds, hg_norm_g=v_hg_norm_g, sg_norm_g=v_sg_norm_g, sg_w=v_sg_w, sg_b=v_sg_b, w_out=v_w_out, norm2_g=v_norm2_g, w_up=v_w_up, conv_w=v_conv_w, conv_b=v_conv_b, w_down=v_w_down, final_norm_g=v_final_norm_g)
    dm = _dims_of(x, ctx, w_in, w_down)
    loss, grad_x, g, dl, nm, nv = _step(dm, x, c, ctx, loss_target, w, m, v)
    return (loss, grad_x, *[g[n] for n in _WEIGHTS], *[dl[n] for n in _WEIGHTS], *[nm[n] for n in _WEIGHTS],
            *[nv[n] for n in _WEIGHTS])
```

```python
import functools
import math
from typing import NamedTuple

import numpy as np
import jax
import jax.numpy as jnp
from jax import lax
from jax.experimental import pallas as pl
from jax.experimental.pallas import tpu as pltpu

F32, BF16 = jnp.float32, jnp.bfloat16
S = jax.ShapeDtypeStruct
MESH = pl.DeviceIdType.MESH

LANE = 128
EPS = 1e-6
F_MIN = 1e-30
ROPE_THETA = 10000.0
N_MOD = 6
ADAM_LR, ADAM_B1, ADAM_B2, ADAM_EPS, ADAM_WD, ADAM_STEP = 0.001, 0.9, 0.999, 1e-08, 0.01, 10
VMEM_LIMIT = 56 * 1024 * 1024


class Dims(NamedTuple):
    d_model: int = 2048
    seq: int = 4096
    ctx_len: int = 256
    grid_w: int = 64
    depth: int = 4
    attn_heads: int = 8
    kv_heads: int = 2
    hg_heads: int = 4
    hg_chunk: int = 16
    sg_groups: int = 4
    d_ff: int = 5632

    @property
    def n_tok(self):
        return self.seq + self.ctx_len

    @property
    def q_w(self):
        return self.attn_heads * LANE

    @property
    def kv_w(self):
        return self.kv_heads * LANE

    @property
    def hg_w(self):
        return self.hg_heads * LANE

    @property
    def sg_w(self):
        return self.sg_groups * LANE

    @property
    def d_mix(self):
        return self.q_w + self.hg_w + self.sg_w

    @property
    def in_sizes(self):
        return (self.q_w, self.kv_w, self.kv_w) + (self.hg_w,) * 5 + (self.sg_w,) * 2

    @property
    def in_cols(self):
        return sum(self.in_sizes)

    @property
    def in_offs(self):
        return tuple(int(v) for v in np.cumsum((0,) + self.in_sizes)[:-1])

    @property
    def row_tile(self):
        return min(256, self.ctx_len)


def _cparams(sem, vmem=VMEM_LIMIT):
    return pltpu.CompilerParams(dimension_semantics=sem, vmem_limit_bytes=vmem)


_ANY = pl.BlockSpec(memory_space=pl.ANY)


class _Side(NamedTuple):
    tag: str
    ins: tuple
    outs: tuple
    alias: tuple
    n_remote: int
    n_local: int
    make: object


def _merge_sides(*sides):
    sides = [s for s in sides if s is not None]
    if len(sides) <= 1:
        return sides[0] if sides else None
    offs, o_in, o_out, o_r, o_l = [], 0, 0, 0, 0
    for s in sides:
        offs.append((o_in, o_out, o_r, o_l))
        o_in, o_out, o_r, o_l = o_in + len(s.ins), o_out + len(s.outs), o_r + s.n_remote, o_l + s.n_local

    def make(sins, souts, sem, lsem):
        remote, local = [], []
        for s, (a, b, r, l) in zip(sides, offs):
            rr, ll = s.make(sins[a:a + len(s.ins)], souts[b:b + len(s.outs)],
                            functools.partial(lambda k, r: sem(r + k), r=r), functools.partial(lambda k, l: lsem(l + k), l=l))
            remote, local = remote + rr, local + ll
        return remote, local

    return _Side("_".join(s.tag for s in sides), sum((s.ins for s in sides), ()), sum((s.outs for s in sides), ()),
                 tuple((a + i, b + o) for s, (a, b, _, _) in zip(sides, offs) for i, o in s.alias), o_r, o_l, make)


def _split_outs(sides, outs):
    res, k = [], 0
    for s in sides:
        if s is not None:
            res.append(tuple(outs[k:k + len(s.outs)]))
            k += len(s.outs)
        else:
            res.append(())
    return res


def _side_scratch(side):
    dma = pltpu.SemaphoreType.DMA
    return [dma((max(1, side.n_remote),)), dma((max(1, side.n_remote),)), dma((max(1, side.n_local),))]


def _pcall(body, *, name, grid, in_specs, out_specs, out_shape, args, sem, scratch_shapes=(), side=None):
    single = not isinstance(out_shape, (list, tuple))
    out_shape_l = [out_shape] if single else list(out_shape)
    out_specs_l = [out_specs] if single else list(out_specs)
    if side is None:
        res = pl.pallas_call(body, name=name, grid=grid, in_specs=list(in_specs), out_specs=out_specs_l,
                             out_shape=out_shape_l, scratch_shapes=list(scratch_shapes),
                             compiler_params=_cparams(sem))(*args)
        return (res[0] if single else res), ()
    n_in, n_out, n_scr = len(in_specs), len(out_shape_l), len(scratch_shapes)
    s_in, s_out = len(side.ins), len(side.outs)
    g = tuple(grid)

    def wrapped(*refs):
        ins, sins = refs[:n_in], refs[n_in:n_in + s_in]
        o0 = n_in + s_in
        outs, souts = refs[o0:o0 + n_out], refs[o0 + n_out:o0 + n_out + s_out]
        scr = refs[o0 + n_out + s_out:o0 + n_out + s_out + n_scr]
        send_sems, recv_sems, local_sems = refs[-3:]
        ids = [pl.program_id(a) for a in range(len(g))]
        first = functools.reduce(jnp.logical_and, [i == 0 for i in ids])
        last = functools.reduce(jnp.logical_and, [i == n - 1 for i, n in zip(ids, g)])
        remote, local = side.make(sins, souts, lambda k: (send_sems.at[k], recv_sems.at[k]), lambda k: local_sems.at[k])

        @pl.when(first)
        def _():
            for cp in local:
                cp.start()
            for snd, _ in remote:
                snd.start()

        body(*ins, *outs, *scr)

        @pl.when(last)
        def _():
            for snd, arr in remote:
                snd.wait_send()
                arr.wait_recv()
            for cp in local:
                cp.wait()

    res = pl.pallas_call(
        wrapped, name=f"{name}_{side.tag}", grid=g, in_specs=list(in_specs) + [_ANY] * s_in,
        out_specs=out_specs_l + [_ANY] * s_out, out_shape=out_shape_l + list(side.outs),
        scratch_shapes=list(scratch_shapes) + _side_scratch(side),
        input_output_aliases={n_in + i: n_out + o for i, o in side.alias},
        compiler_params=_cparams(("arbitrary",) * len(g)))(*args, *side.ins)
    outs = res[:n_out]
    return (outs[0] if single else outs), tuple(res[n_out:])


def _run_side(name, side):
    s_in = len(side.ins)

    def body(*refs):
        sins, souts = refs[:s_in], refs[s_in:s_in + len(side.outs)]
        send_sems, recv_sems, local_sems = refs[-3:]
        remote, local = side.make(sins, souts, lambda k: (send_sems.at[k], recv_sems.at[k]), lambda k: local_sems.at[k])
        for cp in local:
            cp.start()
        for snd, _ in remote:
            snd.start()
        for snd, arr in remote:
            snd.wait_send()
            arr.wait_recv()
        for cp in local:
            cp.wait()

    res = pl.pallas_call(body, name=f"{name}_{side.tag}", in_specs=[_ANY] * s_in, out_specs=[_ANY] * len(side.outs),
                         out_shape=list(side.outs), scratch_shapes=_side_scratch(side),
                         input_output_aliases=dict(side.alias))(*side.ins)
    return tuple(res)


def _place():
    x, y, c = lax.axis_index("x"), lax.axis_index("y"), lax.axis_index("c")
    return x, y, c


def _rcopy(src, dst, sems, to):
    return pltpu.make_async_remote_copy(src_ref=src, dst_ref=dst, send_sem=sems[0], recv_sem=sems[1], device_id=to,
                                        device_id_type=pl.DeviceIdType.MESH)


def _gather_own_side(blks):
    def make(sins, souts, sem, lsem):
        x, y, c = _place()
        me = 4 * x + 2 * y + c
        peers = [(x, y, 1 - c), (1 - x, y, c), (x, 1 - y, c), (1 - x, 1 - y, c)]
        remote, local = [], []
        for w, (src, out) in enumerate(zip(sins, souts)):
            local.append(pltpu.make_async_copy(src, out.at[me], lsem(w)))
            for j, (px, py, pc) in enumerate(peers):
                remote.append((_rcopy(src, out.at[me], sem(4 * w + j), (px, py, pc)),
                               _rcopy(src, out.at[4 * px + 2 * py + pc], sem(4 * w + j), (px, py, pc))))
        return remote, local

    return _Side("gown", tuple(blks), tuple(S((8,) + b.shape, b.dtype) for b in blks), (), 4 * len(blks), len(blks),
                 make)


def _gather_pass_side(bufs):
    def make(sins, souts, sem, lsem):
        x, y, c = _place()
        chips = [(1 - x, y), (x, 1 - y), (1 - x, 1 - y)]
        remote = []
        for w, out in enumerate(souts):
            for j, (px, py) in enumerate(chips):
                mine, theirs = out.at[4 * px + 2 * py + c], out.at[4 * px + 2 * py + 1 - c]
                remote.append((_rcopy(mine, mine, sem(3 * w + j), (x, y, 1 - c)),
                               _rcopy(mine, theirs, sem(3 * w + j), (x, y, 1 - c))))
        return remote, []

    return _Side("gpass", tuple(bufs), tuple(S(b.shape, b.dtype) for b in bufs), tuple((i, i) for i in range(len(bufs))),
                 3 * len(bufs), 0, make)


def _swap_side(gbs):
    def make(sins, souts, sem, lsem):
        x, y, c = _place()
        remote = []
        for w, (g, recv) in enumerate(zip(sins, souts)):
            for s in range(g.shape[0]):
                cp = _rcopy(g.at[s, 1 - c], recv.at[s], sem(4 * w + s), (x, y, 1 - c))
                remote.append((cp, cp))
        return remote, []

    return _Side("swap", tuple(gbs), tuple(S((g.shape[0],) + g.shape[2:], g.dtype) for g in gbs), (),
                 4 * len(gbs), 0, make)


def _xchg_side(ps):
    def make(sins, souts, sem, lsem):
        x, y, c = _place()
        peers = [(1 - x, y), (x, 1 - y), (1 - x, 1 - y)]
        remote = []
        for w, (p, recv) in enumerate(zip(sins, souts)):
            for k, (px, py) in enumerate(peers):
                cp = _rcopy(p.at[2 * px + py], recv.at[k], sem(3 * w + k), (px, py, c))
                remote.append((cp, cp))
        return remote, []

    return _Side("xchg", tuple(ps), tuple(S((3,) + p.shape[1:], p.dtype) for p in ps), (), 3 * len(ps), 0, make)


def _share_side(t2s):
    def make(sins, souts, sem, lsem):
        x, y, c = _place()
        remote = []
        for w, out in enumerate(souts):
            cp = _rcopy(out.at[c], out.at[c], sem(w), (x, y, 1 - c))
            remote.append((cp, _rcopy(out.at[c], out.at[1 - c], sem(w), (x, y, 1 - c))))
        return remote, []

    return _Side("share", tuple(t2s), tuple(S(t.shape, t.dtype) for t in t2s), tuple((i, i) for i in range(len(t2s))),
                 len(t2s), 0, make)


_NN, _NT, _TN = ((1,), (0,)), ((1,), (1,)), ((0,), (0,))


def _dg(a, b, dims):
    return lax.dot_general(a.astype(BF16), b.astype(BF16), (dims, ((), ())), preferred_element_type=F32)


@jax.custom_vjp
def _bdot_nn(a, b):
    return _dg(a, b, _NN)


@jax.custom_vjp
def _bdot_nt(a, b):
    return _dg(a, b, _NT)


@jax.custom_vjp
def _bdot_tn(a, b):
    return _dg(a, b, _TN)


_bdot_nn.defvjp(lambda a, b: (_dg(a, b, _NN), (a, b)),
                lambda r, g: (_bdot_nt(g, r[1]).astype(r[0].dtype), _bdot_tn(r[0], g).astype(r[1].dtype)))
_bdot_nt.defvjp(lambda a, b: (_dg(a, b, _NT), (a, b)),
                lambda r, g: (_bdot_nn(g, r[1]).astype(r[0].dtype), _bdot_tn(g, r[0]).astype(r[1].dtype)))
_bdot_tn.defvjp(lambda a, b: (_dg(a, b, _TN), (a, b)),
                lambda r, g: (_bdot_nt(r[1], g).astype(r[0].dtype), _bdot_nn(r[0], g).astype(r[1].dtype)))


def _f32dot(a, b):
    return lax.dot_general(a, b, (_NN, ((), ())), precision=lax.Precision.HIGHEST, preferred_element_type=F32)


@jax.custom_vjp
def _tri_dot(tri, tri_t, x):
    return _f32dot(tri, x)


_tri_dot.defvjp(lambda tri, tri_t, x: (_f32dot(tri, x), (tri, tri_t)),
                lambda r, g: (jnp.zeros_like(r[0]), jnp.zeros_like(r[1]), _f32dot(r[1], g)))


@jax.custom_vjp
def _pair_swap(x):
    lane = lax.broadcasted_iota(jnp.int32, x.shape, x.ndim - 1)
    return jnp.where(lane % 2 == 0, pltpu.roll(x, LANE - 1, x.ndim - 1), pltpu.roll(x, 1, x.ndim - 1))


_pair_swap.defvjp(lambda x: (_pair_swap(x), None), lambda _, g: (_pair_swap(g),))


def R(a, w=None, cb=0, rmap=None):
    return ("r", a, a.shape[1] if w is None else w, cb, rmap)


def C(a):
    return ("c", a)


def _rows(name, fn, n_rows, tm, ins, outs, accs=()):
    n_in, n_out = len(ins), len(outs)
    in_specs, args = [], []
    for e in ins:
        if e[0] == "r":
            _, a, w, cb, rmap = e
            assert w % LANE == 0 or w == a.shape[1]
            if rmap is None:
                in_specs.append(pl.BlockSpec((tm, w), functools.partial(lambda i, cb: (i, cb), cb=cb)))
            else:
                in_specs.append(pl.BlockSpec((tm, w), functools.partial(lambda i, cb, rm: (rm(i), cb), cb=cb, rm=rmap)))
        else:
            a = e[1]
            in_specs.append(pl.BlockSpec(a.shape, functools.partial(lambda i, nd: (0,) * nd, nd=a.ndim)))
        args.append(a)
    out_shape = [S((n_rows, w), dt) for w, dt in outs] + [S(tuple(sh), F32) for sh in accs]
    out_specs = [pl.BlockSpec((tm, w), lambda i: (i, 0)) for w, _ in outs]
    out_specs += [pl.BlockSpec(tuple(sh), functools.partial(lambda i, nd: (0,) * nd, nd=len(sh))) for sh in accs]

    def body(*refs):
        i = pl.program_id(0)
        vals = fn(i, *[r[...] for r in refs[:n_in]])
        assert len(vals) == n_out + len(accs), (name, len(vals))
        for r, v in zip(refs[n_in:n_in + n_out], vals[:n_out]):
            r[...] = v.astype(r.dtype)
        for r, v in zip(refs[n_in + n_out:], vals[n_out:]):
            def init(r=r, v=v):
                r[...] = v.astype(F32)

            def add(r=r, v=v):
                r[...] += v.astype(F32)

            pl.when(i == 0)(init)
            pl.when(i != 0)(add)

    res = pl.pallas_call(body, name=name, grid=(n_rows // tm,), in_specs=in_specs, out_specs=out_specs,
                         out_shape=out_shape, compiler_params=_cparams(("arbitrary",)))(*args)
    return res


def _div_tile(n, cap, mult):
    if n <= cap:
        return n
    best = None
    for t in range(mult, cap + 1, mult):
        if n % t == 0:
            best = t
    assert best is not None, (n, cap, mult)
    return best


_CAPS_DEFAULT = (1088, 1408, 2048)
_CAPS_DOWN = (544, 1024, 5632)
_CAPS_DH2 = (1088, 1024, 2816)
_CAPS_TOKENS_ONE = (1088, 1408, 4352)
_CAPS_TOKENS_TWO = (1088, 1408, 2176)


def _matmul(name, a, b, form, out_dtype, b_shards=1, out_shards=1, caps=_CAPS_DEFAULT, side=None):
    if form == "tn":
        K, M = a.shape
    else:
        M, K = a.shape
    if form == "nn":
        N = b.shape[-1] * b_shards
    elif form == "nt":
        N = b.shape[-2]
    else:
        N = b.shape[1]
    n_per = N // (b_shards if form == "nn" else out_shards)
    k_per = K // (b_shards if form == "nt" else 1)
    tm = _div_tile(M, caps[0], 16 if form != "tn" else LANE)
    tn = _div_tile(n_per, caps[1], LANE)
    tk = _div_tile(k_per, caps[2], LANE if form != "tn" else 16)
    nk = K // tk
    grid = (M // tm, N // tn, nk)
    nps, kps = n_per // tn, k_per // tk

    if form == "tn":
        a_spec = pl.BlockSpec((tk, tm), lambda i, j, k: (k, i))
    else:
        a_spec = pl.BlockSpec((tm, tk), lambda i, j, k: (i, k))
    if form == "nn":
        if b_shards > 1:
            b_spec = pl.BlockSpec((None, tk, tn), lambda i, j, k: (j // nps, k, j % nps))
        else:
            b_spec = pl.BlockSpec((tk, tn), lambda i, j, k: (k, j))
    elif form == "nt":
        if b_shards > 1:
            b_spec = pl.BlockSpec((None, tn, tk), lambda i, j, k: (k // kps, j, k % kps))
        else:
            b_spec = pl.BlockSpec((tn, tk), lambda i, j, k: (j, k))
    else:
        b_spec = pl.BlockSpec((tk, tn), lambda i, j, k: (k, j))
    if out_shards > 1:
        o_spec = pl.BlockSpec((None, tm, tn), lambda i, j, k: (j // nps, i, j % nps))
        o_shape = S((out_shards, M, n_per), out_dtype)
    else:
        o_spec = pl.BlockSpec((tm, tn), lambda i, j, k: (i, j))
        o_shape = S((M, N), out_dtype)
    dims = {"nn": _NN, "nt": _NT, "tn": _TN}[form]

    def body(a_ref, b_ref, o_ref, acc_ref=None):
        k = pl.program_id(2)
        p = _dg(a_ref[...], b_ref[...], dims)
        if nk == 1:
            o_ref[...] = p.astype(o_ref.dtype)
        else:
            @pl.when(k == 0)
            def _():
                acc_ref[...] = p

            @pl.when(jnp.logical_and(k > 0, k < nk - 1))
            def _():
                acc_ref[...] += p

            @pl.when(k == nk - 1)
            def _():
                o_ref[...] = (acc_ref[...] + p).astype(o_ref.dtype)

    out, landed = _pcall(body, name=name, grid=grid, in_specs=[a_spec, b_spec], out_specs=o_spec, out_shape=o_shape,
                         scratch_shapes=[pltpu.VMEM((tm, tn), F32)] if nk > 1 else [],
                         sem=("parallel", "parallel", "arbitrary"),
                         args=(a, b), side=side)
    return out if side is None else (out, landed)


def _rms(x, g):
    return x * lax.rsqrt(jnp.mean(x * x, axis=-1, keepdims=True) + EPS) * g


def _sel2(mm, is_ctx):
    return jnp.where(is_ctx, mm[0:1], mm[1:2])


def _put2(v, is_ctx):
    row = lax.broadcasted_iota(jnp.int32, (2, v.shape[-1]), 0)
    return jnp.where(row == jnp.where(is_ctx, 0, 1), v, 0.0)


def _normmod(x, g, sh, sc):
    return _rms(x, g) * (1.0 + sc) + sh


def _colsum(v):
    return jnp.sum(v, axis=0, keepdims=True)


def _normmod_fwd(dm, name, x, g, sh2, sc2):
    nc = dm.ctx_len // dm.row_tile

    def fn(i, x, g, sh2, sc2):
        is_ctx = i < nc
        return (_normmod(x, g, _sel2(sh2, is_ctx), _sel2(sc2, is_ctx)),)

    return _rows(name, fn, dm.n_tok, dm.row_tile, [R(x), C(g), C(sh2), C(sc2)], [(dm.d_model, BF16)])[0]


def _resnorm_fwd(dm, name, x, y, gate2, g, sh2, sc2):
    nc = dm.ctx_len // dm.row_tile

    def fn(i, x, y, gate2, g, sh2, sc2):
        is_ctx = i < nc
        x1 = x + _sel2(gate2, is_ctx) * y
        return x1, _normmod(x1, g, _sel2(sh2, is_ctx), _sel2(sc2, is_ctx))

    return _rows(name, fn, dm.n_tok, dm.row_tile, [R(x), R(y), C(gate2), C(g), C(sh2), C(sc2)],
                 [(dm.d_model, F32), (dm.d_model, BF16)])


def _res_fwd(dm, name, x, y, gate2):
    nc = dm.ctx_len // dm.row_tile

    def fn(i, x, y, gate2):
        return (x + _sel2(gate2, i < nc) * y,)

    return _rows(name, fn, dm.n_tok, dm.row_tile, [R(x), R(y), C(gate2)], [(dm.d_model, F32)])[0]


def _gate_bwd(dm, name, dx, y, gate2):
    nc = dm.ctx_len // dm.row_tile

    def fn(i, dx, y, gate2):
        is_ctx = i < nc
        return dx * _sel2(gate2, is_ctx), _put2(_colsum(dx * y), is_ctx)

    return _rows(name, fn, dm.n_tok, dm.row_tile, [R(dx), R(y), C(gate2)], [(dm.d_model, BF16)], [(2, dm.d_model)])


def _normmod_bwd(dm, name, x, dh, dres, g, sh2, sc2):
    nc = dm.ctx_len // dm.row_tile

    def fn(i, x, dh, dres, g, sh2, sc2):
        is_ctx = i < nc
        sh, sc = _sel2(sh2, is_ctx), _sel2(sc2, is_ctx)
        _, vjp = jax.vjp(_normmod, x, g, sh, sc)
        dx, dg, dsh, dsc = vjp(dh)
        return dres + dx, dg, _put2(dsh, is_ctx), _put2(dsc, is_ctx)

    return _rows(name, fn, dm.n_tok, dm.row_tile, [R(x), R(dh), R(dres), C(g), C(sh2), C(sc2)],
                 [(dm.d_model, F32)], [(1, dm.d_model), (2, dm.d_model), (2, dm.d_model)])


def _rope_tables(dm):
    t = jnp.arange(dm.seq)
    row = (t // dm.grid_w).astype(F32)
    col = (t % dm.grid_w).astype(F32)
    n_freq = LANE // 4
    inv = ROPE_THETA ** (-jnp.arange(n_freq, dtype=F32) / n_freq)
    ang = jnp.concatenate([row[:, None] * inv, col[:, None] * inv], axis=-1)
    cos, sin = jnp.cos(ang), jnp.sin(ang)
    ct = jnp.repeat(cos, 2, axis=-1)
    st = jnp.stack([-sin, sin], axis=-1).reshape(dm.seq, LANE)
    ct = jnp.concatenate([jnp.ones((dm.ctx_len, LANE), F32), ct], axis=0)
    st = jnp.concatenate([jnp.zeros((dm.ctx_len, LANE), F32), st], axis=0)
    return ct, st


def _qk_fn(n_q, n_k):
    def fn(aq, ak, ct, st, qg, kg):
        def head(x, g):
            y = _rms(x, g)
            return y * ct + _pair_swap(y) * st

        q = jnp.concatenate([head(aq[:, h * LANE:(h + 1) * LANE], qg) for h in range(n_q)], axis=1)
        k = jnp.concatenate([head(ak[:, h * LANE:(h + 1) * LANE], kg) for h in range(n_k)], axis=1)
        return q, k

    return fn


def _qk_fwd(dm, name, proj, ct, st, qg, kg):
    f = _qk_fn(dm.attn_heads, dm.kv_heads)
    o = dm.in_offs
    return _rows(name, lambda i, *a: f(*a), dm.n_tok, dm.row_tile,
                 [R(proj, dm.q_w, o[0] // dm.q_w), R(proj, dm.kv_w, o[1] // dm.kv_w), R(ct), R(st), C(qg), C(kg)],
                 [(dm.q_w, BF16), (dm.kv_w, BF16)])


def _qk_bwd(dm, name, proj, ct, st, qg, kg, dq, dk):
    f = _qk_fn(dm.attn_heads, dm.kv_heads)
    o = dm.in_offs

    def fn(i, aq, ak, ct, st, qg, kg, dq, dk):
        _, vjp = jax.vjp(lambda aq, ak, qg, kg: f(aq, ak, ct, st, qg, kg), aq, ak, qg, kg)
        return vjp((dq, dk))

    return _rows(name, fn, dm.n_tok, dm.row_tile,
                 [R(proj, dm.q_w, o[0] // dm.q_w), R(proj, dm.kv_w, o[1] // dm.kv_w), R(ct), R(st), C(qg), C(kg),
                  R(dq), R(dk)],
                 [(dm.q_w, BF16), (dm.kv_w, BF16)], [(1, LANE), (1, LANE)])


def _attn_probs(q, k, i, nc, ctx_len, n_tok):
    s = _dg(q, k, _NT) * (LANE ** -0.5)
    col = lax.broadcasted_iota(jnp.int32, (1, n_tok), 1)
    s = s + jnp.where(col < jnp.where(i < nc, ctx_len, n_tok), 0.0, -1e30)
    e = jnp.exp(s - jnp.max(s, axis=-1, keepdims=True))
    return e, 1.0 / jnp.sum(e, axis=-1, keepdims=True)


def _attn_fwd(dm, name, qh, kh, proj, side=None):
    tq, n = dm.row_tile, dm.n_tok
    nc, grp = dm.ctx_len // tq, dm.attn_heads // dm.kv_heads
    v_cb = dm.in_offs[2] // LANE

    def body(q_ref, k_ref, v_ref, o_ref):
        e, inv = _attn_probs(q_ref[...], k_ref[...], pl.program_id(1), nc, dm.ctx_len, n)
        o_ref[...] = (_dg(e, v_ref[...], _NN) * inv).astype(o_ref.dtype)

    out, landed = _pcall(
        body, name=name, grid=(dm.attn_heads, n // tq),
        in_specs=[pl.BlockSpec((tq, LANE), lambda h, i: (i, h)),
                  pl.BlockSpec((n, LANE), lambda h, i: (0, h // grp)),
                  pl.BlockSpec((n, LANE), lambda h, i: (0, v_cb + h // grp))],
        out_specs=pl.BlockSpec((tq, LANE), lambda h, i: (i, h)),
        out_shape=S((n, dm.q_w), BF16), sem=("parallel", "parallel"), args=(qh, kh, proj), side=side)
    return out if side is None else (out, landed)


def _attn_bwd(dm, name, qh, kh, proj, attn, dmix, side=None):
    tq, n = dm.row_tile, dm.n_tok
    nc, grp = dm.ctx_len // tq, dm.attn_heads // dm.kv_heads
    v_cb = dm.in_offs[2] // LANE

    def body(q_ref, k_ref, v_ref, o_ref, do_ref, dq_ref, dk_ref, dv_ref):
        first = jnp.logical_and(pl.program_id(1) == 0, pl.program_id(2) == 0)
        q, k, v, do = q_ref[...], k_ref[...], v_ref[...], do_ref[...]
        e, inv = _attn_probs(q, k, pl.program_id(2), nc, dm.ctx_len, n)
        delta = jnp.sum(do * o_ref[...].astype(F32), axis=-1, keepdims=True)
        ds = e * ((_dg(do, v, _NT) - delta) * (inv * (LANE ** -0.5)))
        dq_ref[...] = _dg(ds, k, _NN)
        dk = _dg(ds, q, _TN)
        dv = _dg(e, do * inv, _TN)

        @pl.when(first)
        def _():
            dk_ref[...] = dk
            dv_ref[...] = dv

        @pl.when(jnp.logical_not(first))
        def _():
            dk_ref[...] += dk
            dv_ref[...] += dv

    outs, landed = _pcall(
        body, name=name, grid=(dm.kv_heads, grp, n // tq),
        in_specs=[pl.BlockSpec((tq, LANE), lambda g, hh, i: (i, g * grp + hh)),
                  pl.BlockSpec((n, LANE), lambda g, hh, i: (0, g)),
                  pl.BlockSpec((n, LANE), lambda g, hh, i: (0, v_cb + g)),
                  pl.BlockSpec((tq, LANE), lambda g, hh, i: (i, g * grp + hh)),
                  pl.BlockSpec((tq, LANE), lambda g, hh, i: (i, g * grp + hh))],
        out_specs=[pl.BlockSpec((tq, LANE), lambda g, hh, i: (i, g * grp + hh)),
                   pl.BlockSpec((n, LANE), lambda g, hh, i: (0, g)),
                   pl.BlockSpec((n, LANE), lambda g, hh, i: (0, g))],
        out_shape=[S((n, dm.q_w), F32), S((n, dm.kv_w), F32), S((n, dm.kv_w), F32)],
        sem=("parallel", "arbitrary", "arbitrary"), args=(qh, kh, proj, attn, dmix), side=side)
    return outs if side is None else (outs, landed)


def _hg_chunk(d, st, qraw, fraw, v, lb):
    c = qraw.shape[0]
    sig = jax.nn.sigmoid(fraw)
    f = lb + (1.0 - lb) * sig
    logf = jnp.log(jnp.maximum(f, F_MIN))
    k = (1.0 - lb) * jax.nn.sigmoid(-fraw)
    q = qraw * jax.nn.sigmoid(qraw)
    r_i = lax.broadcasted_iota(jnp.int32, (c, c), 0)
    c_i = lax.broadcasted_iota(jnp.int32, (c, c), 1)
    sgn = 1 - 2 * d
    tri = ((r_i - c_i) * sgn >= 0).astype(F32)
    tri_t = ((c_i - r_i) * sgn >= 0).astype(F32)
    b = _tri_dot(tri, tri_t, logf)
    b_last = jnp.sum(logf, axis=0, keepdims=True)
    trow = lax.broadcasted_iota(jnp.int32, (c, 1), 0)
    o = _bdot_nt(q * jnp.exp(b), st)
    for s in range(c):
        m = (trow - s) * sgn >= 0
        e = jnp.exp(jnp.where(m, b - b[s:s + 1], 0.0))
        w = jnp.where(m, q * k[s:s + 1] * e, 0.0)
        o = o + jnp.sum(w, axis=-1, keepdims=True) * v[s:s + 1]
    st_new = st * jnp.exp(b_last) + _bdot_tn(v, k * jnp.exp(b_last - b))
    return o, st_new


def _hg_blk(dm, tb):
    nbc, nbl = dm.ctx_len // tb, dm.seq // tb

    def blk(d, j):
        rev = jnp.where(j < nbc, nbc - 1 - j, 2 * nbc + nbl - 1 - j)
        return jnp.where(d == 0, j, rev)

    return blk, nbc + nbl


def _hgrn_fwd(dm, name, proj, lb, side=None):
    tb, n, hh, ck, hw = dm.row_tile, dm.n_tok, dm.hg_heads, dm.hg_chunk, dm.hg_w
    blk, nblk = _hg_blk(dm, tb)
    ncb = tb // ck
    o = dm.in_offs
    q_cb, f_cb, v_cb = o[3] // hw, o[4] // hw, o[6] // hw

    def body(q0, f0, v0, q1, f1, v1, lb_ref, o0, o1, h0, h1, st_ref):
        @pl.when(pl.program_id(0) == 0)
        def _():
            st_ref[...] = jnp.zeros_like(st_ref)

        lbv = lb_ref[...]

        def chunk(ci, carry):
            for d, (q_ref, f_ref, v_ref, o_ref, hist_ref) in enumerate(((q0, f0, v0, o0, h0), (q1, f1, v1, o1, h1))):
                c = ci if d == 0 else ncb - 1 - ci
                rows = pl.ds(pl.multiple_of(c * ck, ck), ck)
                for h in range(hh):
                    cols = slice(h * LANE, (h + 1) * LANE)
                    st = st_ref[d, h]
                    hist_ref[h, c] = st
                    oc, stn = _hg_chunk(d, st, q_ref[rows, cols], f_ref[rows, cols], v_ref[rows, cols],
                                        lbv[d][:, cols])
                    o_ref[rows, cols] = oc
                    st_ref[d, h] = stn
            return carry

        lax.fori_loop(0, ncb, chunk, 0)

    def ins(d):
        return [pl.BlockSpec((tb, hw), lambda j: (blk(d, j), q_cb)),
                pl.BlockSpec((tb, hw), lambda j: (blk(d, j), f_cb + d)),
                pl.BlockSpec((tb, hw), lambda j: (blk(d, j), v_cb))]

    outs, landed = _pcall(
        body, name=name, grid=(nblk,),
        in_specs=ins(0) + ins(1) + [pl.BlockSpec((2, 1, hw), lambda j: (0, 0, 0))],
        out_specs=[pl.BlockSpec((tb, hw), lambda j: (blk(0, j), 0)), pl.BlockSpec((tb, hw), lambda j: (blk(1, j), 0)),
                   pl.BlockSpec((hh, ncb, LANE, LANE), lambda j: (0, blk(0, j), 0, 0)),
                   pl.BlockSpec((hh, ncb, LANE, LANE), lambda j: (0, blk(1, j), 0, 0))],
        out_shape=[S((n, hw), F32)] * 2 + [S((hh, n // ck, LANE, LANE), F32)] * 2,
        scratch_shapes=[pltpu.VMEM((2, hh, LANE, LANE), F32)],
        sem=("arbitrary",), args=(proj,) * 6 + (lb,), side=side)
    res = ((outs[0], outs[1]), (outs[2], outs[3]))
    return res if side is None else (res, landed)


def _hgrn_bwd(dm, name, proj, lb, hist, do, side=None):
    tb, n, hh, ck, hw = dm.row_tile, dm.n_tok, dm.hg_heads, dm.hg_chunk, dm.hg_w
    blk, nblk = _hg_blk(dm, tb)
    ncb = tb // ck
    o = dm.in_offs
    q_cb, f_cb, v_cb = o[3] // hw, o[4] // hw, o[6] // hw

    def rblk(d, j):
        return blk(d, nblk - 1 - j)

    def body(q0, f0, v0, h0, do0, q1, f1, v1, h1, do1, lb_ref, dq0, df0, dv0, dq1, df1, dv1, dlb_ref, dst_ref):
        @pl.when(pl.program_id(0) == 0)
        def _():
            dst_ref[...] = jnp.zeros_like(dst_ref)
            dlb_ref[...] = jnp.zeros_like(dlb_ref)

        lbv = lb_ref[...]
        per_dir = ((q0, f0, v0, h0, do0, dq0, df0, dv0), (q1, f1, v1, h1, do1, dq1, df1, dv1))

        def chunk(ci, carry):
            for d, (q_ref, f_ref, v_ref, hist_ref, do_ref, dq_ref, df_ref, dv_ref) in enumerate(per_dir):
                c = ncb - 1 - ci if d == 0 else ci
                rows = pl.ds(pl.multiple_of(c * ck, ck), ck)
                for h in range(hh):
                    cols = slice(h * LANE, (h + 1) * LANE)
                    _, vjp = jax.vjp(functools.partial(_hg_chunk, d), hist_ref[h, c], q_ref[rows, cols],
                                     f_ref[rows, cols], v_ref[rows, cols], lbv[d][:, cols])
                    dst, dq, df, dv, dlb = vjp((do_ref[rows, cols], dst_ref[d, h]))
                    dq_ref[rows, cols] = dq
                    df_ref[rows, cols] = df
                    dv_ref[rows, cols] = dv
                    dlb_ref[d, :, cols] += dlb
                    dst_ref[d, h] = dst
            return carry

        lax.fori_loop(0, ncb, chunk, 0)

    def ins(d):
        return [pl.BlockSpec((tb, hw), lambda j: (rblk(d, j), q_cb)),
                pl.BlockSpec((tb, hw), lambda j: (rblk(d, j), f_cb + d)),
                pl.BlockSpec((tb, hw), lambda j: (rblk(d, j), v_cb)),
                pl.BlockSpec((hh, ncb, LANE, LANE), lambda j: (0, rblk(d, j), 0, 0)),
                pl.BlockSpec((tb, hw), lambda j: (rblk(d, j), 0))]

    rows_of = lambda d: pl.BlockSpec((tb, hw), lambda j: (rblk(d, j), 0))
    outs, landed = _pcall(
        body, name=name, grid=(nblk,),
        in_specs=ins(0) + ins(1) + [pl.BlockSpec((2, 1, hw), lambda j: (0, 0, 0))],
        out_specs=[rows_of(0)] * 3 + [rows_of(1)] * 3 + [pl.BlockSpec((2, 1, hw), lambda j: (0, 0, 0))],
        out_shape=[S((n, hw), F32)] * 6 + [S((2, 1, hw), F32)],
        scratch_shapes=[pltpu.VMEM((2, hh, LANE, LANE), F32)],
        sem=("arbitrary",),
        args=(proj, proj, proj, hist[0], do, proj, proj, proj, hist[1], do, lb), side=side)
    res = ((outs[0], outs[3]), (outs[1], outs[4]), (outs[2], outs[5]), outs[6])
    return res if side is None else (res, landed)


def _hgc_fn(n_h):
    def fn(o0, o1, gt, g):
        osum = o0 + o1
        y = jnp.concatenate([_rms(osum[:, h * LANE:(h + 1) * LANE], g) for h in range(n_h)], axis=1)
        return y * (gt * jax.nn.sigmoid(gt))

    return fn


def _hgc_fwd(dm, name, o_dir, proj, g):
    f = _hgc_fn(dm.hg_heads)
    cb = dm.in_offs[7] // dm.hg_w
    return _rows(name, lambda i, *a: (f(*a),), dm.n_tok, dm.row_tile,
                 [R(o_dir[0]), R(o_dir[1]), R(proj, dm.hg_w, cb), C(g)], [(dm.hg_w, BF16)])[0]


def _hgc_bwd(dm, name, o_dir, proj, g, dmix):
    f = _hgc_fn(dm.hg_heads)
    cb = dm.in_offs[7] // dm.hg_w

    def fn(i, o0, o1, gt, g, dy):
        _, vjp = jax.vjp(f, o0, o1, gt, g)
        do, _, dgt, dg = vjp(dy)
        return do, dgt, dg

    return _rows(name, fn, dm.n_tok, dm.row_tile,
                 [R(o_dir[0]), R(o_dir[1]), R(proj, dm.hg_w, cb), C(g), R(dmix, dm.hg_w, dm.q_w // dm.hg_w)],
                 [(dm.hg_w, F32), (dm.hg_w, BF16)], [(1, LANE)])


def _sg_fn(n_g):
    def fn(su, sv, ng, w, bcol):
        u = jax.nn.gelu(su)
        gv = jax.nn.gelu(sv)
        outs = []
        for g in range(n_g):
            sl = slice(g * LANE, (g + 1) * LANE)
            vn = _rms(gv[:, sl], ng[:, sl])
            outs.append(_bdot_nn(w[g], vn) + bcol[g])
        return u * jnp.concatenate(outs, axis=1)

    return fn


def _sg_fwd(dm, name, proj, ng, w, bcol):
    f = _sg_fn(dm.sg_groups)
    o = dm.in_offs
    return _rows(name, lambda i, *a: (f(*a),), dm.n_tok, LANE,
                 [R(proj, dm.sg_w, o[8] // dm.sg_w), R(proj, dm.sg_w, o[9] // dm.sg_w), C(ng), C(w), C(bcol)],
                 [(dm.sg_w, BF16)])[0]


def _sg_bwd(dm, name, proj, ng, w, bcol, dmix):
    f = _sg_fn(dm.sg_groups)
    o = dm.in_offs

    def fn(i, su, sv, ng, w, bcol, dy):
        _, vjp = jax.vjp(f, su, sv, ng, w, bcol)
        return vjp(dy)

    return _rows(name, fn, dm.n_tok, LANE,
                 [R(proj, dm.sg_w, o[8] // dm.sg_w), R(proj, dm.sg_w, o[9] // dm.sg_w), C(ng), C(w), C(bcol),
                  R(dmix, dm.sg_w, (dm.q_w + dm.hg_w) // dm.sg_w)],
                 [(dm.sg_w, BF16), (dm.sg_w, BF16)],
                 [(1, dm.sg_w), (dm.sg_groups, LANE, LANE), (dm.sg_groups, LANE, 1)])


def _conv3(x, prev, nxt, w0, w1, w2, zero_prev, zero_next):
    tm = x.shape[0]
    x = x.astype(F32)
    row = lax.broadcasted_iota(jnp.int32, (tm, 1), 0)
    up = jnp.where(zero_prev, 0.0, prev[prev.shape[0] - 1:].astype(F32))
    dn = jnp.where(zero_next, 0.0, nxt[0:1].astype(F32))
    x_m1 = jnp.where(row == 0, up, pltpu.roll(x, 1, 0))
    x_p1 = jnp.where(row == tm - 1, dn, pltpu.roll(x, tm - 1, 0))
    return w0 * x_m1 + w1 * x + w2 * x_p1, x_m1, x_p1


def _conv_edges(dm, tm):
    nbc, nb = dm.ctx_len // tm, dm.n_tok // tm

    def edges(i):
        return (jnp.logical_or(i == 0, i == nbc), jnp.logical_or(i == nbc - 1, i == nb - 1))

    return edges


def _halo_rows(dtype):
    return 16 if dtype == BF16 else 8


def _halo_specs(tm, tn, n_rows, hr, pick):
    last, per = n_rows // hr - 1, tm // hr
    return [pl.BlockSpec((tm, tn), lambda *ids: pick(*ids)),
            pl.BlockSpec((hr, tn), lambda *ids: (jnp.maximum(pick(*ids)[0] * per - 1, 0), pick(*ids)[1])),
            pl.BlockSpec((hr, tn), lambda *ids: (jnp.minimum((pick(*ids)[0] + 1) * per, last), pick(*ids)[1]))]


def _conv_fwd(dm, name, up, cw, cb):
    n, ff, tm = dm.n_tok, dm.d_ff, dm.row_tile
    tn = _div_tile(ff, 1408, LANE)
    nj = ff // tn
    edges = _conv_edges(dm, tm)

    def body(g_ref, gp_ref, gn_ref, v_ref, vp_ref, vn_ref, wg_ref, wv_ref, bg_ref, bv_ref, a_ref):
        zp, zn = edges(pl.program_id(0))
        wg, wv = wg_ref[...], wv_ref[...]
        yg = _conv3(g_ref[...], gp_ref[...], gn_ref[...], wg[0:1], wg[1:2], wg[2:3], zp, zn)[0] + bg_ref[...]
        yv = _conv3(v_ref[...], vp_ref[...], vn_ref[...], wv[0:1], wv[1:2], wv[2:3], zp, zn)[0] + bv_ref[...]
        a_ref[...] = (yg * jax.nn.sigmoid(yg) * yv).astype(a_ref.dtype)

    small = lambda off: pl.BlockSpec((3, tn), lambda i, j: (0, j + off))
    bias = lambda off: pl.BlockSpec((1, tn), lambda i, j: (0, j + off))
    return pl.pallas_call(
        body, name=name, grid=(n // tm, nj),
        in_specs=_halo_specs(tm, tn, n, _halo_rows(up.dtype), lambda i, j: (i, j))
        + _halo_specs(tm, tn, n, _halo_rows(up.dtype), lambda i, j: (i, j + nj))
        + [small(0), small(nj), bias(0), bias(nj)],
        out_specs=pl.BlockSpec((tm, tn), lambda i, j: (i, j)), out_shape=S((n, ff), BF16),
        compiler_params=_cparams(("parallel", "parallel")))(up, up, up, up, up, up, cw, cw, cb, cb)


def _conv_bwd_dy(dm, name, up, cw, cb, da, side=None):
    n, ff, tm = dm.n_tok, dm.d_ff, dm.row_tile
    tn = _div_tile(ff, 1408, LANE)
    nj = ff // tn
    edges = _conv_edges(dm, tm)

    def body(g_ref, gp_ref, gn_ref, v_ref, vp_ref, vn_ref, wg_ref, wv_ref, bg_ref, bv_ref, da_ref,
             dyg_ref, dyv_ref, dwg_ref, dwv_ref, dbg_ref, dbv_ref):
        i = pl.program_id(1)
        zp, zn = edges(i)
        wg, wv = wg_ref[...], wv_ref[...]
        g, v = g_ref[...].astype(F32), v_ref[...].astype(F32)
        cg, g_m1, g_p1 = _conv3(g, gp_ref[...], gn_ref[...], wg[0:1], wg[1:2], wg[2:3], zp, zn)
        cv, v_m1, v_p1 = _conv3(v, vp_ref[...], vn_ref[...], wv[0:1], wv[1:2], wv[2:3], zp, zn)
        yg, yv = cg + bg_ref[...], cv + bv_ref[...]
        sg = jax.nn.sigmoid(yg)
        da = da_ref[...]
        dyg = da * yv * (sg * (1.0 + yg * (1.0 - sg)))
        dyv = da * (yg * sg)
        dyg_ref[...] = dyg.astype(dyg_ref.dtype)
        dyv_ref[...] = dyv.astype(dyv_ref.dtype)
        row = lax.broadcasted_iota(jnp.int32, (3, tn), 0)

        def stack3(dy, a, b, c):
            return jnp.where(row == 0, _colsum(dy * a), jnp.where(row == 1, _colsum(dy * b), _colsum(dy * c)))

        upd = [(dwg_ref, stack3(dyg, g_m1, g, g_p1)), (dwv_ref, stack3(dyv, v_m1, v, v_p1)),
               (dbg_ref, _colsum(dyg)), (dbv_ref, _colsum(dyv))]

        @pl.when(i == 0)
        def _():
            for r, val in upd:
                r[...] = val

        @pl.when(i != 0)
        def _():
            for r, val in upd:
                r[...] += val

    hs = lambda off: _halo_specs(tm, tn, n, _halo_rows(up.dtype), lambda j, i: (i, j + off))
    small = lambda off: pl.BlockSpec((3, tn), lambda j, i: (0, j + off))
    bias = lambda off: pl.BlockSpec((1, tn), lambda j, i: (0, j + off))
    blk = pl.BlockSpec((tm, tn), lambda j, i: (i, j))
    outs, landed = _pcall(
        body, name=name, grid=(nj, n // tm),
        in_specs=hs(0) + hs(nj) + [small(0), small(nj), bias(0), bias(nj), blk],
        out_specs=[blk, blk, small(0), small(0), bias(0), bias(0)],
        out_shape=[S((n, ff), BF16), S((n, ff), BF16), S((3, ff), F32), S((3, ff), F32), S((1, ff), F32),
                   S((1, ff), F32)],
        sem=("parallel", "arbitrary"), args=(up, up, up, up, up, up, cw, cw, cb, cb, da), side=side)
    return outs if side is None else (outs, landed)


def _conv_bwd_dx(dm, name, dyg, dyv, cw):
    n, ff, tm = dm.n_tok, dm.d_ff, dm.row_tile
    tn = _div_tile(ff, 1408, LANE)
    nj = ff // tn
    edges = _conv_edges(dm, tm)

    def body(g_ref, gp_ref, gn_ref, v_ref, vp_ref, vn_ref, wg_ref, wv_ref, o_ref):
        zp, zn = edges(pl.program_id(0))
        half = pl.program_id(1) // nj
        x = jnp.where(half == 0, g_ref[...], v_ref[...])
        xp = jnp.where(half == 0, gp_ref[...], vp_ref[...])
        xn = jnp.where(half == 0, gn_ref[...], vn_ref[...])
        w = jnp.where(half == 0, wg_ref[...], wv_ref[...])
        o_ref[...] = _conv3(x, xp, xn, w[2:3], w[1:2], w[0:1], zp, zn)[0].astype(o_ref.dtype)

    hr = _halo_rows(dyg.dtype)
    g_specs = _halo_specs(tm, tn, n, hr, lambda i, j: (i, jnp.minimum(j, nj - 1)))
    v_specs = _halo_specs(tm, tn, n, hr, lambda i, j: (i, jnp.maximum(j - nj, 0)))
    return pl.pallas_call(
        body, name=name, grid=(n // tm, 2 * nj),
        in_specs=g_specs + v_specs + [pl.BlockSpec((3, tn), lambda i, j: (0, jnp.minimum(j, nj - 1))),
                                      pl.BlockSpec((3, tn), lambda i, j: (0, nj + jnp.maximum(j - nj, 0)))],
        out_specs=pl.BlockSpec((tm, tn), lambda i, j: (i, j)), out_shape=S((n, 2 * ff), BF16),
        compiler_params=_cparams(("parallel", "parallel")))(dyg, dyg, dyg, dyv, dyv, dyv, cw, cw)


def _loss_head(dm, name, x, tgt, g):
    tm = dm.row_tile
    nc = dm.ctx_len // tm

    def fn(i, x, t, g):
        def f(x, g):
            err = _rms(x, g) - t
            return 0.5 * jnp.sum(jnp.mean(err * err, axis=-1, keepdims=True), axis=0, keepdims=True)

        loss, vjp = jax.vjp(f, x, g)
        dx, dg = vjp(jnp.ones((1, 1), F32))
        live = i >= nc
        return (jnp.where(live, dx, 0.0), jnp.where(live, jnp.broadcast_to(loss, (1, LANE)), 0.0),
                jnp.where(live, dg, 0.0))

    return _rows(name, fn, dm.n_tok, tm, [R(x), R(tgt, rmap=lambda i: jnp.maximum(i - nc, 0)), C(g)],
                 [(dm.d_model, F32)], [(1, LANE), (1, dm.d_model)])


def _dproj_assemble(dm, name, d_aq, d_ak, dv, dq_dir, df_dir, dv_dir, d_hgt, d_su, d_sv):
    def fn(i, d_aq, d_ak, dv, q0, q1, f0, f1, v0, v1, d_hgt, d_su, d_sv):
        parts = [d_aq, d_ak, dv, q0 + q1, f0, f1, v0 + v1, d_hgt, d_su, d_sv]
        return (jnp.concatenate([p.astype(F32) for p in parts], axis=1),)

    ins = [R(d_aq), R(d_ak), R(dv), R(dq_dir[0]), R(dq_dir[1]), R(df_dir[0]), R(df_dir[1]), R(dv_dir[0]),
           R(dv_dir[1]), R(d_hgt), R(d_su), R(d_sv)]
    return _rows(name, fn, dm.n_tok, dm.row_tile, ins, [(dm.in_cols, BF16)])[0]


def _layer_fwd(dm, l, x, h, mods, wl, tabs, blocks=None):
    ct, st = tabs
    d = dm.d_model
    w_in_next = None
    proj = _matmul("proj", h, wl["w_in"], "nn", F32, b_shards=4)
    qh, kh = _qk_fwd(dm, "qk", proj, ct, st, wl["q_g"], wl["k_g"])
    if blocks is None:
        attn = _attn_fwd(dm, "attn", qh, kh, proj)
        o_dir, hist = _hgrn_fwd(dm, "hgrn", proj, wl["lb"])
    else:
        attn, g1 = _attn_fwd(dm, "attn", qh, kh, proj, side=_gather_own_side([blocks["w_up"], blocks["w_out"]]))
        second = [blocks["w_down"]] + ([blocks["w_in_next"]] if "w_in_next" in blocks else [])
        sides = [_gather_own_side(second), _gather_pass_side(list(g1))]
        (o_dir, hist), landed = _hgrn_fwd(dm, "hgrn", proj, wl["lb"], side=_merge_sides(*sides))
        g2, g1 = _split_outs(sides, landed)
        wl = dict(wl, w_up=g1[0].reshape(4, d, -1), w_out=g1[1].reshape(dm.d_mix, d))
    hg = _hgc_fwd(dm, "hgc", o_dir, proj, wl["hg_g"])
    sg = _sg_fwd(dm, "sg", proj, wl["sg_g"], wl["sg_w"], wl["sg_bcol"])
    mix = jnp.concatenate([attn, hg, sg], axis=1)
    m = _matmul("out", mix, wl["w_out"], "nn", F32)
    x1, h2 = _resnorm_fwd(dm, "resnorm2", x, m, mods[2], wl["norm2_g"], mods[3], mods[4])
    if blocks is None:
        up = _matmul("up", h2, wl["w_up"], "nn", BF16, b_shards=4)
    else:
        up, g2 = _matmul("up", h2, wl["w_up"], "nn", BF16, b_shards=4, side=_gather_pass_side(list(g2)))
        wl = dict(wl, w_down=g2[0].reshape(dm.d_ff, d))
        if len(g2) > 1:
            w_in_next = g2[1].reshape(4, d, -1)
    a = _conv_fwd(dm, "conv", up, wl["conv_w"], wl["conv_b"])
    f = _matmul("down", a, wl["w_down"], "nn", F32, caps=_CAPS_DOWN)
    saved = dict(x=x, h=h, proj=proj, qh=qh, kh=kh, attn=attn, o_dir=o_dir, hist=hist, mix=mix, m=m, x1=x1, h2=h2,
                 up=up, a=a, f=f)
    return x1, f, saved, wl, w_in_next


def _blocks42(g):
    return g.reshape(4, 2, -1, g.shape[-1])


def _layer_bwd(dm, l, dx2, sv, mods, wl, tabs, rs=None):
    ct, st = tabs
    g = {}
    df, g["mod5"] = _gate_bwd(dm, "b_gate5", dx2, sv["f"], mods[5])
    da = _matmul("b_da", df, wl["w_down"], "nt", F32)
    g["w_down"] = _matmul("b_wdown", sv["a"], df, "tn", BF16, caps=_CAPS_TOKENS_ONE)
    if rs is None:
        dyg, dyv, dwg, dwv, dbg, dbv = _conv_bwd_dy(dm, "b_convdy", sv["up"], wl["conv_w"], wl["conv_b"], da)
    else:
        jobs1 = [((l, "w_down"), _blocks42(g.pop("w_down")))] + rs["pending"]
        rs["pending"] = []
        (dyg, dyv, dwg, dwv, dbg, dbv), recv = _conv_bwd_dy(dm, "b_convdy", sv["up"], wl["conv_w"], wl["conv_b"], da,
                                                             side=_swap_side([gb for _, gb in jobs1]))
        ps1 = [_pair_sum("rs_sum2", gb, r) for (_, gb), r in zip(jobs1, recv)]
    g["conv_w"] = jnp.concatenate([dwg, dwv], axis=1)
    g["conv_b"] = jnp.concatenate([dbg, dbv], axis=1)
    d_up = _conv_bwd_dx(dm, "b_convdx", dyg, dyv, wl["conv_w"])
    if rs is None:
        dh2 = _matmul("b_dh2", d_up, wl["w_up"], "nt", F32, b_shards=4, caps=_CAPS_DH2)
    else:
        dh2, recv = _matmul("b_dh2", d_up, wl["w_up"], "nt", F32, b_shards=4, caps=_CAPS_DH2, side=_xchg_side(ps1))
        ts1 = [_chip_sum("rs_sum4", p, r) for p, r in zip(ps1, recv)]
    g["w_up"] = _matmul("b_wup", sv["h2"], d_up, "tn", BF16, out_shards=4, caps=_CAPS_TOKENS_TWO)
    dx1, g["norm2_g"], g["mod3"], g["mod4"] = _normmod_bwd(dm, "b_norm2", sv["x1"], dh2, dx2, wl["norm2_g"],
                                                             mods[3], mods[4])
    dmv, g["mod2"] = _gate_bwd(dm, "b_gate2", dx1, sv["m"], mods[2])
    dmix = _matmul("b_dmix", dmv, wl["w_out"], "nt", F32)
    g["w_out"] = _matmul("b_wout", sv["mix"], dmv, "tn", BF16, caps=_CAPS_TOKENS_ONE)
    proj = sv["proj"]
    if rs is None:
        dqh, dkh, dv = _attn_bwd(dm, "b_attn", sv["qh"], sv["kh"], proj, sv["attn"], dmix)
    else:
        jobs2 = [((l, "w_up"), _blocks42(g.pop("w_up"))), ((l, "w_out"), _blocks42(g.pop("w_out")))]
        sides = [_share_side(ts1), _swap_side([gb for _, gb in jobs2])]
        (dqh, dkh, dv), landed = _attn_bwd(dm, "b_attn", sv["qh"], sv["kh"], proj, sv["attn"], dmix,
                                           side=_merge_sides(*sides))
        fin, recv = _split_outs(sides, landed)
        for (key, gb), t in zip(jobs1, fin):
            rs["done"][key] = t.reshape(-1, t.shape[-1])
        ps2 = [_pair_sum("rs_sum2", gb, r) for (_, gb), r in zip(jobs2, recv)]
    d_aq, d_ak, g["q_g"], g["k_g"] = _qk_bwd(dm, "b_qk", proj, ct, st, wl["q_g"], wl["k_g"], dqh, dkh)
    do, d_hgt, g["hg_g"] = _hgc_bwd(dm, "b_hgc", sv["o_dir"], proj, wl["hg_g"], dmix)
    if rs is None:
        dq_dir, df_dir, dv_dir, g["lb"] = _hgrn_bwd(dm, "b_hgrn", proj, wl["lb"], sv["hist"], do)
    else:
        (dq_dir, df_dir, dv_dir, g["lb"]), recv = _hgrn_bwd(dm, "b_hgrn", proj, wl["lb"], sv["hist"], do,
                                                            side=_xchg_side(ps2))
        ts2 = [_chip_sum("rs_sum4", p, r) for p, r in zip(ps2, recv)]
    d_su, d_sv, g["sg_g"], g["sg_w"], g["sg_bcol"] = _sg_bwd(dm, "b_sg", proj, wl["sg_g"], wl["sg_w"],
                                                            wl["sg_bcol"], dmix)
    dproj = _dproj_assemble(dm, "b_dproj", d_aq, d_ak, dv, dq_dir, df_dir, dv_dir, d_hgt, d_su, d_sv)
    if rs is None:
        dh = _matmul("b_dh", dproj, wl["w_in"], "nt", F32, b_shards=4)
    else:
        dh, fin = _matmul("b_dh", dproj, wl["w_in"], "nt", F32, b_shards=4, side=_share_side(ts2))
        for (key, gb), t in zip(jobs2, fin):
            rs["done"][key] = t.reshape(-1, t.shape[-1])
    g["w_in"] = _matmul("b_win", sv["h"], dproj, "tn", BF16, out_shards=4, caps=_CAPS_TOKENS_TWO)
    if rs is not None:
        rs["pending"] = [((l, "w_in"), _blocks42(g.pop("w_in")))]
    dx, g["norm1_g"], g["mod0"], g["mod1"] = _normmod_bwd(dm, "b_norm1", sv["x"], dh, dx1, wl["norm1_g"],
                                                           mods[0], mods[1])
    return dx, g


def _sample_step(dm, x_all, tgt, mods, wls, final_g, tabs, half_blocks=None, rs=None):
    wls = [dict(wl) for wl in wls]
    if half_blocks is not None:
        wls[0]["w_in"] = _all_gather8("ag_w_in", half_blocks[0]["w_in"]).reshape(4, dm.d_model, -1)
    saved = []
    x = x_all
    h = _normmod_fwd(dm, "norm1", x, wls[0]["norm1_g"], mods[0][0], mods[0][1])
    for l in range(dm.depth):
        blocks = None
        if half_blocks is not None:
            blocks = {k: half_blocks[l][k] for k in ("w_up", "w_out", "w_down")}
            if l + 1 < dm.depth:
                blocks["w_in_next"] = half_blocks[l + 1]["w_in"]
        x1, f, sv, wls[l], w_in_next = _layer_fwd(dm, l, x, h, mods[l], wls[l], tabs, blocks=blocks)
        if w_in_next is not None:
            wls[l + 1]["w_in"] = w_in_next
        saved.append(sv)
        if l + 1 < dm.depth:
            x, h = _resnorm_fwd(dm, "resnorm1", x1, f, mods[l][5], wls[l + 1]["norm1_g"], mods[l + 1][0],
                                mods[l + 1][1])
        else:
            x = _res_fwd(dm, "res", x1, f, mods[l][5])
    dx, loss, dfg = _loss_head(dm, "loss_head", x, tgt, final_g)
    grads = [None] * dm.depth
    for l in reversed(range(dm.depth)):
        dx, grads[l] = _layer_bwd(dm, l, dx, saved[l], mods[l], wls[l], tabs, rs=rs)
    if rs is not None:
        for key, gb in rs["pending"]:
            rs["done"][key] = _reduce_scatter_grad("rs_tail", gb)
        rs["pending"] = []
    return loss, dx, grads, dfg


def _all_gather8(name, blk):
    r, cdim = blk.shape

    def body(x_ref, out_ref, send_sems, recv_sems, local_sem):
        x, y, c = _place()
        me, sibling = (x, y, c), (x, y, 1 - c)
        chips = [(1 - x, y), (x, 1 - y), (1 - x, 1 - y)]

        def slot(px, py, pc):
            return out_ref.at[4 * px + 2 * py + pc]

        def copy(k, block, to, src=None):
            return pltpu.make_async_remote_copy(
                src_ref=slot(*block) if src is None else src, dst_ref=slot(*block),
                send_sem=send_sems.at[k], recv_sem=recv_sems.at[k], device_id=to, device_id_type=MESH)

        mine = pltpu.make_async_copy(x_ref, slot(*me), local_sem)
        mine.start()
        first = [copy(0, me, sibling, src=x_ref)]
        first += [copy(1 + j, me, (*chip, c), src=x_ref) for j, chip in enumerate(chips)]
        for cp in first:
            cp.start()
        passed = [copy(4 + j, (*chip, c), sibling) for j, chip in enumerate(chips)]
        for j, chip in enumerate(chips):
            copy(1 + j, (*chip, c), me).wait_recv()
            passed[j].start()
        copy(0, sibling, me).wait_recv()
        for j, chip in enumerate(chips):
            copy(4 + j, (*chip, 1 - c), me).wait_recv()
        for cp in first + passed:
            cp.wait_send()
        mine.wait()

    return pl.pallas_call(
        body, name=name, out_shape=S((8, r, cdim), blk.dtype), in_specs=[_ANY], out_specs=_ANY,
        scratch_shapes=[pltpu.SemaphoreType.DMA((7,)), pltpu.SemaphoreType.DMA((7,)), pltpu.SemaphoreType.DMA])(blk)


def _pair_swap_halves(name, g):
    n_s, _, r, cdim = g.shape

    def body(g_ref, recv_ref, send_sems, recv_sems):
        x, y, c = _place()
        remote = [pltpu.make_async_remote_copy(src_ref=g_ref.at[s, 1 - c], dst_ref=recv_ref.at[s],
                                               send_sem=send_sems.at[s], recv_sem=recv_sems.at[s],
                                               device_id=(x, y, 1 - c), device_id_type=MESH) for s in range(n_s)]
        for cp in remote:
            cp.start()
        for cp in remote:
            cp.wait()

    return pl.pallas_call(
        body, name=name, out_shape=S((n_s, r, cdim), g.dtype), in_specs=[_ANY], out_specs=_ANY,
        scratch_shapes=[pltpu.SemaphoreType.DMA((n_s,))] * 2)(g)


def _pair_sum(name, g, recv):
    n_s, _, r, cdim = g.shape
    tm = _ew_tile(r, cdim, 4)

    def body(g0_ref, g1_ref, r_ref, o_ref):
        own = jnp.where(lax.axis_index("c") == 0, g0_ref[...].astype(F32), g1_ref[...].astype(F32))
        o_ref[...] = (own + r_ref[...].astype(F32)).astype(o_ref.dtype)

    return pl.pallas_call(
        body, name=name, grid=(n_s, r // tm),
        in_specs=[pl.BlockSpec((None, None, tm, cdim), lambda s, i: (s, 0, i, 0)),
                  pl.BlockSpec((None, None, tm, cdim), lambda s, i: (s, 1, i, 0)),
                  pl.BlockSpec((None, tm, cdim), lambda s, i: (s, i, 0))],
        out_specs=pl.BlockSpec((None, tm, cdim), lambda s, i: (s, i, 0)), out_shape=S((n_s, r, cdim), BF16),
        compiler_params=_cparams(("parallel", "parallel")))(g, g, recv)


def _chip_exchange(name, p):
    _, r, cdim = p.shape

    def body(p_ref, recv_ref, send_sems, recv_sems):
        x, y, c = _place()
        peers = [(1 - x, y), (x, 1 - y), (1 - x, 1 - y)]
        remote = [pltpu.make_async_remote_copy(src_ref=p_ref.at[2 * px + py], dst_ref=recv_ref.at[k],
                                               send_sem=send_sems.at[k], recv_sem=recv_sems.at[k],
                                               device_id=(px, py, c), device_id_type=MESH)
                  for k, (px, py) in enumerate(peers)]
        for cp in remote:
            cp.start()
        for cp in remote:
            cp.wait()

    return pl.pallas_call(
        body, name=name, out_shape=S((3, r, cdim), p.dtype), in_specs=[_ANY], out_specs=_ANY,
        scratch_shapes=[pltpu.SemaphoreType.DMA((3,)), pltpu.SemaphoreType.DMA((3,))])(p)


def _chip_sum(name, p, recv):
    n_s, r, cdim = p.shape
    tm = _ew_tile(r, cdim, 9)

    def body(*refs):
        chip = 2 * lax.axis_index("x") + lax.axis_index("y")
        own = refs[n_s - 1][...].astype(F32)
        for s in range(n_s - 2, -1, -1):
            own = jnp.where(chip == s, refs[s][...].astype(F32), own)
        r0, r1, r2, o_ref = refs[n_s:]
        tot = ((own + r0[...].astype(F32)) + r1[...].astype(F32)) + r2[...].astype(F32)
        o_ref[0] = tot
        o_ref[1] = tot

    blk = lambda s: pl.BlockSpec((None, tm, cdim), functools.partial(lambda i, s: (s, i, 0), s=s))
    return pl.pallas_call(
        body, name=name, grid=(r // tm,), in_specs=[blk(s) for s in range(n_s)] + [blk(k) for k in range(3)],
        out_specs=pl.BlockSpec((2, tm, cdim), lambda i: (0, i, 0)), out_shape=S((2, r, cdim), F32),
        compiler_params=_cparams(("parallel",)))(*([p] * n_s), *([recv] * 3))


def _pair_share(name, t2):
    def body(t_ref, out_ref, send_sem, recv_sem):
        x, y, c = _place()
        remote = pltpu.make_async_remote_copy(src_ref=out_ref.at[c], dst_ref=out_ref.at[c], send_sem=send_sem,
                                              recv_sem=recv_sem, device_id=(x, y, 1 - c), device_id_type=MESH)
        remote.start()
        remote.wait()

    return pl.pallas_call(
        body, name=name, out_shape=S(t2.shape, t2.dtype), in_specs=[_ANY], out_specs=_ANY,
        input_output_aliases={0: 0},
        scratch_shapes=[pltpu.SemaphoreType.DMA, pltpu.SemaphoreType.DMA])(t2)


def _ew_tile(rows, cols, n_arrays):
    cap = min(1024, max(16, (24 * 1024 * 1024) // (n_arrays * 2 * cols * 4)))
    if rows <= 16:
        return rows
    mult = 16 if any(rows % t == 0 for t in range(16, cap + 1, 16)) else 8
    return _div_tile(rows, cap, mult)


def _reduce_scatter_grad(name, gb):
    _, _, r, cdim = gb.shape
    p = _pair_sum(name + "_sum2", gb, _pair_swap_halves(name + "_swap", gb))
    tot2 = _chip_sum(name + "_sum4", p, _chip_exchange(name + "_xchg", p))
    return _pair_share(name + "_share", tot2).reshape(2 * r, cdim)


def _ada_fwd(name, a16, w_ada, b_cols):
    depth, d, cols = w_ada.shape
    tn = _div_tile(cols, 1536, LANE)

    def body(a_ref, w_ref, b_ref, o_ref):
        a = a_ref[...]
        o_ref[...] = _dg(a * jax.nn.sigmoid(a), w_ref[...], _NN) + b_ref[...]

    return pl.pallas_call(
        body, name=name, grid=(depth, cols // tn),
        in_specs=[pl.BlockSpec((16, d), lambda l, j: (0, 0)), pl.BlockSpec((None, d, tn), lambda l, j: (l, 0, j)),
                  pl.BlockSpec((None, 1, tn), lambda l, j: (l, 0, j))],
        out_specs=pl.BlockSpec((None, 16, tn), lambda l, j: (l, 0, j)), out_shape=S((depth, 16, cols), F32),
        compiler_params=_cparams(("parallel", "parallel")))(a16, w_ada, b_cols)


def _ada_bwd_w(name, a_t, dmod):
    depth, _, cols = dmod.shape
    d = a_t.shape[0]
    tm, tn = _div_tile(d, 512, 8), _div_tile(cols, 1536, LANE)

    def body(a_ref, g_ref, o_ref):
        a = a_ref[...]
        o_ref[...] = _dg(a * jax.nn.sigmoid(a), g_ref[...], _NN)

    return pl.pallas_call(
        body, name=name, grid=(depth, d // tm, cols // tn),
        in_specs=[pl.BlockSpec((tm, 16), lambda l, i, j: (i, 0)), pl.BlockSpec((None, 16, tn), lambda l, i, j: (l, 0, j))],
        out_specs=pl.BlockSpec((None, tm, tn), lambda l, i, j: (l, i, j)), out_shape=S((depth, d, cols), F32),
        compiler_params=_cparams(("parallel", "parallel", "parallel")))(a_t, dmod)


def _ada_bwd_a(name, dmod, w_ada):
    depth, d, cols = w_ada.shape
    tn = _div_tile(cols, 1536, LANE)
    nj = cols // tn

    def body(g_ref, w_ref, o_ref):
        first = jnp.logical_and(pl.program_id(0) == 0, pl.program_id(1) == 0)
        p = _dg(g_ref[...], w_ref[...], _NT)

        @pl.when(first)
        def _():
            o_ref[...] = p

        @pl.when(jnp.logical_not(first))
        def _():
            o_ref[...] += p

    return pl.pallas_call(
        body, name=name, grid=(depth, nj),
        in_specs=[pl.BlockSpec((None, 16, tn), lambda l, j: (l, 0, j)), pl.BlockSpec((None, d, tn), lambda l, j: (l, 0, j))],
        out_specs=pl.BlockSpec((16, d), lambda l, j: (0, 0)), out_shape=S((16, d), F32),
        compiler_params=_cparams(("arbitrary", "arbitrary")))(dmod, w_ada)


def _lbs_fn(p):
    depth = p.shape[0]
    rows = [p[l] for l in range(depth)]
    mx = functools.reduce(jnp.maximum, rows)
    ex = [jnp.exp(r - mx) for r in rows]
    den = functools.reduce(lambda a, b: a + b, ex)
    sm = [e / den for e in ex]
    out, run = [], None
    for l in range(depth):
        run = sm[l] if run is None else run + sm[l]
        out.append(run - sm[0])
    return jnp.stack(out, axis=0)


def _lbs_fwd(name, p):
    def body(p_ref, o_ref):
        o_ref[...] = _lbs_fn(p_ref[...])

    return pl.pallas_call(body, name=name, out_shape=S(p.shape, F32))(p)


def _lbs_bwd(name, p, d_out):
    def body(p_ref, g_ref, o_ref):
        _, vjp = jax.vjp(_lbs_fn, p_ref[...])
        o_ref[...] = vjp(g_ref[...])[0]

    return pl.pallas_call(body, name=name, out_shape=S(p.shape, F32))(p, d_out)


def _sum8(name, g):
    _, r, cdim = g.shape
    tm = _ew_tile(r, cdim, 9)

    def body(g_ref, o_ref):
        acc = g_ref[0]
        for k in range(1, 8):
            acc = acc + g_ref[k]
        o_ref[...] = acc

    return pl.pallas_call(body, name=name, grid=(r // tm,),
                          in_specs=[pl.BlockSpec((8, tm, cdim), lambda i: (0, i, 0))],
                          out_specs=pl.BlockSpec((tm, cdim), lambda i: (i, 0)), out_shape=S((r, cdim), F32),
                          compiler_params=_cparams(("parallel",)))(g)


def _adamw(name, w, m, v, g):
    rows, cols = w.shape
    tm = _ew_tile(rows, cols, 7)

    def fn(i, w, m, v, g):
        m = ADAM_B1 * m + (1.0 - ADAM_B1) * g
        v = ADAM_B2 * v + (1.0 - ADAM_B2) * jnp.square(g)
        m_hat = m / (1.0 - ADAM_B1 ** ADAM_STEP)
        v_hat = v / (1.0 - ADAM_B2 ** ADAM_STEP)
        return -ADAM_LR * (m_hat / (jnp.sqrt(v_hat) + ADAM_EPS) + ADAM_WD * w), m, v

    return _rows(name, fn, rows, tm, [R(w), R(m), R(v), R(g)], [(cols, F32)] * 3)


def _silu_grad_mul(name, g, z):
    def body(g_ref, z_ref, o_ref):
        zz = z_ref[...]
        sg = jax.nn.sigmoid(zz)
        o_ref[...] = g_ref[...] * (sg * (1.0 + zz * (1.0 - sg)))

    return pl.pallas_call(body, name=name, out_shape=S(g.shape, F32))(g, z)


def _pack(arrs):
    parts, meta, off = [], [], 0
    for a in arrs:
        n = int(np.prod(a.shape))
        rows = -(-n // (8 * LANE)) * 8
        flat = a.reshape(-1).astype(F32)
        parts.append(jnp.pad(flat, (0, rows * LANE - n)).reshape(rows, LANE))
        meta.append((off, rows, n, a.shape))
        off += rows
    return jnp.concatenate(parts, axis=0), meta


def _unpack(buf, meta, lead=()):
    out = []
    for off, rows, n, shape in meta:
        seg = buf[..., off:off + rows, :].reshape(*lead, rows * LANE)[..., :n]
        out.append(seg.reshape(*lead, *shape))
    return out


_SMALL = ("c_ctx", "b_ada", "norm1_g", "q_norm_g", "k_norm_g", "hg_lower_bounds", "hg_norm_g", "sg_norm_g", "sg_w",
          "sg_b", "norm2_g", "conv_w", "conv_b", "final_norm_g")
_BIG = ("w_ada", "w_in", "w_out", "w_up", "w_down")
_WEIGHTS = ("c_ctx", "w_ada", "b_ada", "norm1_g", "w_in", "q_norm_g", "k_norm_g", "hg_lower_bounds", "hg_norm_g",
            "sg_norm_g", "sg_w", "sg_b", "w_out", "norm2_g", "w_up", "conv_w", "conv_b", "w_down", "final_norm_g")


def _dims_of(x, ctx, w_in, w_down):
    return Dims(d_model=x.shape[-1], seq=x.shape[1], ctx_len=ctx.shape[1], depth=w_in.shape[0],
                d_ff=w_down.shape[1] * 4)


def _step(dm, x, c, ctx, tgt, w, m, v):
    d, depth = dm.d_model, dm.depth
    xi, yi, ci = _place()
    chip = 2 * xi + yi
    me = 4 * xi + 2 * yi + ci
    n_chips = 4
    take_chips = lambda g8: g8[0::2]

    small_in, meta_in = _pack([c, w["conv_w"], w["hg_lower_bounds"]])
    gath = _all_gather8("ag_small_in", small_in)
    c_all, conv_sh, lb_sh = _unpack(gath, meta_in, lead=(8,))
    c_all = c_all.reshape(8, d)
    conv_w = take_chips(conv_sh).transpose(1, 2, 0, 3).reshape(depth, 3, 2 * dm.d_ff)
    lb_logits = take_chips(lb_sh).transpose(1, 2, 0, 3).reshape(2, depth, dm.hg_w)
    lb_p = lb_logits.transpose(1, 0, 2)
    lbs = _lbs_fwd("lbs_fwd", lb_p)

    a16 = jnp.concatenate([c_all, w["c_ctx"][None], jnp.zeros((7, d), F32)], axis=0)
    cols = w["w_ada"].shape[-1]
    b_cols = lax.dynamic_slice_in_dim(w["b_ada"], chip * cols, cols, axis=1)[:, None, :]
    mod_sh = _ada_fwd("ada_fwd", a16, w["w_ada"], b_cols)
    mod_g = take_chips(_all_gather8("ag_mod", mod_sh.reshape(depth * 16, cols)))
    mod_all = mod_g.reshape(n_chips, depth, 16, cols).transpose(1, 2, 0, 3).reshape(depth, 16, n_chips * cols)
    mod_lat = lax.dynamic_index_in_dim(mod_all, me, axis=1, keepdims=False)
    mod_ctx = mod_all[:, 8]
    mods = [[jnp.stack([mod_ctx[l, k * d:(k + 1) * d], mod_lat[l, k * d:(k + 1) * d]]) for k in range(N_MOD)]
            for l in range(depth)]

    def my_half(shard):
        half = shard.shape[0] // 2
        return lax.dynamic_slice_in_dim(shard, ci * half, half, axis=0).astype(BF16)

    wls, half_blocks = [], []
    for l in range(depth):
        half_blocks.append({name: my_half(w[name][l]) for name in ("w_in", "w_up", "w_out", "w_down")})
        wls.append(dict(
            conv_w=conv_w[l], conv_b=w["conv_b"][l][None], norm1_g=w["norm1_g"][l][None],
            norm2_g=w["norm2_g"][l][None], q_g=w["q_norm_g"][l][None], k_g=w["k_norm_g"][l][None],
            hg_g=w["hg_norm_g"][l][None], sg_g=w["sg_norm_g"][l][None], sg_w=w["sg_w"][l],
            sg_bcol=w["sg_b"][l][:, :, None], lb=lbs[l].reshape(2, 1, dm.hg_w)))

    x_all = jnp.concatenate([ctx[0], x[0]], axis=0)
    rs = dict(pending=[], done={})
    loss_row, dx_all, grads, dfg = _sample_step(dm, x_all, tgt[0], mods, wls, w["final_norm_g"][None],
                                                _rope_tables(dm), half_blocks=half_blocks, rs=rs)
    loss = lax.psum(loss_row[0, 0], ("x", "y", "c"))
    grad_x = dx_all[dm.ctx_len:][None]

    g_big = {name: jnp.stack([rs["done"][(l, name)] for l in range(depth)])
             for name in ("w_in", "w_up", "w_out", "w_down")}

    dmod_lat = jnp.stack([jnp.concatenate([grads[l][f"mod{k}"][1] for k in range(N_MOD)]) for l in range(depth)])
    dmod_ctx = jnp.stack([jnp.concatenate([grads[l][f"mod{k}"][0] for k in range(N_MOD)]) for l in range(depth)])
    d_lbs = jnp.stack([grads[l]["lb"].reshape(2, dm.hg_w) for l in range(depth)])
    d_lb_p = _lbs_bwd("lbs_bwd", lb_p, d_lbs).transpose(1, 0, 2)
    stk = lambda key: jnp.stack([grads[l][key] for l in range(depth)])
    part = {
        "b_ada": dmod_lat + dmod_ctx, "norm1_g": stk("norm1_g")[:, 0], "q_norm_g": stk("q_g")[:, 0],
        "k_norm_g": stk("k_g")[:, 0], "hg_lower_bounds": d_lb_p, "hg_norm_g": stk("hg_g")[:, 0],
        "sg_norm_g": stk("sg_g")[:, 0], "sg_w": stk("sg_w"), "sg_b": stk("sg_bcol")[..., 0],
        "norm2_g": stk("norm2_g")[:, 0], "conv_w": stk("conv_w"), "conv_b": stk("conv_b")[:, 0],
        "final_norm_g": dfg[0]}
    names = [n for n in _SMALL if n != "c_ctx"]
    packed, meta = _pack([part[n] for n in names] + [dmod_ctx, dmod_lat])
    gath = _all_gather8("ag_small_grads", packed)
    summed = _unpack(_sum8("sum_small_grads", gath), meta)
    g_small = dict(zip(names, summed[:len(names)]))
    dmod_ctx_tot = summed[len(names)]
    dmod_lat_all = _unpack(gath, meta[-1:], lead=(8,))[0]

    dmod16 = jnp.concatenate([dmod_lat_all.transpose(1, 0, 2), dmod_ctx_tot[:, None], jnp.zeros((depth, 7, 6 * d), F32)],
                             axis=1)
    dmod16 = lax.dynamic_slice_in_dim(dmod16, chip * cols, cols, axis=2)
    g_big["w_ada"] = _ada_bwd_w("ada_bwd_w", a16.T, dmod16)
    da16 = _ada_bwd_a("ada_bwd_a", dmod16, w["w_ada"])
    da_g = take_chips(_all_gather8("ag_dctx", da16))
    da_sum = _rows("sum_dctx", lambda i, a, b, c2, d2: (((a + b) + c2) + d2,), 16, 16,
                   [R(da_g[k]) for k in range(n_chips)], [(d, F32)])[0]
    g_small["c_ctx"] = _silu_grad_mul("dctx_silu", da_sum[8:9], w["c_ctx"][None])[0]

    g_small["conv_w"] = lax.dynamic_slice_in_dim(g_small["conv_w"], chip * w["conv_w"].shape[-1], w["conv_w"].shape[-1], axis=2)
    g_small["hg_lower_bounds"] = lax.dynamic_slice_in_dim(g_small["hg_lower_bounds"], chip * w["hg_lower_bounds"].shape[-1],
                                                          w["hg_lower_bounds"].shape[-1], axis=2)

    grads_out, deltas, new_m, new_v = {}, {}, {}, {}
    for name in _BIG:
        shp = w[name].shape
        flat = lambda a: a.reshape(-1, shp[-1])
        dl, nm, nv = _adamw(f"adamw_{name}", flat(w[name]), flat(m[name]), flat(v[name]), flat(g_big[name]))
        grads_out[name] = g_big[name].reshape(shp)
        deltas[name], new_m[name], new_v[name] = dl.reshape(shp), nm.reshape(shp), nv.reshape(shp)
    pw, meta_s = _pack([w[n] for n in _SMALL])
    pm, _ = _pack([m[n] for n in _SMALL])
    pv, _ = _pack([v[n] for n in _SMALL])
    pg, _ = _pack([g_small[n].reshape(w[n].shape) for n in _SMALL])
    dl, nm, nv = _adamw("adamw_small", pw, pm, pv, pg)
    for name, a, b, c2 in zip(_SMALL, _unpack(dl, meta_s), _unpack(nm, meta_s), _unpack(nv, meta_s)):
        grads_out[name] = g_small[name].reshape(w[name].shape)
        deltas[name], new_m[name], new_v[name] = a, b, c2
    return loss, grad_x, grads_out, deltas, new_m, new_v


def kernel(x, c, ctx, c_ctx, w_ada, b_ada, norm1_g, w_in, q_norm_g, k_norm_g, hg_lower_bounds, hg_norm_g, sg_norm_g, sg_w, sg_b, w_out, norm2_g, w_up, conv_w, conv_b, w_down, final_norm_g, loss_target, m_c_ctx, m_w_ada, m_b_ada, m_norm1_g, m_w_in, m_q_norm_g, m_k_norm_g, m_hg_lower_bounds, m_hg_norm_g, m_sg_norm_g, m_sg_w, m_sg_b, m_w_out, m_norm2_g, m_w_up, m_conv_w, m_conv_b, m_w_down, m_final_norm_g, v_c_ctx, v_w_ada, v_b_ada, v_norm1_g, v_w_in, v_q_norm_g, v_k_norm_g, v_hg_lower_bounds, v_hg_norm_g, v_sg_norm_g, v_sg_w, v_sg_b, v_w_out, v_norm2_g, v_w_up, v_conv_w, v_conv_b, v_w_down, v_final_norm_g):
    w = dict(c_ctx=c_ctx, w_ada=w_ada, b_ada=b_ada, norm1_g=norm1_g, w_in=w_in, q_norm_g=q_norm_g, k_norm_g=k_norm_g, hg_lower_bounds=hg_lower_bounds, hg_norm_g=hg_norm_g, sg_norm_g=sg_norm_g, sg_w=sg_w, sg_b=sg_b, w_out=w_out, norm2_g=norm2_g, w_up=w_up, conv_w=conv_w, conv_b=conv_b, w_down=w_down, final_norm_g=final_norm_g)
    m = dict(c_ctx=m_c_ctx, w_ada=m_w_ada, b_ada=m_b_ada, norm1_g=m_norm1_g, w_in=m_w_in, q_norm_g=m_q_norm_g, k_norm_g=m_k_norm_g, hg_lower_bounds=m_hg_lower_bounds, hg_norm_g=m_hg_norm_g, sg_norm_g=m_sg_norm_g, sg_w=m_sg_w, sg_b=m_sg_b, w_out=m_w_out, norm2_g=m_norm2_g, w_up=m_w_up, conv_w=m_conv_w, conv_b=m_conv_b, w_down=m_w_down, final_norm_g=m_final_norm_g)
    v = dict(c_ctx=v_c_ctx, w_ada=v_w_ada, b_ada=v_b_ada, norm1_g=v_norm1_g, w_in=v_w_in, q_norm_g=v_q_norm_g, k_norm_g=v_k_norm_g, hg_lower_bounds=v_hg_lower_bounds, hg_norm_g=v_hg_norm_g, sg_norm_g=v_sg_norm_g, sg_w=v_sg_w, sg_b=v_sg_b, w_out=v_w_out, norm2_g=v_norm2_g, w_up=v_w_up, conv_w=v_conv_w, conv_b=v_conv_b, w_down=v_w_down, final_norm_g=v_final_norm_g)
    dm = _dims_of(x, ctx, w_in, w_down)
    loss, grad_x, g, dl, nm, nv = _step(dm, x, c, ctx, loss_target, w, m, v)
    return (loss, grad_x, *[g[n] for n in _WEIGHTS], *[dl[n] for n in _WEIGHTS], *[nm[n] for n in _WEIGHTS],
            *[nv[n] for n in _WEIGHTS])
```

```python
import functools
import math
from typing import NamedTuple

import numpy as np
import jax
import jax.numpy as jnp
from jax import lax
from jax.experimental import pallas as pl
from jax.experimental.pallas import tpu as pltpu

F32, BF16 = jnp.float32, jnp.bfloat16
S = jax.ShapeDtypeStruct
MESH = pl.DeviceIdType.MESH

LANE = 128
EPS = 1e-6
F_MIN = 1e-30
ROPE_THETA = 10000.0
N_MOD = 6
ADAM_LR, ADAM_B1, ADAM_B2, ADAM_EPS, ADAM_WD, ADAM_STEP = 0.001, 0.9, 0.999, 1e-08, 0.01, 10
VMEM_LIMIT = 56 * 1024 * 1024


class Dims(NamedTuple):
    d_model: int = 2048
    seq: int = 4096
    ctx_len: int = 256
    grid_w: int = 64
    depth: int = 4
    attn_heads: int = 8
    kv_heads: int = 2
    hg_heads: int = 4
    hg_chunk: int = 16
    sg_groups: int = 4
    d_ff: int = 5632

    @property
    def n_tok(self):
        return self.seq + self.ctx_len

    @property
    def q_w(self):
        return self.attn_heads * LANE

    @property
    def kv_w(self):
        return self.kv_heads * LANE

    @property
    def hg_w(self):
        return self.hg_heads * LANE

    @property
    def sg_w(self):
        return self.sg_groups * LANE

    @property
    def d_mix(self):
        return self.q_w + self.hg_w + self.sg_w

    @property
    def in_sizes(self):
        return (self.q_w, self.kv_w, self.kv_w) + (self.hg_w,) * 5 + (self.sg_w,) * 2

    @property
    def in_cols(self):
        return sum(self.in_sizes)

    @property
    def in_offs(self):
        return tuple(int(v) for v in np.cumsum((0,) + self.in_sizes)[:-1])

    @property
    def row_tile(self):
        return min(256, self.ctx_len)


def _cparams(sem, vmem=VMEM_LIMIT):
    return pltpu.CompilerParams(dimension_semantics=sem, vmem_limit_bytes=vmem)


_ANY = pl.BlockSpec(memory_space=pl.ANY)


class _Side(NamedTuple):
    tag: str
    ins: tuple
    outs: tuple
    alias: tuple
    n_remote: int
    n_local: int
    make: object


def _merge_sides(*sides):
    sides = [s for s in sides if s is not None]
    if len(sides) <= 1:
        return sides[0] if sides else None
    offs, o_in, o_out, o_r, o_l = [], 0, 0, 0, 0
    for s in sides:
        offs.append((o_in, o_out, o_r, o_l))
        o_in, o_out, o_r, o_l = o_in + len(s.ins), o_out + len(s.outs), o_r + s.n_remote, o_l + s.n_local

    def make(sins, souts, sem, lsem):
        remote, local = [], []
        for s, (a, b, r, l) in zip(sides, offs):
            rr, ll = s.make(sins[a:a + len(s.ins)], souts[b:b + len(s.outs)],
                            functools.partial(lambda k, r: sem(r + k), r=r), functools.partial(lambda k, l: lsem(l + k), l=l))
            remote, local = remote + rr, local + ll
        return remote, local

    return _Side("_".join(s.tag for s in sides), sum((s.ins for s in sides), ()), sum((s.outs for s in sides), ()),
                 tuple((a + i, b + o) for s, (a, b, _, _) in zip(sides, offs) for i, o in s.alias), o_r, o_l, make)


def _split_outs(sides, outs):
    res, k = [], 0
    for s in sides:
        if s is not None:
            res.append(tuple(outs[k:k + len(s.outs)]))
            k += len(s.outs)
        else:
            res.append(())
    return res


def _side_scratch(side):
    dma = pltpu.SemaphoreType.DMA
    return [dma((max(1, side.n_remote),)), dma((max(1, side.n_remote),)), dma((max(1, side.n_local),))]


def _pcall(body, *, name, grid, in_specs, out_specs, out_shape, args, sem, scratch_shapes=(), side=None):
    single = not isinstance(out_shape, (list, tuple))
    out_shape_l = [out_shape] if single else list(out_shape)
    out_specs_l = [out_specs] if single else list(out_specs)
    if side is None:
        res = pl.pallas_call(body, name=name, grid=grid, in_specs=list(in_specs), out_specs=out_specs_l,
                             out_shape=out_shape_l, scratch_shapes=list(scratch_shapes),
                             compiler_params=_cparams(sem))(*args)
        return (res[0] if single else res), ()
    n_in, n_out, n_scr = len(in_specs), len(out_shape_l), len(scratch_shapes)
    s_in, s_out = len(side.ins), len(side.outs)
    g = tuple(grid)

    def wrapped(*refs):
        ins, sins = refs[:n_in], refs[n_in:n_in + s_in]
        o0 = n_in + s_in
        outs, souts = refs[o0:o0 + n_out], refs[o0 + n_out:o0 + n_out + s_out]
        scr = refs[o0 + n_out + s_out:o0 + n_out + s_out + n_scr]
        send_sems, recv_sems, local_sems = refs[-3:]
        ids = [pl.program_id(a) for a in range(len(g))]
        first = functools.reduce(jnp.logical_and, [i == 0 for i in ids])
        last = functools.reduce(jnp.logical_and, [i == n - 1 for i, n in zip(ids, g)])
        remote, local = side.make(sins, souts, lambda k: (send_sems.at[k], recv_sems.at[k]), lambda k: local_sems.at[k])

        @pl.when(first)
        def _():
            for cp in local:
                cp.start()
            for snd, _ in remote:
                snd.start()

        body(*ins, *outs, *scr)

        @pl.when(last)
        def _():
            for snd, arr in remote:
                snd.wait_send()
                arr.wait_recv()
            for cp in local:
                cp.wait()

    res = pl.pallas_call(
        wrapped, name=f"{name}_{side.tag}", grid=g, in_specs=list(in_specs) + [_ANY] * s_in,
        out_specs=out_specs_l + [_ANY] * s_out, out_shape=out_shape_l + list(side.outs),
        scratch_shapes=list(scratch_shapes) + _side_scratch(side),
        input_output_aliases={n_in + i: n_out + o for i, o in side.alias},
        compiler_params=_cparams(("arbitrary",) * len(g)))(*args, *side.ins)
    outs = res[:n_out]
    return (outs[0] if single else outs), tuple(res[n_out:])


def _run_side(name, side):
    s_in = len(side.ins)

    def body(*refs):
        sins, souts = refs[:s_in], refs[s_in:s_in + len(side.outs)]
        send_sems, recv_sems, local_sems = refs[-3:]
        remote, local = side.make(sins, souts, lambda k: (send_sems.at[k], recv_sems.at[k]), lambda k: local_sems.at[k])
        for cp in local:
            cp.start()
        for snd, _ in remote:
            snd.start()
        for snd, arr in remote:
            snd.wait_send()
            arr.wait_recv()
        for cp in local:
            cp.wait()

    res = pl.pallas_call(body, name=f"{name}_{side.tag}", in_specs=[_ANY] * s_in, out_specs=[_ANY] * len(side.outs),
                         out_shape=list(side.outs), scratch_shapes=_side_scratch(side),
                         input_output_aliases=dict(side.alias))(*side.ins)
    return tuple(res)


def _place():
    x, y, c = lax.axis_index("x"), lax.axis_index("y"), lax.axis_index("c")
    return x, y, c


def _rcopy(src, dst, sems, to):
    return pltpu.make_async_remote_copy(src_ref=src, dst_ref=dst, send_sem=sems[0], recv_sem=sems[1], device_id=to,
                                        device_id_type=pl.DeviceIdType.MESH)


def _gather_own_side(blks):
    def make(sins, souts, sem, lsem):
        x, y, c = _place()
        me = 4 * x + 2 * y + c
        peers = [(x, y, 1 - c), (1 - x, y, c), (x, 1 - y, c), (1 - x, 1 - y, c)]
        remote, local = [], []
        for w, (src, out) in enumerate(zip(sins, souts)):
            local.append(pltpu.make_async_copy(src, out.at[me], lsem(w)))
            for j, (px, py, pc) in enumerate(peers):
                remote.append((_rcopy(src, out.at[me], sem(4 * w + j), (px, py, pc)),
                               _rcopy(src, out.at[4 * px + 2 * py + pc], sem(4 * w + j), (px, py, pc))))
        return remote, local

    return _Side("gown", tuple(blks), tuple(S((8,) + b.shape, b.dtype) for b in blks), (), 4 * len(blks), len(blks),
                 make)


def _gather_pass_side(bufs):
    def make(sins, souts, sem, lsem):
        x, y, c = _place()
        chips = [(1 - x, y), (x, 1 - y), (1 - x, 1 - y)]
        remote = []
        for w, out in enumerate(souts):
            for j, (px, py) in enumerate(chips):
                mine, theirs = out.at[4 * px + 2 * py + c], out.at[4 * px + 2 * py + 1 - c]
                remote.append((_rcopy(mine, mine, sem(3 * w + j), (x, y, 1 - c)),
                               _rcopy(mine, theirs, sem(3 * w + j), (x, y, 1 - c))))
        return remote, []

    return _Side("gpass", tuple(bufs), tuple(S(b.shape, b.dtype) for b in bufs), tuple((i, i) for i in range(len(bufs))),
                 3 * len(bufs), 0, make)


def _swap_side(gbs):
    def make(sins, souts, sem, lsem):
        x, y, c = _place()
        remote = []
        for w, (g, recv) in enumerate(zip(sins, souts)):
            for s in range(g.shape[0]):
                cp = _rcopy(g.at[s, 1 - c], recv.at[s], sem(4 * w + s), (x, y, 1 - c))
                remote.append((cp, cp))
        return remote, []

    return _Side("swap", tuple(gbs), tuple(S((g.shape[0],) + g.shape[2:], g.dtype) for g in gbs), (),
                 4 * len(gbs), 0, make)


def _xchg_side(ps):
    def make(sins, souts, sem, lsem):
        x, y, c = _place()
        peers = [(1 - x, y), (x, 1 - y), (1 - x, 1 - y)]
        remote = []
        for w, (p, recv) in enumerate(zip(sins, souts)):
            for k, (px, py) in enumerate(peers):
                cp = _rcopy(p.at[2 * px + py], recv.at[k], sem(3 * w + k), (px, py, c))
                remote.append((cp, cp))
        return remote, []

    return _Side("xchg", tuple(ps), tuple(S((3,) + p.shape[1:], p.dtype) for p in ps), (), 3 * len(ps), 0, make)


def _share_side(t2s):
    def make(sins, souts, sem, lsem):
        x, y, c = _place()
        remote = []
        for w, out in enumerate(souts):
            cp = _rcopy(out.at[c], out.at[c], sem(w), (x, y, 1 - c))
            remote.append((cp, _rcopy(out.at[c], out.at[1 - c], sem(w), (x, y, 1 - c))))
        return remote, []

    return _Side("share", tuple(t2s), tuple(S(t.shape, t.dtype) for t in t2s), tuple((i, i) for i in range(len(t2s))),
                 len(t2s), 0, make)


_NN, _NT, _TN = ((1,), (0,)), ((1,), (1,)), ((0,), (0,))


def _dg(a, b, dims):
    return lax.dot_general(a.astype(BF16), b.astype(BF16), (dims, ((), ())), preferred_element_type=F32)


@jax.custom_vjp
def _bdot_nn(a, b):
    return _dg(a, b, _NN)


@jax.custom_vjp
def _bdot_nt(a, b):
    return _dg(a, b, _NT)


@jax.custom_vjp
def _bdot_tn(a, b):
    return _dg(a, b, _TN)


_bdot_nn.defvjp(lambda a, b: (_dg(a, b, _NN), (a, b)),
                lambda r, g: (_bdot_nt(g, r[1]).astype(r[0].dtype), _bdot_tn(r[0], g).astype(r[1].dtype)))
_bdot_nt.defvjp(lambda a, b: (_dg(a, b, _NT), (a, b)),
                lambda r, g: (_bdot_nn(g, r[1]).astype(r[0].dtype), _bdot_tn(g, r[0]).astype(r[1].dtype)))
_bdot_tn.defvjp(lambda a, b: (_dg(a, b, _TN), (a, b)),
                lambda r, g: (_bdot_nt(r[1], g).astype(r[0].dtype), _bdot_nn(r[0], g).astype(r[1].dtype)))


def _f32dot(a, b):
    return lax.dot_general(a, b, (_NN, ((), ())), precision=lax.Precision.HIGHEST, preferred_element_type=F32)


@jax.custom_vjp
def _tri_dot(tri, tri_t, x):
    return _f32dot(tri, x)


_tri_dot.defvjp(lambda tri, tri_t, x: (_f32dot(tri, x), (tri, tri_t)),
                lambda r, g: (jnp.zeros_like(r[0]), jnp.zeros_like(r[1]), _f32dot(r[1], g)))


@jax.custom_vjp
def _pair_swap(x):
    lane = lax.broadcasted_iota(jnp.int32, x.shape, x.ndim - 1)
    return jnp.where(lane % 2 == 0, pltpu.roll(x, LANE - 1, x.ndim - 1), pltpu.roll(x, 1, x.ndim - 1))


_pair_swap.defvjp(lambda x: (_pair_swap(x), None), lambda _, g: (_pair_swap(g),))


def R(a, w=None, cb=0, rmap=None):
    return ("r", a, a.shape[1] if w is None else w, cb, rmap)


def C(a):
    return ("c", a)


def _rows(name, fn, n_rows, tm, ins, outs, accs=()):
    n_in, n_out = len(ins), len(outs)
    in_specs, args = [], []
    for e in ins:
        if e[0] == "r":
            _, a, w, cb, rmap = e
            assert w % LANE == 0 or w == a.shape[1]
            if rmap is None:
                in_specs.append(pl.BlockSpec((tm, w), functools.partial(lambda i, cb: (i, cb), cb=cb)))
            else:
                in_specs.append(pl.BlockSpec((tm, w), functools.partial(lambda i, cb, rm: (rm(i), cb), cb=cb, rm=rmap)))
        else:
            a = e[1]
            in_specs.append(pl.BlockSpec(a.shape, functools.partial(lambda i, nd: (0,) * nd, nd=a.ndim)))
        args.append(a)
    out_shape = [S((n_rows, w), dt) for w, dt in outs] + [S(tuple(sh), F32) for sh in accs]
    out_specs = [pl.BlockSpec((tm, w), lambda i: (i, 0)) for w, _ in outs]
    out_specs += [pl.BlockSpec(tuple(sh), functools.partial(lambda i, nd: (0,) * nd, nd=len(sh))) for sh in accs]

    def body(*refs):
        i = pl.program_id(0)
        vals = fn(i, *[r[...] for r in refs[:n_in]])
        assert len(vals) == n_out + len(accs), (name, len(vals))
        for r, v in zip(refs[n_in:n_in + n_out], vals[:n_out]):
            r[...] = v.astype(r.dtype)
        for r, v in zip(refs[n_in + n_out:], vals[n_out:]):
            def init(r=r, v=v):
                r[...] = v.astype(F32)

            def add(r=r, v=v):
                r[...] += v.astype(F32)

            pl.when(i == 0)(init)
            pl.when(i != 0)(add)

    res = pl.pallas_call(body, name=name, grid=(n_rows // tm,), in_specs=in_specs, out_specs=out_specs,
                         out_shape=out_shape, compiler_params=_cparams(("arbitrary",)))(*args)
    return res


def _div_tile(n, cap, mult):
    if n <= cap:
        return n
    best = None
    for t in range(mult, cap + 1, mult):
        if n % t == 0:
            best = t
    assert best is not None, (n, cap, mult)
    return best


_CAPS_DEFAULT = (1088, 1408, 2048)
_CAPS_DOWN = (544, 1024, 5632)
_CAPS_DH2 = (1088, 1024, 2816)
_CAPS_TOKENS_ONE = (1088, 1408, 4352)
_CAPS_TOKENS_NARROW = (1088, 704, 4352)


def _matmul(name, a, b, form, out_dtype, b_shards=1, out_shards=1, caps=_CAPS_DEFAULT, side=None):
    if form == "tn":
        K, M = a.shape
    else:
        M, K = a.shape
    if form == "nn":
        N = b.shape[-1] * b_shards
    elif form == "nt":
        N = b.shape[-2]
    else:
        N = b.shape[1]
    n_per = N // (b_shards if form == "nn" else out_shards)
    k_per = K // (b_shards if form == "nt" else 1)
    tm = _div_tile(M, caps[0], 16 if form != "tn" else LANE)
    tn = _div_tile(n_per, caps[1], LANE)
    tk = _div_tile(k_per, caps[2], LANE if form != "tn" else 16)
    nk = K // tk
    grid = (M // tm, N // tn, nk)
    nps, kps = n_per // tn, k_per // tk

    if form == "tn":
        a_spec = pl.BlockSpec((tk, tm), lambda i, j, k: (k, i))
    else:
        a_spec = pl.BlockSpec((tm, tk), lambda i, j, k: (i, k))
    if form == "nn":
        if b_shards > 1:
            b_spec = pl.BlockSpec((None, tk, tn), lambda i, j, k: (j // nps, k, j % nps))
        else:
            b_spec = pl.BlockSpec((tk, tn), lambda i, j, k: (k, j))
    elif form == "nt":
        if b_shards > 1:
            b_spec = pl.BlockSpec((None, tn, tk), lambda i, j, k: (k // kps, j, k % kps))
        else:
            b_spec = pl.BlockSpec((tn, tk), lambda i, j, k: (j, k))
    else:
        b_spec = pl.BlockSpec((tk, tn), lambda i, j, k: (k, j))
    if out_shards > 1:
        o_spec = pl.BlockSpec((None, tm, tn), lambda i, j, k: (j // nps, i, j % nps))
        o_shape = S((out_shards, M, n_per), out_dtype)
    else:
        o_spec = pl.BlockSpec((tm, tn), lambda i, j, k: (i, j))
        o_shape = S((M, N), out_dtype)
    dims = {"nn": _NN, "nt": _NT, "tn": _TN}[form]

    def body(a_ref, b_ref, o_ref, acc_ref=None):
        k = pl.program_id(2)
        p = _dg(a_ref[...], b_ref[...], dims)
        if nk == 1:
            o_ref[...] = p.astype(o_ref.dtype)
        else:
            @pl.when(k == 0)
            def _():
                acc_ref[...] = p

            @pl.when(jnp.logical_and(k > 0, k < nk - 1))
            def _():
                acc_ref[...] += p

            @pl.when(k == nk - 1)
            def _():
                o_ref[...] = (acc_ref[...] + p).astype(o_ref.dtype)

    out, landed = _pcall(body, name=name, grid=grid, in_specs=[a_spec, b_spec], out_specs=o_spec, out_shape=o_shape,
                         scratch_shapes=[pltpu.VMEM((tm, tn), F32)] if nk > 1 else [],
                         sem=("parallel", "parallel", "arbitrary"),
                         args=(a, b), side=side)
    return out if side is None else (out, landed)


def _rms(x, g):
    return x * lax.rsqrt(jnp.mean(x * x, axis=-1, keepdims=True) + EPS) * g


def _sel2(mm, is_ctx):
    return jnp.where(is_ctx, mm[0:1], mm[1:2])


def _put2(v, is_ctx):
    row = lax.broadcasted_iota(jnp.int32, (2, v.shape[-1]), 0)
    return jnp.where(row == jnp.where(is_ctx, 0, 1), v, 0.0)


def _normmod(x, g, sh, sc):
    return _rms(x, g) * (1.0 + sc) + sh


def _colsum(v):
    return jnp.sum(v, axis=0, keepdims=True)


def _normmod_fwd(dm, name, x, g, sh2, sc2):
    nc = dm.ctx_len // dm.row_tile

    def fn(i, x, g, sh2, sc2):
        is_ctx = i < nc
        return (_normmod(x, g, _sel2(sh2, is_ctx), _sel2(sc2, is_ctx)),)

    return _rows(name, fn, dm.n_tok, dm.row_tile, [R(x), C(g), C(sh2), C(sc2)], [(dm.d_model, BF16)])[0]


def _resnorm_fwd(dm, name, x, y, gate2, g, sh2, sc2):
    nc = dm.ctx_len // dm.row_tile

    def fn(i, x, y, gate2, g, sh2, sc2):
        is_ctx = i < nc
        x1 = x + _sel2(gate2, is_ctx) * y
        return x1, _normmod(x1, g, _sel2(sh2, is_ctx), _sel2(sc2, is_ctx))

    return _rows(name, fn, dm.n_tok, dm.row_tile, [R(x), R(y), C(gate2), C(g), C(sh2), C(sc2)],
                 [(dm.d_model, F32), (dm.d_model, BF16)])


def _res_fwd(dm, name, x, y, gate2):
    nc = dm.ctx_len // dm.row_tile

    def fn(i, x, y, gate2):
        return (x + _sel2(gate2, i < nc) * y,)

    return _rows(name, fn, dm.n_tok, dm.row_tile, [R(x), R(y), C(gate2)], [(dm.d_model, F32)])[0]


def _gate_bwd(dm, name, dx, y, gate2):
    nc = dm.ctx_len // dm.row_tile

    def fn(i, dx, y, gate2):
        is_ctx = i < nc
        return dx * _sel2(gate2, is_ctx), _put2(_colsum(dx * y), is_ctx)

    return _rows(name, fn, dm.n_tok, dm.row_tile, [R(dx), R(y), C(gate2)], [(dm.d_model, BF16)], [(2, dm.d_model)])


def _normmod_bwd(dm, name, x, dh, dres, g, sh2, sc2):
    nc = dm.ctx_len // dm.row_tile

    def fn(i, x, dh, dres, g, sh2, sc2):
        is_ctx = i < nc
        sh, sc = _sel2(sh2, is_ctx), _sel2(sc2, is_ctx)
        _, vjp = jax.vjp(_normmod, x, g, sh, sc)
        dx, dg, dsh, dsc = vjp(dh)
        return dres + dx, dg, _put2(dsh, is_ctx), _put2(dsc, is_ctx)

    return _rows(name, fn, dm.n_tok, dm.row_tile, [R(x), R(dh), R(dres), C(g), C(sh2), C(sc2)],
                 [(dm.d_model, F32)], [(1, dm.d_model), (2, dm.d_model), (2, dm.d_model)])


def _rope_tables(dm):
    t = jnp.arange(dm.seq)
    row = (t // dm.grid_w).astype(F32)
    col = (t % dm.grid_w).astype(F32)
    n_freq = LANE // 4
    inv = ROPE_THETA ** (-jnp.arange(n_freq, dtype=F32) / n_freq)
    ang = jnp.concatenate([row[:, None] * inv, col[:, None] * inv], axis=-1)
    cos, sin = jnp.cos(ang), jnp.sin(ang)
    ct = jnp.repeat(cos, 2, axis=-1)
    st = jnp.stack([-sin, sin], axis=-1).reshape(dm.seq, LANE)
    ct = jnp.concatenate([jnp.ones((dm.ctx_len, LANE), F32), ct], axis=0)
    st = jnp.concatenate([jnp.zeros((dm.ctx_len, LANE), F32), st], axis=0)
    return ct, st


def _qk_fn(n_q, n_k):
    def fn(aq, ak, ct, st, qg, kg):
        def head(x, g):
            y = _rms(x, g)
            return y * ct + _pair_swap(y) * st

        q = jnp.concatenate([head(aq[:, h * LANE:(h + 1) * LANE], qg) for h in range(n_q)], axis=1)
        k = jnp.concatenate([head(ak[:, h * LANE:(h + 1) * LANE], kg) for h in range(n_k)], axis=1)
        return q, k

    return fn


def _qk_fwd(dm, name, proj, ct, st, qg, kg):
    f = _qk_fn(dm.attn_heads, dm.kv_heads)
    o = dm.in_offs
    return _rows(name, lambda i, *a: f(*a), dm.n_tok, dm.row_tile,
                 [R(proj, dm.q_w, o[0] // dm.q_w), R(proj, dm.kv_w, o[1] // dm.kv_w), R(ct), R(st), C(qg), C(kg)],
                 [(dm.q_w, BF16), (dm.kv_w, BF16)])


def _qk_bwd(dm, name, proj, ct, st, qg, kg, dq, dk):
    f = _qk_fn(dm.attn_heads, dm.kv_heads)
    o = dm.in_offs

    def fn(i, aq, ak, ct, st, qg, kg, dq, dk):
        _, vjp = jax.vjp(lambda aq, ak, qg, kg: f(aq, ak, ct, st, qg, kg), aq, ak, qg, kg)
        return vjp((dq, dk))

    return _rows(name, fn, dm.n_tok, dm.row_tile,
                 [R(proj, dm.q_w, o[0] // dm.q_w), R(proj, dm.kv_w, o[1] // dm.kv_w), R(ct), R(st), C(qg), C(kg),
                  R(dq), R(dk)],
                 [(dm.q_w, BF16), (dm.kv_w, BF16)], [(1, LANE), (1, LANE)])


def _attn_probs(q, k, i, nc, ctx_len, n_tok):
    s = _dg(q, k, _NT) * (LANE ** -0.5)
    col = lax.broadcasted_iota(jnp.int32, (1, n_tok), 1)
    s = s + jnp.where(col < jnp.where(i < nc, ctx_len, n_tok), 0.0, -1e30)
    e = jnp.exp(s - jnp.max(s, axis=-1, keepdims=True))
    return e, 1.0 / jnp.sum(e, axis=-1, keepdims=True)


def _attn_fwd(dm, name, qh, kh, proj, side=None):
    tq, n = dm.row_tile, dm.n_tok
    nc, grp = dm.ctx_len // tq, dm.attn_heads // dm.kv_heads
    v_cb = dm.in_offs[2] // LANE

    def body(q_ref, k_ref, v_ref, o_ref):
        e, inv = _attn_probs(q_ref[...], k_ref[...], pl.program_id(1), nc, dm.ctx_len, n)
        o_ref[...] = (_dg(e, v_ref[...], _NN) * inv).astype(o_ref.dtype)

    out, landed = _pcall(
        body, name=name, grid=(dm.attn_heads, n // tq),
        in_specs=[pl.BlockSpec((tq, LANE), lambda h, i: (i, h)),
                  pl.BlockSpec((n, LANE), lambda h, i: (0, h // grp)),
                  pl.BlockSpec((n, LANE), lambda h, i: (0, v_cb + h // grp))],
        out_specs=pl.BlockSpec((tq, LANE), lambda h, i: (i, h)),
        out_shape=S((n, dm.q_w), BF16), sem=("parallel", "parallel"), args=(qh, kh, proj), side=side)
    return out if side is None else (out, landed)


def _attn_bwd(dm, name, qh, kh, proj, attn, dmix, side=None):
    tq, n = dm.row_tile, dm.n_tok
    nc, grp = dm.ctx_len // tq, dm.attn_heads // dm.kv_heads
    v_cb = dm.in_offs[2] // LANE

    def body(q_ref, k_ref, v_ref, o_ref, do_ref, dq_ref, dk_ref, dv_ref):
        first = jnp.logical_and(pl.program_id(1) == 0, pl.program_id(2) == 0)
        q, k, v, do = q_ref[...], k_ref[...], v_ref[...], do_ref[...]
        e, inv = _attn_probs(q, k, pl.program_id(2), nc, dm.ctx_len, n)
        delta = jnp.sum(do * o_ref[...].astype(F32), axis=-1, keepdims=True)
        ds = e * ((_dg(do, v, _NT) - delta) * (inv * (LANE ** -0.5)))
        dq_ref[...] = _dg(ds, k, _NN)
        dk = _dg(ds, q, _TN)
        dv = _dg(e, do * inv, _TN)

        @pl.when(first)
        def _():
            dk_ref[...] = dk
            dv_ref[...] = dv

        @pl.when(jnp.logical_not(first))
        def _():
            dk_ref[...] += dk
            dv_ref[...] += dv

    outs, landed = _pcall(
        body, name=name, grid=(dm.kv_heads, grp, n // tq),
        in_specs=[pl.BlockSpec((tq, LANE), lambda g, hh, i: (i, g * grp + hh)),
                  pl.BlockSpec((n, LANE), lambda g, hh, i: (0, g)),
                  pl.BlockSpec((n, LANE), lambda g, hh, i: (0, v_cb + g)),
                  pl.BlockSpec((tq, LANE), lambda g, hh, i: (i, g * grp + hh)),
                  pl.BlockSpec((tq, LANE), lambda g, hh, i: (i, g * grp + hh))],
        out_specs=[pl.BlockSpec((tq, LANE), lambda g, hh, i: (i, g * grp + hh)),
                   pl.BlockSpec((n, LANE), lambda g, hh, i: (0, g)),
                   pl.BlockSpec((n, LANE), lambda g, hh, i: (0, g))],
        out_shape=[S((n, dm.q_w), F32), S((n, dm.kv_w), F32), S((n, dm.kv_w), F32)],
        sem=("parallel", "arbitrary", "arbitrary"), args=(qh, kh, proj, attn, dmix), side=side)
    return outs if side is None else (outs, landed)


def _hg_chunk(d, st, qraw, fraw, v, lb):
    c = qraw.shape[0]
    sig = jax.nn.sigmoid(fraw)
    f = lb + (1.0 - lb) * sig
    logf = jnp.log(jnp.maximum(f, F_MIN))
    k = (1.0 - lb) * jax.nn.sigmoid(-fraw)
    q = qraw * jax.nn.sigmoid(qraw)
    r_i = lax.broadcasted_iota(jnp.int32, (c, c), 0)
    c_i = lax.broadcasted_iota(jnp.int32, (c, c), 1)
    sgn = 1 - 2 * d
    tri = ((r_i - c_i) * sgn >= 0).astype(F32)
    tri_t = ((c_i - r_i) * sgn >= 0).astype(F32)
    b = _tri_dot(tri, tri_t, logf)
    b_last = jnp.sum(logf, axis=0, keepdims=True)
    trow = lax.broadcasted_iota(jnp.int32, (c, 1), 0)
    o = _bdot_nt(q * jnp.exp(b), st)
    for s in range(c):
        m = (trow - s) * sgn >= 0
        e = jnp.exp(jnp.where(m, b - b[s:s + 1], 0.0))
        w = jnp.where(m, q * k[s:s + 1] * e, 0.0)
        o = o + jnp.sum(w, axis=-1, keepdims=True) * v[s:s + 1]
    st_new = st * jnp.exp(b_last) + _bdot_tn(v, k * jnp.exp(b_last - b))
    return o, st_new


def _hg_blk(dm, tb):
    nbc, nbl = dm.ctx_len // tb, dm.seq // tb

    def blk(d, j):
        rev = jnp.where(j < nbc, nbc - 1 - j, 2 * nbc + nbl - 1 - j)
        return jnp.where(d == 0, j, rev)

    return blk, nbc + nbl


def _hgrn_fwd(dm, name, proj, lb, side=None):
    tb, n, hh, ck, hw = dm.row_tile, dm.n_tok, dm.hg_heads, dm.hg_chunk, dm.hg_w
    blk, nblk = _hg_blk(dm, tb)
    ncb = tb // ck
    o = dm.in_offs
    q_cb, f_cb, v_cb = o[3] // hw, o[4] // hw, o[6] // hw

    def body(q0, f0, v0, q1, f1, v1, lb_ref, o0, o1, h0, h1, st_ref):
        @pl.when(pl.program_id(0) == 0)
        def _():
            st_ref[...] = jnp.zeros_like(st_ref)

        lbv = lb_ref[...]

        def chunk(ci, carry):
            for d, (q_ref, f_ref, v_ref, o_ref, hist_ref) in enumerate(((q0, f0, v0, o0, h0), (q1, f1, v1, o1, h1))):
                c = ci if d == 0 else ncb - 1 - ci
                rows = pl.ds(pl.multiple_of(c * ck, ck), ck)
                for h in range(hh):
                    cols = slice(h * LANE, (h + 1) * LANE)
                    st = st_ref[d, h]
                    hist_ref[h, c] = st
                    oc, stn = _hg_chunk(d, st, q_ref[rows, cols], f_ref[rows, cols], v_ref[rows, cols],
                                        lbv[d][:, cols])
                    o_ref[rows, cols] = oc
                    st_ref[d, h] = stn
            return carry

        lax.fori_loop(0, ncb, chunk, 0)

    def ins(d):
        return [pl.BlockSpec((tb, hw), lambda j: (blk(d, j), q_cb)),
                pl.BlockSpec((tb, hw), lambda j: (blk(d, j), f_cb + d)),
                pl.BlockSpec((tb, hw), lambda j: (blk(d, j), v_cb))]

    outs, landed = _pcall(
        body, name=name, grid=(nblk,),
        in_specs=ins(0) + ins(1) + [pl.BlockSpec((2, 1, hw), lambda j: (0, 0, 0))],
        out_specs=[pl.BlockSpec((tb, hw), lambda j: (blk(0, j), 0)), pl.BlockSpec((tb, hw), lambda j: (blk(1, j), 0)),
                   pl.BlockSpec((hh, ncb, LANE, LANE), lambda j: (0, blk(0, j), 0, 0)),
                   pl.BlockSpec((hh, ncb, LANE, LANE), lambda j: (0, blk(1, j), 0, 0))],
        out_shape=[S((n, hw), F32)] * 2 + [S((hh, n // ck, LANE, LANE), F32)] * 2,
        scratch_shapes=[pltpu.VMEM((2, hh, LANE, LANE), F32)],
        sem=("arbitrary",), args=(proj,) * 6 + (lb,), side=side)
    res = ((outs[0], outs[1]), (outs[2], outs[3]))
    return res if side is None else (res, landed)


def _hgrn_bwd(dm, name, proj, lb, hist, do, side=None):
    tb, n, hh, ck, hw = dm.row_tile, dm.n_tok, dm.hg_heads, dm.hg_chunk, dm.hg_w
    blk, nblk = _hg_blk(dm, tb)
    ncb = tb // ck
    o = dm.in_offs
    q_cb, f_cb, v_cb = o[3] // hw, o[4] // hw, o[6] // hw

    def rblk(d, j):
        return blk(d, nblk - 1 - j)

    def body(q0, f0, v0, h0, do0, q1, f1, v1, h1, do1, lb_ref, dq0, df0, dv0, dq1, df1, dv1, dlb_ref, dst_ref):
        @pl.when(pl.program_id(0) == 0)
        def _():
            dst_ref[...] = jnp.zeros_like(dst_ref)
            dlb_ref[...] = jnp.zeros_like(dlb_ref)

        lbv = lb_ref[...]
        per_dir = ((q0, f0, v0, h0, do0, dq0, df0, dv0), (q1, f1, v1, h1, do1, dq1, df1, dv1))

        def chunk(ci, carry):
            for d, (q_ref, f_ref, v_ref, hist_ref, do_ref, dq_ref, df_ref, dv_ref) in enumerate(per_dir):
                c = ncb - 1 - ci if d == 0 else ci
                rows = pl.ds(pl.multiple_of(c * ck, ck), ck)
                for h in range(hh):
                    cols = slice(h * LANE, (h + 1) * LANE)
                    _, vjp = jax.vjp(functools.partial(_hg_chunk, d), hist_ref[h, c], q_ref[rows, cols],
                                     f_ref[rows, cols], v_ref[rows, cols], lbv[d][:, cols])
                    dst, dq, df, dv, dlb = vjp((do_ref[rows, cols], dst_ref[d, h]))
                    dq_ref[rows, cols] = dq
                    df_ref[rows, cols] = df
                    dv_ref[rows, cols] = dv
                    dlb_ref[d, :, cols] += dlb
                    dst_ref[d, h] = dst
            return carry

        lax.fori_loop(0, ncb, chunk, 0)

    def ins(d):
        return [pl.BlockSpec((tb, hw), lambda j: (rblk(d, j), q_cb)),
                pl.BlockSpec((tb, hw), lambda j: (rblk(d, j), f_cb + d)),
                pl.BlockSpec((tb, hw), lambda j: (rblk(d, j), v_cb)),
                pl.BlockSpec((hh, ncb, LANE, LANE), lambda j: (0, rblk(d, j), 0, 0)),
                pl.BlockSpec((tb, hw), lambda j: (rblk(d, j), 0))]

    rows_of = lambda d: pl.BlockSpec((tb, hw), lambda j: (rblk(d, j), 0))
    outs, landed = _pcall(
        body, name=name, grid=(nblk,),
        in_specs=ins(0) + ins(1) + [pl.BlockSpec((2, 1, hw), lambda j: (0, 0, 0))],
        out_specs=[rows_of(0)] * 3 + [rows_of(1)] * 3 + [pl.BlockSpec((2, 1, hw), lambda j: (0, 0, 0))],
        out_shape=[S((n, hw), F32)] * 6 + [S((2, 1, hw), F32)],
        scratch_shapes=[pltpu.VMEM((2, hh, LANE, LANE), F32)],
        sem=("arbitrary",),
        args=(proj, proj, proj, hist[0], do, proj, proj, proj, hist[1], do, lb), side=side)
    res = ((outs[0], outs[3]), (outs[1], outs[4]), (outs[2], outs[5]), outs[6])
    return res if side is None else (res, landed)


def _hgc_fn(n_h):
    def fn(o0, o1, gt, g):
        osum = o0 + o1
        y = jnp.concatenate([_rms(osum[:, h * LANE:(h + 1) * LANE], g) for h in range(n_h)], axis=1)
        return y * (gt * jax.nn.sigmoid(gt))

    return fn


def _hgc_fwd(dm, name, o_dir, proj, g):
    f = _hgc_fn(dm.hg_heads)
    cb = dm.in_offs[7] // dm.hg_w
    return _rows(name, lambda i, *a: (f(*a),), dm.n_tok, dm.row_tile,
                 [R(o_dir[0]), R(o_dir[1]), R(proj, dm.hg_w, cb), C(g)], [(dm.hg_w, BF16)])[0]


def _hgc_bwd(dm, name, o_dir, proj, g, dmix):
    f = _hgc_fn(dm.hg_heads)
    cb = dm.in_offs[7] // dm.hg_w

    def fn(i, o0, o1, gt, g, dy):
        _, vjp = jax.vjp(f, o0, o1, gt, g)
        do, _, dgt, dg = vjp(dy)
        return do, dgt, dg

    return _rows(name, fn, dm.n_tok, dm.row_tile,
                 [R(o_dir[0]), R(o_dir[1]), R(proj, dm.hg_w, cb), C(g), R(dmix, dm.hg_w, dm.q_w // dm.hg_w)],
                 [(dm.hg_w, F32), (dm.hg_w, BF16)], [(1, LANE)])


def _sg_fn(n_g):
    def fn(su, sv, ng, w, bcol):
        u = jax.nn.gelu(su)
        gv = jax.nn.gelu(sv)
        outs = []
        for g in range(n_g):
            sl = slice(g * LANE, (g + 1) * LANE)
            vn = _rms(gv[:, sl], ng[:, sl])
            outs.append(_bdot_nn(w[g], vn) + bcol[g])
        return u * jnp.concatenate(outs, axis=1)

    return fn


def _sg_fwd(dm, name, proj, ng, w, bcol):
    f = _sg_fn(dm.sg_groups)
    o = dm.in_offs
    return _rows(name, lambda i, *a: (f(*a),), dm.n_tok, LANE,
                 [R(proj, dm.sg_w, o[8] // dm.sg_w), R(proj, dm.sg_w, o[9] // dm.sg_w), C(ng), C(w), C(bcol)],
                 [(dm.sg_w, BF16)])[0]


def _sg_bwd(dm, name, proj, ng, w, bcol, dmix):
    f = _sg_fn(dm.sg_groups)
    o = dm.in_offs

    def fn(i, su, sv, ng, w, bcol, dy):
        _, vjp = jax.vjp(f, su, sv, ng, w, bcol)
        return vjp(dy)

    return _rows(name, fn, dm.n_tok, LANE,
                 [R(proj, dm.sg_w, o[8] // dm.sg_w), R(proj, dm.sg_w, o[9] // dm.sg_w), C(ng), C(w), C(bcol),
                  R(dmix, dm.sg_w, (dm.q_w + dm.hg_w) // dm.sg_w)],
                 [(dm.sg_w, BF16), (dm.sg_w, BF16)],
                 [(1, dm.sg_w), (dm.sg_groups, LANE, LANE), (dm.sg_groups, LANE, 1)])


def _conv3(x, prev, nxt, w0, w1, w2, zero_prev, zero_next):
    tm = x.shape[0]
    x = x.astype(F32)
    row = lax.broadcasted_iota(jnp.int32, (tm, 1), 0)
    up = jnp.where(zero_prev, 0.0, prev[prev.shape[0] - 1:].astype(F32))
    dn = jnp.where(zero_next, 0.0, nxt[0:1].astype(F32))
    x_m1 = jnp.where(row == 0, up, pltpu.roll(x, 1, 0))
    x_p1 = jnp.where(row == tm - 1, dn, pltpu.roll(x, tm - 1, 0))
    return w0 * x_m1 + w1 * x + w2 * x_p1, x_m1, x_p1


def _conv_edges(dm, tm):
    nbc, nb = dm.ctx_len // tm, dm.n_tok // tm

    def edges(i):
        return (jnp.logical_or(i == 0, i == nbc), jnp.logical_or(i == nbc - 1, i == nb - 1))

    return edges


def _halo_rows(dtype):
    return 16 if dtype == BF16 else 8


def _halo_specs(tm, tn, n_rows, hr, pick):
    last, per = n_rows // hr - 1, tm // hr
    return [pl.BlockSpec((tm, tn), lambda *ids: pick(*ids)),
            pl.BlockSpec((hr, tn), lambda *ids: (jnp.maximum(pick(*ids)[0] * per - 1, 0), pick(*ids)[1])),
            pl.BlockSpec((hr, tn), lambda *ids: (jnp.minimum((pick(*ids)[0] + 1) * per, last), pick(*ids)[1]))]


def _conv_fwd(dm, name, up, cw, cb):
    n, ff, tm = dm.n_tok, dm.d_ff, dm.row_tile
    tn = _div_tile(ff, 1408, LANE)
    nj = ff // tn
    edges = _conv_edges(dm, tm)

    def body(g_ref, gp_ref, gn_ref, v_ref, vp_ref, vn_ref, wg_ref, wv_ref, bg_ref, bv_ref, a_ref):
        zp, zn = edges(pl.program_id(0))
        wg, wv = wg_ref[...], wv_ref[...]
        yg = _conv3(g_ref[...], gp_ref[...], gn_ref[...], wg[0:1], wg[1:2], wg[2:3], zp, zn)[0] + bg_ref[...]
        yv = _conv3(v_ref[...], vp_ref[...], vn_ref[...], wv[0:1], wv[1:2], wv[2:3], zp, zn)[0] + bv_ref[...]
        a_ref[...] = (yg * jax.nn.sigmoid(yg) * yv).astype(a_ref.dtype)

    small = lambda off: pl.BlockSpec((3, tn), lambda i, j: (0, j + off))
    bias = lambda off: pl.BlockSpec((1, tn), lambda i, j: (0, j + off))
    return pl.pallas_call(
        body, name=name, grid=(n // tm, nj),
        in_specs=_halo_specs(tm, tn, n, _halo_rows(up.dtype), lambda i, j: (i, j))
        + _halo_specs(tm, tn, n, _halo_rows(up.dtype), lambda i, j: (i, j + nj))
        + [small(0), small(nj), bias(0), bias(nj)],
        out_specs=pl.BlockSpec((tm, tn), lambda i, j: (i, j)), out_shape=S((n, ff), BF16),
        compiler_params=_cparams(("parallel", "parallel")))(up, up, up, up, up, up, cw, cw, cb, cb)


def _conv_bwd_dy(dm, name, up, cw, cb, da, side=None):
    n, ff, tm = dm.n_tok, dm.d_ff, dm.row_tile
    tn = _div_tile(ff, 1408, LANE)
    nj = ff // tn
    edges = _conv_edges(dm, tm)

    def body(g_ref, gp_ref, gn_ref, v_ref, vp_ref, vn_ref, wg_ref, wv_ref, bg_ref, bv_ref, da_ref,
             dyg_ref, dyv_ref, dwg_ref, dwv_ref, dbg_ref, dbv_ref):
        i = pl.program_id(1)
        zp, zn = edges(i)
        wg, wv = wg_ref[...], wv_ref[...]
        g, v = g_ref[...].astype(F32), v_ref[...].astype(F32)
        cg, g_m1, g_p1 = _conv3(g, gp_ref[...], gn_ref[...], wg[0:1], wg[1:2], wg[2:3], zp, zn)
        cv, v_m1, v_p1 = _conv3(v, vp_ref[...], vn_ref[...], wv[0:1], wv[1:2], wv[2:3], zp, zn)
        yg, yv = cg + bg_ref[...], cv + bv_ref[...]
        sg = jax.nn.sigmoid(yg)
        da = da_ref[...]
        dyg = da * yv * (sg * (1.0 + yg * (1.0 - sg)))
        dyv = da * (yg * sg)
        dyg_ref[...] = dyg.astype(dyg_ref.dtype)
        dyv_ref[...] = dyv.astype(dyv_ref.dtype)
        row = lax.broadcasted_iota(jnp.int32, (3, tn), 0)

        def stack3(dy, a, b, c):
            return jnp.where(row == 0, _colsum(dy * a), jnp.where(row == 1, _colsum(dy * b), _colsum(dy * c)))

        upd = [(dwg_ref, stack3(dyg, g_m1, g, g_p1)), (dwv_ref, stack3(dyv, v_m1, v, v_p1)),
               (dbg_ref, _colsum(dyg)), (dbv_ref, _colsum(dyv))]

        @pl.when(i == 0)
        def _():
            for r, val in upd:
                r[...] = val

        @pl.when(i != 0)
        def _():
            for r, val in upd:
                r[...] += val

    hs = lambda off: _halo_specs(tm, tn, n, _halo_rows(up.dtype), lambda j, i: (i, j + off))
    small = lambda off: pl.BlockSpec((3, tn), lambda j, i: (0, j + off))
    bias = lambda off: pl.BlockSpec((1, tn), lambda j, i: (0, j + off))
    blk = pl.BlockSpec((tm, tn), lambda j, i: (i, j))
    outs, landed = _pcall(
        body, name=name, grid=(nj, n // tm),
        in_specs=hs(0) + hs(nj) + [small(0), small(nj), bias(0), bias(nj), blk],
        out_specs=[blk, blk, small(0), small(0), bias(0), bias(0)],
        out_shape=[S((n, ff), BF16), S((n, ff), BF16), S((3, ff), F32), S((3, ff), F32), S((1, ff), F32),
                   S((1, ff), F32)],
        sem=("parallel", "arbitrary"), args=(up, up, up, up, up, up, cw, cw, cb, cb, da), side=side)
    return outs if side is None else (outs, landed)


def _conv_bwd_dx(dm, name, dyg, dyv, cw):
    n, ff, tm = dm.n_tok, dm.d_ff, dm.row_tile
    tn = _div_tile(ff, 1408, LANE)
    nj = ff // tn
    edges = _conv_edges(dm, tm)

    def body(g_ref, gp_ref, gn_ref, v_ref, vp_ref, vn_ref, wg_ref, wv_ref, o_ref):
        zp, zn = edges(pl.program_id(0))
        half = pl.program_id(1) // nj
        x = jnp.where(half == 0, g_ref[...], v_ref[...])
        xp = jnp.where(half == 0, gp_ref[...], vp_ref[...])
        xn = jnp.where(half == 0, gn_ref[...], vn_ref[...])
        w = jnp.where(half == 0, wg_ref[...], wv_ref[...])
        o_ref[...] = _conv3(x, xp, xn, w[2:3], w[1:2], w[0:1], zp, zn)[0].astype(o_ref.dtype)

    hr = _halo_rows(dyg.dtype)
    g_specs = _halo_specs(tm, tn, n, hr, lambda i, j: (i, jnp.minimum(j, nj - 1)))
    v_specs = _halo_specs(tm, tn, n, hr, lambda i, j: (i, jnp.maximum(j - nj, 0)))
    return pl.pallas_call(
        body, name=name, grid=(n // tm, 2 * nj),
        in_specs=g_specs + v_specs + [pl.BlockSpec((3, tn), lambda i, j: (0, jnp.minimum(j, nj - 1))),
                                      pl.BlockSpec((3, tn), lambda i, j: (0, nj + jnp.maximum(j - nj, 0)))],
        out_specs=pl.BlockSpec((tm, tn), lambda i, j: (i, j)), out_shape=S((n, 2 * ff), BF16),
        compiler_params=_cparams(("parallel", "parallel")))(dyg, dyg, dyg, dyv, dyv, dyv, cw, cw)


def _loss_head(dm, name, x, tgt, g):
    tm = dm.row_tile
    nc = dm.ctx_len // tm

    def fn(i, x, t, g):
        def f(x, g):
            err = _rms(x, g) - t
            return 0.5 * jnp.sum(jnp.mean(err * err, axis=-1, keepdims=True), axis=0, keepdims=True)

        loss, vjp = jax.vjp(f, x, g)
        dx, dg = vjp(jnp.ones((1, 1), F32))
        live = i >= nc
        return (jnp.where(live, dx, 0.0), jnp.where(live, jnp.broadcast_to(loss, (1, LANE)), 0.0),
                jnp.where(live, dg, 0.0))

    return _rows(name, fn, dm.n_tok, tm, [R(x), R(tgt, rmap=lambda i: jnp.maximum(i - nc, 0)), C(g)],
                 [(dm.d_model, F32)], [(1, LANE), (1, dm.d_model)])


def _dproj_assemble(dm, name, d_aq, d_ak, dv, dq_dir, df_dir, dv_dir, d_hgt, d_su, d_sv):
    def fn(i, d_aq, d_ak, dv, q0, q1, f0, f1, v0, v1, d_hgt, d_su, d_sv):
        parts = [d_aq, d_ak, dv, q0 + q1, f0, f1, v0 + v1, d_hgt, d_su, d_sv]
        return (jnp.concatenate([p.astype(F32) for p in parts], axis=1),)

    ins = [R(d_aq), R(d_ak), R(dv), R(dq_dir[0]), R(dq_dir[1]), R(df_dir[0]), R(df_dir[1]), R(dv_dir[0]),
           R(dv_dir[1]), R(d_hgt), R(d_su), R(d_sv)]
    return _rows(name, fn, dm.n_tok, dm.row_tile, ins, [(dm.in_cols, BF16)])[0]


def _layer_fwd(dm, l, x, h, mods, wl, tabs, blocks=None):
    ct, st = tabs
    d = dm.d_model
    w_in_next = None
    proj = _matmul("proj", h, wl["w_in"], "nn", F32, b_shards=4)
    qh, kh = _qk_fwd(dm, "qk", proj, ct, st, wl["q_g"], wl["k_g"])
    if blocks is None:
        attn = _attn_fwd(dm, "attn", qh, kh, proj)
        o_dir, hist = _hgrn_fwd(dm, "hgrn", proj, wl["lb"])
    else:
        attn, g1 = _attn_fwd(dm, "attn", qh, kh, proj, side=_gather_own_side([blocks["w_up"], blocks["w_out"]]))
        second = [blocks["w_down"]] + ([blocks["w_in_next"]] if "w_in_next" in blocks else [])
        sides = [_gather_own_side(second), _gather_pass_side(list(g1))]
        (o_dir, hist), landed = _hgrn_fwd(dm, "hgrn", proj, wl["lb"], side=_merge_sides(*sides))
        g2, g1 = _split_outs(sides, landed)
        wl = dict(wl, w_up=g1[0].reshape(4, d, -1), w_out=g1[1].reshape(dm.d_mix, d))
    hg = _hgc_fwd(dm, "hgc", o_dir, proj, wl["hg_g"])
    sg = _sg_fwd(dm, "sg", proj, wl["sg_g"], wl["sg_w"], wl["sg_bcol"])
    mix = jnp.concatenate([attn, hg, sg], axis=1)
    m = _matmul("out", mix, wl["w_out"], "nn", F32)
    x1, h2 = _resnorm_fwd(dm, "resnorm2", x, m, mods[2], wl["norm2_g"], mods[3], mods[4])
    if blocks is None:
        up = _matmul("up", h2, wl["w_up"], "nn", BF16, b_shards=4)
    else:
        up, g2 = _matmul("up", h2, wl["w_up"], "nn", BF16, b_shards=4, side=_gather_pass_side(list(g2)))
        wl = dict(wl, w_down=g2[0].reshape(dm.d_ff, d))
        if len(g2) > 1:
            w_in_next = g2[1].reshape(4, d, -1)
    a = _conv_fwd(dm, "conv", up, wl["conv_w"], wl["conv_b"])
    f = _matmul("down", a, wl["w_down"], "nn", F32, caps=_CAPS_DOWN)
    saved = dict(x=x, h=h, proj=proj, qh=qh, kh=kh, attn=attn, o_dir=o_dir, hist=hist, mix=mix, m=m, x1=x1, h2=h2,
                 up=up, a=a, f=f)
    return x1, f, saved, wl, w_in_next


def _blocks42(g):
    return g.reshape(4, 2, -1, g.shape[-1])


def _layer_bwd(dm, l, dx2, sv, mods, wl, tabs, rs=None):
    ct, st = tabs
    g = {}
    df, g["mod5"] = _gate_bwd(dm, "b_gate5", dx2, sv["f"], mods[5])
    da = _matmul("b_da", df, wl["w_down"], "nt", F32)
    g["w_down"] = _matmul("b_wdown", sv["a"], df, "tn", BF16, caps=_CAPS_TOKENS_ONE)
    if rs is None:
        dyg, dyv, dwg, dwv, dbg, dbv = _conv_bwd_dy(dm, "b_convdy", sv["up"], wl["conv_w"], wl["conv_b"], da)
    else:
        jobs1 = [((l, "w_down"), _blocks42(g.pop("w_down")))] + rs["pending"]
        rs["pending"] = []
        (dyg, dyv, dwg, dwv, dbg, dbv), recv = _conv_bwd_dy(dm, "b_convdy", sv["up"], wl["conv_w"], wl["conv_b"], da,
                                                             side=_swap_side([gb for _, gb in jobs1]))
        ps1 = [_pair_sum("rs_sum2", gb, r) for (_, gb), r in zip(jobs1, recv)]
    g["conv_w"] = jnp.concatenate([dwg, dwv], axis=1)
    g["conv_b"] = jnp.concatenate([dbg, dbv], axis=1)
    d_up = _conv_bwd_dx(dm, "b_convdx", dyg, dyv, wl["conv_w"])
    if rs is None:
        dh2 = _matmul("b_dh2", d_up, wl["w_up"], "nt", F32, b_shards=4, caps=_CAPS_DH2)
    else:
        dh2, recv = _matmul("b_dh2", d_up, wl["w_up"], "nt", F32, b_shards=4, caps=_CAPS_DH2, side=_xchg_side(ps1))
        ts1 = [_chip_sum("rs_sum4", p, r) for p, r in zip(ps1, recv)]
    g["w_up"] = _matmul("b_wup", sv["h2"], d_up, "tn", BF16, out_shards=4, caps=_CAPS_TOKENS_NARROW)
    dx1, g["norm2_g"], g["mod3"], g["mod4"] = _normmod_bwd(dm, "b_norm2", sv["x1"], dh2, dx2, wl["norm2_g"],
                                                             mods[3], mods[4])
    dmv, g["mod2"] = _gate_bwd(dm, "b_gate2", dx1, sv["m"], mods[2])
    dmix = _matmul("b_dmix", dmv, wl["w_out"], "nt", F32)
    g["w_out"] = _matmul("b_wout", sv["mix"], dmv, "tn", BF16, caps=_CAPS_TOKENS_ONE)
    proj = sv["proj"]
    if rs is None:
        dqh, dkh, dv = _attn_bwd(dm, "b_attn", sv["qh"], sv["kh"], proj, sv["attn"], dmix)
    else:
        jobs2 = [((l, "w_up"), _blocks42(g.pop("w_up"))), ((l, "w_out"), _blocks42(g.pop("w_out")))]
        sides = [_share_side(ts1), _swap_side([gb for _, gb in jobs2])]
        (dqh, dkh, dv), landed = _attn_bwd(dm, "b_attn", sv["qh"], sv["kh"], proj, sv["attn"], dmix,
                                           side=_merge_sides(*sides))
        fin, recv = _split_outs(sides, landed)
        for (key, gb), t in zip(jobs1, fin):
            rs["done"][key] = t.reshape(-1, t.shape[-1])
        ps2 = [_pair_sum("rs_sum2", gb, r) for (_, gb), r in zip(jobs2, recv)]
    d_aq, d_ak, g["q_g"], g["k_g"] = _qk_bwd(dm, "b_qk", proj, ct, st, wl["q_g"], wl["k_g"], dqh, dkh)
    do, d_hgt, g["hg_g"] = _hgc_bwd(dm, "b_hgc", sv["o_dir"], proj, wl["hg_g"], dmix)
    if rs is None:
        dq_dir, df_dir, dv_dir, g["lb"] = _hgrn_bwd(dm, "b_hgrn", proj, wl["lb"], sv["hist"], do)
    else:
        (dq_dir, df_dir, dv_dir, g["lb"]), recv = _hgrn_bwd(dm, "b_hgrn", proj, wl["lb"], sv["hist"], do,
                                                            side=_xchg_side(ps2))
        ts2 = [_chip_sum("rs_sum4", p, r) for p, r in zip(ps2, recv)]
    d_su, d_sv, g["sg_g"], g["sg_w"], g["sg_bcol"] = _sg_bwd(dm, "b_sg", proj, wl["sg_g"], wl["sg_w"],
                                                            wl["sg_bcol"], dmix)
    dproj = _dproj_assemble(dm, "b_dproj", d_aq, d_ak, dv, dq_dir, df_dir, dv_dir, d_hgt, d_su, d_sv)
    if rs is None:
        dh = _matmul("b_dh", dproj, wl["w_in"], "nt", F32, b_shards=4)
    else:
        dh, fin = _matmul("b_dh", dproj, wl["w_in"], "nt", F32, b_shards=4, side=_share_side(ts2))
        for (key, gb), t in zip(jobs2, fin):
            rs["done"][key] = t.reshape(-1, t.shape[-1])
    g["w_in"] = _matmul("b_win", sv["h"], dproj, "tn", BF16, out_shards=4, caps=_CAPS_TOKENS_NARROW)
    if rs is not None:
        rs["pending"] = [((l, "w_in"), _blocks42(g.pop("w_in")))]
    dx, g["norm1_g"], g["mod0"], g["mod1"] = _normmod_bwd(dm, "b_norm1", sv["x"], dh, dx1, wl["norm1_g"],
                                                           mods[0], mods[1])
    return dx, g


def _sample_step(dm, x_all, tgt, mods, wls, final_g, tabs, half_blocks=None, rs=None):
    wls = [dict(wl) for wl in wls]
    if half_blocks is not None:
        wls[0]["w_in"] = _all_gather8("ag_w_in", half_blocks[0]["w_in"]).reshape(4, dm.d_model, -1)
    saved = []
    x = x_all
    h = _normmod_fwd(dm, "norm1", x, wls[0]["norm1_g"], mods[0][0], mods[0][1])
    for l in range(dm.depth):
        blocks = None
        if half_blocks is not None:
            blocks = {k: half_blocks[l][k] for k in ("w_up", "w_out", "w_down")}
            if l + 1 < dm.depth:
                blocks["w_in_next"] = half_blocks[l + 1]["w_in"]
        x1, f, sv, wls[l], w_in_next = _layer_fwd(dm, l, x, h, mods[l], wls[l], tabs, blocks=blocks)
        if w_in_next is not None:
            wls[l + 1]["w_in"] = w_in_next
        saved.append(sv)
        if l + 1 < dm.depth:
            x, h = _resnorm_fwd(dm, "resnorm1", x1, f, mods[l][5], wls[l + 1]["norm1_g"], mods[l + 1][0],
                                mods[l + 1][1])
        else:
            x = _res_fwd(dm, "res", x1, f, mods[l][5])
    dx, loss, dfg = _loss_head(dm, "loss_head", x, tgt, final_g)
    grads = [None] * dm.depth
    for l in reversed(range(dm.depth)):
        dx, grads[l] = _layer_bwd(dm, l, dx, saved[l], mods[l], wls[l], tabs, rs=rs)
    if rs is not None:
        for key, gb in rs["pending"]:
            rs["done"][key] = _reduce_scatter_grad("rs_tail", gb)
        rs["pending"] = []
    return loss, dx, grads, dfg


def _all_gather8(name, blk):
    r, cdim = blk.shape

    def body(x_ref, out_ref, send_sems, recv_sems, local_sem):
        x, y, c = _place()
        me, sibling = (x, y, c), (x, y, 1 - c)
        chips = [(1 - x, y), (x, 1 - y), (1 - x, 1 - y)]

        def slot(px, py, pc):
            return out_ref.at[4 * px + 2 * py + pc]

        def copy(k, block, to, src=None):
            return pltpu.make_async_remote_copy(
                src_ref=slot(*block) if src is None else src, dst_ref=slot(*block),
                send_sem=send_sems.at[k], recv_sem=recv_sems.at[k], device_id=to, device_id_type=MESH)

        mine = pltpu.make_async_copy(x_ref, slot(*me), local_sem)
        mine.start()
        first = [copy(0, me, sibling, src=x_ref)]
        first += [copy(1 + j, me, (*chip, c), src=x_ref) for j, chip in enumerate(chips)]
        for cp in first:
            cp.start()
        passed = [copy(4 + j, (*chip, c), sibling) for j, chip in enumerate(chips)]
        for j, chip in enumerate(chips):
            copy(1 + j, (*chip, c), me).wait_recv()
            passed[j].start()
        copy(0, sibling, me).wait_recv()
        for j, chip in enumerate(chips):
            copy(4 + j, (*chip, 1 - c), me).wait_recv()
        for cp in first + passed:
            cp.wait_send()
        mine.wait()

    return pl.pallas_call(
        body, name=name, out_shape=S((8, r, cdim), blk.dtype), in_specs=[_ANY], out_specs=_ANY,
        scratch_shapes=[pltpu.SemaphoreType.DMA((7,)), pltpu.SemaphoreType.DMA((7,)), pltpu.SemaphoreType.DMA])(blk)


def _pair_swap_halves(name, g):
    n_s, _, r, cdim = g.shape

    def body(g_ref, recv_ref, send_sems, recv_sems):
        x, y, c = _place()
        remote = [pltpu.make_async_remote_copy(src_ref=g_ref.at[s, 1 - c], dst_ref=recv_ref.at[s],
                                               send_sem=send_sems.at[s], recv_sem=recv_sems.at[s],
                                               device_id=(x, y, 1 - c), device_id_type=MESH) for s in range(n_s)]
        for cp in remote:
            cp.start()
        for cp in remote:
            cp.wait()

    return pl.pallas_call(
        body, name=name, out_shape=S((n_s, r, cdim), g.dtype), in_specs=[_ANY], out_specs=_ANY,
        scratch_shapes=[pltpu.SemaphoreType.DMA((n_s,))] * 2)(g)


def _pair_sum(name, g, recv):
    n_s, _, r, cdim = g.shape
    tm = _ew_tile(r, cdim, 4)

    def body(g0_ref, g1_ref, r_ref, o_ref):
        own = jnp.where(lax.axis_index("c") == 0, g0_ref[...].astype(F32), g1_ref[...].astype(F32))
        o_ref[...] = (own + r_ref[...].astype(F32)).astype(o_ref.dtype)

    return pl.pallas_call(
        body, name=name, grid=(n_s, r // tm),
        in_specs=[pl.BlockSpec((None, None, tm, cdim), lambda s, i: (s, 0, i, 0)),
                  pl.BlockSpec((None, None, tm, cdim), lambda s, i: (s, 1, i, 0)),
                  pl.BlockSpec((None, tm, cdim), lambda s, i: (s, i, 0))],
        out_specs=pl.BlockSpec((None, tm, cdim), lambda s, i: (s, i, 0)), out_shape=S((n_s, r, cdim), BF16),
        compiler_params=_cparams(("parallel", "parallel")))(g, g, recv)


def _chip_exchange(name, p):
    _, r, cdim = p.shape

    def body(p_ref, recv_ref, send_sems, recv_sems):
        x, y, c = _place()
        peers = [(1 - x, y), (x, 1 - y), (1 - x, 1 - y)]
        remote = [pltpu.make_async_remote_copy(src_ref=p_ref.at[2 * px + py], dst_ref=recv_ref.at[k],
                                               send_sem=send_sems.at[k], recv_sem=recv_sems.at[k],
                                               device_id=(px, py, c), device_id_type=MESH)
                  for k, (px, py) in enumerate(peers)]
        for cp in remote:
            cp.start()
        for cp in remote:
            cp.wait()

    return pl.pallas_call(
        body, name=name, out_shape=S((3, r, cdim), p.dtype), in_specs=[_ANY], out_specs=_ANY,
        scratch_shapes=[pltpu.SemaphoreType.DMA((3,)), pltpu.SemaphoreType.DMA((3,))])(p)


def _chip_sum(name, p, recv):
    n_s, r, cdim = p.shape
    tm = _ew_tile(r, cdim, 9)

    def body(*refs):
        chip = 2 * lax.axis_index("x") + lax.axis_index("y")
        own = refs[n_s - 1][...].astype(F32)
        for s in range(n_s - 2, -1, -1):
            own = jnp.where(chip == s, refs[s][...].astype(F32), own)
        r0, r1, r2, o_ref = refs[n_s:]
        tot = ((own + r0[...].astype(F32)) + r1[...].astype(F32)) + r2[...].astype(F32)
        o_ref[0] = tot
        o_ref[1] = tot

    blk = lambda s: pl.BlockSpec((None, tm, cdim), functools.partial(lambda i, s: (s, i, 0), s=s))
    return pl.pallas_call(
        body, name=name, grid=(r // tm,), in_specs=[blk(s) for s in range(n_s)] + [blk(k) for k in range(3)],
        out_specs=pl.BlockSpec((2, tm, cdim), lambda i: (0, i, 0)), out_shape=S((2, r, cdim), F32),
        compiler_params=_cparams(("parallel",)))(*([p] * n_s), *([recv] * 3))


def _pair_share(name, t2):
    def body(t_ref, out_ref, send_sem, recv_sem):
        x, y, c = _place()
        remote = pltpu.make_async_remote_copy(src_ref=out_ref.at[c], dst_ref=out_ref.at[c], send_sem=send_sem,
                                              recv_sem=recv_sem, device_id=(x, y, 1 - c), device_id_type=MESH)
        remote.start()
        remote.wait()

    return pl.pallas_call(
        body, name=name, out_shape=S(t2.shape, t2.dtype), in_specs=[_ANY], out_specs=_ANY,
        input_output_aliases={0: 0},
        scratch_shapes=[pltpu.SemaphoreType.DMA, pltpu.SemaphoreType.DMA])(t2)


def _ew_tile(rows, cols, n_arrays):
    cap = min(1024, max(16, (24 * 1024 * 1024) // (n_arrays * 2 * cols * 4)))
    if rows <= 16:
        return rows
    mult = 16 if any(rows % t == 0 for t in range(16, cap + 1, 16)) else 8
    return _div_tile(rows, cap, mult)


def _reduce_scatter_grad(name, gb):
    _, _, r, cdim = gb.shape
    p = _pair_sum(name + "_sum2", gb, _pair_swap_halves(name + "_swap", gb))
    tot2 = _chip_sum(name + "_sum4", p, _chip_exchange(name + "_xchg", p))
    return _pair_share(name + "_share", tot2).reshape(2 * r, cdim)


def _ada_fwd(name, a16, w_ada, b_cols):
    depth, d, cols = w_ada.shape
    tn = _div_tile(cols, 1536, LANE)

    def body(a_ref, w_ref, b_ref, o_ref):
        a = a_ref[...]
        o_ref[...] = _dg(a * jax.nn.sigmoid(a), w_ref[...], _NN) + b_ref[...]

    return pl.pallas_call(
        body, name=name, grid=(depth, cols // tn),
        in_specs=[pl.BlockSpec((16, d), lambda l, j: (0, 0)), pl.BlockSpec((None, d, tn), lambda l, j: (l, 0, j)),
                  pl.BlockSpec((None, 1, tn), lambda l, j: (l, 0, j))],
        out_specs=pl.BlockSpec((None, 16, tn), lambda l, j: (l, 0, j)), out_shape=S((depth, 16, cols), F32),
        compiler_params=_cparams(("parallel", "parallel")))(a16, w_ada, b_cols)


def _ada_bwd_w(name, a_t, dmod):
    depth, _, cols = dmod.shape
    d = a_t.shape[0]
    tm, tn = _div_tile(d, 512, 8), _div_tile(cols, 1536, LANE)

    def body(a_ref, g_ref, o_ref):
        a = a_ref[...]
        o_ref[...] = _dg(a * jax.nn.sigmoid(a), g_ref[...], _NN)

    return pl.pallas_call(
        body, name=name, grid=(depth, d // tm, cols // tn),
        in_specs=[pl.BlockSpec((tm, 16), lambda l, i, j: (i, 0)), pl.BlockSpec((None, 16, tn), lambda l, i, j: (l, 0, j))],
        out_specs=pl.BlockSpec((None, tm, tn), lambda l, i, j: (l, i, j)), out_shape=S((depth, d, cols), F32),
        compiler_params=_cparams(("parallel", "parallel", "parallel")))(a_t, dmod)


def _ada_bwd_a(name, dmod, w_ada):
    depth, d, cols = w_ada.shape
    tn = _div_tile(cols, 1536, LANE)
    nj = cols // tn

    def body(g_ref, w_ref, o_ref):
        first = jnp.logical_and(pl.program_id(0) == 0, pl.program_id(1) == 0)
        p = _dg(g_ref[...], w_ref[...], _NT)

        @pl.when(first)
        def _():
            o_ref[...] = p

        @pl.when(jnp.logical_not(first))
        def _():
            o_ref[...] += p

    return pl.pallas_call(
        body, name=name, grid=(depth, nj),
        in_specs=[pl.BlockSpec((None, 16, tn), lambda l, j: (l, 0, j)), pl.BlockSpec((None, d, tn), lambda l, j: (l, 0, j))],
        out_specs=pl.BlockSpec((16, d), lambda l, j: (0, 0)), out_shape=S((16, d), F32),
        compiler_params=_cparams(("arbitrary", "arbitrary")))(dmod, w_ada)


def _lbs_fn(p):
    depth = p.shape[0]
    rows = [p[l] for l in range(depth)]
    mx = functools.reduce(jnp.maximum, rows)
    ex = [jnp.exp(r - mx) for r in rows]
    den = functools.reduce(lambda a, b: a + b, ex)
    sm = [e / den for e in ex]
    out, run = [], None
    for l in range(depth):
        run = sm[l] if run is None else run + sm[l]
        out.append(run - sm[0])
    return jnp.stack(out, axis=0)


def _lbs_fwd(name, p):
    def body(p_ref, o_ref):
        o_ref[...] = _lbs_fn(p_ref[...])

    return pl.pallas_call(body, name=name, out_shape=S(p.shape, F32))(p)


def _lbs_bwd(name, p, d_out):
    def body(p_ref, g_ref, o_ref):
        _, vjp = jax.vjp(_lbs_fn, p_ref[...])
        o_ref[...] = vjp(g_ref[...])[0]

    return pl.pallas_call(body, name=name, out_shape=S(p.shape, F32))(p, d_out)


def _sum8(name, g):
    _, r, cdim = g.shape
    tm = _ew_tile(r, cdim, 9)

    def body(g_ref, o_ref):
        acc = g_ref[0]
        for k in range(1, 8):
            acc = acc + g_ref[k]
        o_ref[...] = acc

    return pl.pallas_call(body, name=name, grid=(r // tm,),
                          in_specs=[pl.BlockSpec((8, tm, cdim), lambda i: (0, i, 0))],
                          out_specs=pl.BlockSpec((tm, cdim), lambda i: (i, 0)), out_shape=S((r, cdim), F32),
                          compiler_params=_cparams(("parallel",)))(g)


def _adamw(name, w, m, v, g):
    rows, cols = w.shape
    tm = _ew_tile(rows, cols, 7)

    def fn(i, w, m, v, g):
        m = ADAM_B1 * m + (1.0 - ADAM_B1) * g
        v = ADAM_B2 * v + (1.0 - ADAM_B2) * jnp.square(g)
        m_hat = m / (1.0 - ADAM_B1 ** ADAM_STEP)
        v_hat = v / (1.0 - ADAM_B2 ** ADAM_STEP)
        return -ADAM_LR * (m_hat / (jnp.sqrt(v_hat) + ADAM_EPS) + ADAM_WD * w), m, v

    return _rows(name, fn, rows, tm, [R(w), R(m), R(v), R(g)], [(cols, F32)] * 3)


def _silu_grad_mul(name, g, z):
    def body(g_ref, z_ref, o_ref):
        zz = z_ref[...]
        sg = jax.nn.sigmoid(zz)
        o_ref[...] = g_ref[...] * (sg * (1.0 + zz * (1.0 - sg)))

    return pl.pallas_call(body, name=name, out_shape=S(g.shape, F32))(g, z)


def _pack(arrs):
    parts, meta, off = [], [], 0
    for a in arrs:
        n = int(np.prod(a.shape))
        rows = -(-n // (8 * LANE)) * 8
        flat = a.reshape(-1).astype(F32)
        parts.append(jnp.pad(flat, (0, rows * LANE - n)).reshape(rows, LANE))
        meta.append((off, rows, n, a.shape))
        off += rows
    return jnp.concatenate(parts, axis=0), meta


def _unpack(buf, meta, lead=()):
    out = []
    for off, rows, n, shape in meta:
        seg = buf[..., off:off + rows, :].reshape(*lead, rows * LANE)[..., :n]
        out.append(seg.reshape(*lead, *shape))
    return out


_SMALL = ("c_ctx", "b_ada", "norm1_g", "q_norm_g", "k_norm_g", "hg_lower_bounds", "hg_norm_g", "sg_norm_g", "sg_w",
          "sg_b", "norm2_g", "conv_w", "conv_b", "final_norm_g")
_BIG = ("w_ada", "w_in", "w_out", "w_up", "w_down")
_WEIGHTS = ("c_ctx", "w_ada", "b_ada", "norm1_g", "w_in", "q_norm_g", "k_norm_g", "hg_lower_bounds", "hg_norm_g",
            "sg_norm_g", "sg_w", "sg_b", "w_out", "norm2_g", "w_up", "conv_w", "conv_b", "w_down", "final_norm_g")


def _dims_of(x, ctx, w_in, w_down):
    return Dims(d_model=x.shape[-1], seq=x.shape[1], ctx_len=ctx.shape[1], depth=w_in.shape[0],
                d_ff=w_down.shape[1] * 4)


def _step(dm, x, c, ctx, tgt, w, m, v):
    d, depth = dm.d_model, dm.depth
    xi, yi, ci = _place()
    chip = 2 * xi + yi
    me = 4 * xi + 2 * yi + ci
    n_chips = 4
    take_chips = lambda g8: g8[0::2]

    small_in, meta_in = _pack([c, w["conv_w"], w["hg_lower_bounds"]])
    gath = _all_gather8("ag_small_in", small_in)
    c_all, conv_sh, lb_sh = _unpack(gath, meta_in, lead=(8,))
    c_all = c_all.reshape(8, d)
    conv_w = take_chips(conv_sh).transpose(1, 2, 0, 3).reshape(depth, 3, 2 * dm.d_ff)
    lb_logits = take_chips(lb_sh).transpose(1, 2, 0, 3).reshape(2, depth, dm.hg_w)
    lb_p = lb_logits.transpose(1, 0, 2)
    lbs = _lbs_fwd("lbs_fwd", lb_p)

    a16 = jnp.concatenate([c_all, w["c_ctx"][None], jnp.zeros((7, d), F32)], axis=0)
    cols = w["w_ada"].shape[-1]
    b_cols = lax.dynamic_slice_in_dim(w["b_ada"], chip * cols, cols, axis=1)[:, None, :]
    mod_sh = _ada_fwd("ada_fwd", a16, w["w_ada"], b_cols)
    mod_g = take_chips(_all_gather8("ag_mod", mod_sh.reshape(depth * 16, cols)))
    mod_all = mod_g.reshape(n_chips, depth, 16, cols).transpose(1, 2, 0, 3).reshape(depth, 16, n_chips * cols)
    mod_lat = lax.dynamic_index_in_dim(mod_all, me, axis=1, keepdims=False)
    mod_ctx = mod_all[:, 8]
    mods = [[jnp.stack([mod_ctx[l, k * d:(k + 1) * d], mod_lat[l, k * d:(k + 1) * d]]) for k in range(N_MOD)]
            for l in range(depth)]

    def my_half(shard):
        half = shard.shape[0] // 2
        return lax.dynamic_slice_in_dim(shard, ci * half, half, axis=0).astype(BF16)

    wls, half_blocks = [], []
    for l in range(depth):
        half_blocks.append({name: my_half(w[name][l]) for name in ("w_in", "w_up", "w_out", "w_down")})
        wls.append(dict(
            conv_w=conv_w[l], conv_b=w["conv_b"][l][None], norm1_g=w["norm1_g"][l][None],
            norm2_g=w["norm2_g"][l][None], q_g=w["q_norm_g"][l][None], k_g=w["k_norm_g"][l][None],
            hg_g=w["hg_norm_g"][l][None], sg_g=w["sg_norm_g"][l][None], sg_w=w["sg_w"][l],
            sg_bcol=w["sg_b"][l][:, :, None], lb=lbs[l].reshape(2, 1, dm.hg_w)))

    x_all = jnp.concatenate([ctx[0], x[0]], axis=0)
    rs = dict(pending=[], done={})
    loss_row, dx_all, grads, dfg = _sample_step(dm, x_all, tgt[0], mods, wls, w["final_norm_g"][None],
                                                _rope_tables(dm), half_blocks=half_blocks, rs=rs)
    loss = lax.psum(loss_row[0, 0], ("x", "y", "c"))
    grad_x = dx_all[dm.ctx_len:][None]

    g_big = {name: jnp.stack([rs["done"][(l, name)] for l in range(depth)])
             for name in ("w_in", "w_up", "w_out", "w_down")}

    dmod_lat = jnp.stack([jnp.concatenate([grads[l][f"mod{k}"][1] for k in range(N_MOD)]) for l in range(depth)])
    dmod_ctx = jnp.stack([jnp.concatenate([grads[l][f"mod{k}"][0] for k in range(N_MOD)]) for l in range(depth)])
    d_lbs = jnp.stack([grads[l]["lb"].reshape(2, dm.hg_w) for l in range(depth)])
    d_lb_p = _lbs_bwd("lbs_bwd", lb_p, d_lbs).transpose(1, 0, 2)
    stk = lambda key: jnp.stack([grads[l][key] for l in range(depth)])
    part = {
        "b_ada": dmod_lat + dmod_ctx, "norm1_g": stk("norm1_g")[:, 0], "q_norm_g": stk("q_g")[:, 0],
        "k_norm_g": stk("k_g")[:, 0], "hg_lower_bounds": d_lb_p, "hg_norm_g": stk("hg_g")[:, 0],
        "sg_norm_g": stk("sg_g")[:, 0], "sg_w": stk("sg_w"), "sg_b": stk("sg_bcol")[..., 0],
        "norm2_g": stk("norm2_g")[:, 0], "conv_w": stk("conv_w"), "conv_b": stk("conv_b")[:, 0],
        "final_norm_g": dfg[0]}
    names = [n for n in _SMALL if n != "c_ctx"]
    packed, meta = _pack([part[n] for n in names] + [dmod_ctx, dmod_lat])
    gath = _all_gather8("ag_small_grads", packed)
    summed = _unpack(_sum8("sum_small_grads", gath), meta)
    g_small = dict(zip(names, summed[:len(names)]))
    dmod_ctx_tot = summed[len(names)]
    dmod_lat_all = _unpack(gath, meta[-1:], lead=(8,))[0]

    dmod16 = jnp.concatenate([dmod_lat_all.transpose(1, 0, 2), dmod_ctx_tot[:, None], jnp.zeros((depth, 7, 6 * d), F32)],
                             axis=1)
    dmod16 = lax.dynamic_slice_in_dim(dmod16, chip * cols, cols, axis=2)
    g_big["w_ada"] = _ada_bwd_w("ada_bwd_w", a16.T, dmod16)
    da16 = _ada_bwd_a("ada_bwd_a", dmod16, w["w_ada"])
    da_g = take_chips(_all_gather8("ag_dctx", da16))
    da_sum = _rows("sum_dctx", lambda i, a, b, c2, d2: (((a + b) + c2) + d2,), 16, 16,
                   [R(da_g[k]) for k in range(n_chips)], [(d, F32)])[0]
    g_small["c_ctx"] = _silu_grad_mul("dctx_silu", da_sum[8:9], w["c_ctx"][None])[0]

    g_small["conv_w"] = lax.dynamic_slice_in_dim(g_small["conv_w"], chip * w["conv_w"].shape[-1], w["conv_w"].shape[-1], axis=2)
    g_small["hg_lower_bounds"] = lax.dynamic_slice_in_dim(g_small["hg_lower_bounds"], chip * w["hg_lower_bounds"].shape[-1],
                                                          w["hg_lower_bounds"].shape[-1], axis=2)

    grads_out, deltas, new_m, new_v = {}, {}, {}, {}
    for name in _BIG:
        shp = w[name].shape
        flat = lambda a: a.reshape(-1, shp[-1])
        dl, nm, nv = _adamw(f"adamw_{name}", flat(w[name]), flat(m[name]), flat(v[name]), flat(g_big[name]))
        grads_out[name] = g_big[name].reshape(shp)
        deltas[name], new_m[name], new_v[name] = dl.reshape(shp), nm.reshape(shp), nv.reshape(shp)
    pw, meta_s = _pack([w[n] for n in _SMALL])
    pm, _ = _pack([m[n] for n in _SMALL])
    pv, _ = _pack([v[n] for n in _SMALL])
    pg, _ = _pack([g_small[n].reshape(w[n].shape) for n in _SMALL])
    dl, nm, nv = _adamw("adamw_small", pw, pm, pv, pg)
    for name, a, b, c2 in zip(_SMALL, _unpack(dl, meta_s), _unpack(nm, meta_s), _unpack(nv, meta_s)):
        grads_out[name] = g_small[name].reshape(w[name].shape)
        deltas[name], new_m[name], new_v[name] = a, b, c2
    return loss, grad_x, grads_out, deltas, new_m, new_v


def kernel(x, c, ctx, c_ctx, w_ada, b_ada, norm1_g, w_in, q_norm_g, k_norm_g, hg_lower_bounds, hg_norm_g, sg_norm_g, sg_w, sg_b, w_out, norm2_g, w_up, conv_w, conv_b, w_down, final_norm_g, loss_target, m_c_ctx, m_w_ada, m_b_ada, m_norm1_g, m_w_in, m_q_norm_g, m_k_norm_g, m_hg_lower_bounds, m_hg_norm_g, m_sg_norm_g, m_sg_w, m_sg_b, m_w_out, m_norm2_g, m_w_up, m_conv_w, m_conv_b, m_w_down, m_final_norm_g, v_c_ctx, v_w_ada, v_b_ada, v_norm1_g, v_w_in, v_q_norm_g, v_k_norm_g, v_hg_lower_bounds, v_hg_norm_g, v_sg_norm_g, v_sg_w, v_sg_b, v_w_out, v_norm2_g, v_w_up, v_conv_w, v_conv_b, v_w_down, v_final_norm_g):
    w = dict(c_ctx=c_ctx, w_ada=w_ada, b_ada=b_ada, norm1_g=norm1_g, w_in=w_in, q_norm_g=q_norm_g, k_norm_g=k_norm_g, hg_lower_bounds=hg_lower_bounds, hg_norm_g=hg_norm_g, sg_norm_g=sg_norm_g, sg_w=sg_w, sg_b=sg_b, w_out=w_out, norm2_g=norm2_g, w_up=w_up, conv_w=conv_w, conv_b=conv_b, w_down=w_down, final_norm_g=final_norm_g)
    m = dict(c_ctx=m_c_ctx, w_ada=m_w_ada, b_ada=m_b_ada, norm1_g=m_norm1_g, w_in=m_w_in, q_norm_g=m_q_norm_g, k_norm_g=m_k_norm_g, hg_lower_bounds=m_hg_lower_bounds, hg_norm_g=m_hg_norm_g, sg_norm_g=m_sg_norm_g, sg_w=m_sg_w, sg_b=m_sg_b, w_out=m_w_out, norm2_g=m_norm2_g, w_up=m_w_up, conv_w=m_conv_w, conv_b=m_conv_b, w_down=m_w_down, final_norm_g=m_final_norm_g)
    v = dict(c_ctx=v_c_ctx, w_ada=v_w_ada, b_ada=v_b_ada, norm1_g=v_norm1_g, w_in=v_w_in, q_norm_g=v_q_norm_g, k_norm_g=v_k_norm_g, hg_lower_bounds=v_hg_lower_bounds, hg_norm_g=v_hg_norm_g, sg_norm_g=v_sg_norm_g, sg_w=v_sg_w, sg_b=v_sg_b, w_out=v_w_out, norm2_g=v_norm2_g, w_up=v_w_up, conv_w=v_conv_w, conv_b=v_conv_b, w_down=v_w_down, final_norm_g=v_final_norm_g)
    dm = _dims_of(x, ctx, w_in, w_down)
    loss, grad_x, g, dl, nm, nv = _step(dm, x, c, ctx, loss_target, w, m, v)
    return (loss, grad_x, *[g[n] for n in _WEIGHTS], *[dl[n] for n in _WEIGHTS], *[nm[n] for n in _WEIGHTS],
            *[nv[n] for n in _WEIGHTS])
```
